```python
import math
import jax, jax.numpy as jnp
from jax import lax
import numpy as np

D_MODEL = 1024
BATCH = 4
SEQ = 4096
DEPTH = 1

MOBA_HEADS = 8
MOBA_HEAD_DIM = 128
MOBA_BLOCK = 256
MOBA_TOPK = 3
MOBA_Q_CHUNK = 64
DN_HEADS = 8
DN_HEAD_DIM = 128
DN_CONV = 4
DN_CHUNK = 64
MEM_LEN = 256
MEM_HEADS = 4
MEM_HEAD_DIM = 256
N_BRANCHES = 3
BRANCH_W = 1024
PEER_HEADS = 8
PEER_N_KEYS = 128
PEER_N_EXPERTS = PEER_N_KEYS * PEER_N_KEYS
PEER_QUERY_DIM = 256
PEER_TOPK = 16
PEER_TOKEN_CHUNK = 128

ROPE_THETA = 10000.0
EPS = 1e-6

MOBA_W = MOBA_HEADS * MOBA_HEAD_DIM
DN_W = DN_HEADS * DN_HEAD_DIM
MEM_W = MEM_HEADS * MEM_HEAD_DIM
IN_SIZES = (3 * MOBA_W, 3 * DN_W, DN_HEADS, DN_HEADS, DN_W, MEM_W, N_BRANCHES * D_MODEL)
IN_WIDTH = 3 * MOBA_W + 3 * DN_W + 2 * DN_HEADS + DN_W + MEM_W + N_BRANCHES * D_MODEL

kernel_name = "hybrid_moba_deltanet_mem_peer"


def rms_norm(x, g):
    xf = x.astype(jnp.float32)
    y = xf * lax.rsqrt(jnp.mean(xf * xf, axis=-1, keepdims=True) + EPS)
    return (y * g.astype(jnp.float32)).astype(x.dtype)


def l2_norm(x):
    xf = x.astype(jnp.float32)
    return xf * lax.rsqrt(jnp.sum(xf * xf, axis=-1, keepdims=True) + EPS)


def rope(x, pos):
    hd = x.shape[-1]
    inv = 1.0 / (ROPE_THETA ** (jnp.arange(0, hd, 2, dtype=jnp.float32) / hd))
    ang = pos.astype(jnp.float32)[:, None] * inv[None, :]
    cos, sin = jnp.cos(ang), jnp.sin(ang)
    xf = x.astype(jnp.float32)
    x1, x2 = xf[..., : hd // 2], xf[..., hd // 2:]
    return jnp.concatenate([x1 * cos - x2 * sin, x2 * cos + x1 * sin], axis=-1).astype(x.dtype)


def split_heads(t, n_heads):
    B, S, _ = t.shape
    return t.reshape(B, S, n_heads, -1).transpose(0, 2, 1, 3)


def merge_heads(t):
    B, H, S, hd = t.shape
    return t.transpose(0, 2, 1, 3).reshape(B, S, H * hd)


def causal_depthwise_conv(x, w):
    K, C = w.shape
    return lax.conv_general_dilated(x, w[:, None, :], window_strides=(1,), padding=[(K - 1, 0)],
                                    dimension_numbers=('NWC', 'WIO', 'NWC'), feature_group_count=C)


def moba_attention(q, k, v):
    B, H, S, hd = q.shape
    L = MOBA_BLOCK
    nb = -(-S // L)
    k_sel = min(MOBA_TOPK, nb)
    pad = ((0, 0), (0, 0), (0, nb * L - S), (0, 0))
    kb = jnp.pad(k, pad).reshape(B, H, nb, L, hd)
    vb = jnp.pad(v, pad).reshape(B, H, nb, L, hd)
    scale = hd ** -0.5
    pos = jnp.arange(S)
    q_blk = pos // L
    k_mean = jnp.mean(kb.astype(jnp.float32), axis=3)
    gate = jnp.einsum('bhsd,bhnd->bhsn', q.astype(jnp.float32), k_mean)
    fully_past = jnp.arange(nb)[None, :] < q_blk[:, None]
    gate = jnp.where(fully_past, gate, -jnp.inf)
    _, sel = lax.top_k(gate, k_sel)
    valid = jnp.arange(k_sel)[None, :] < q_blk[:, None]

    C = MOBA_Q_CHUNK
    nc = S // C
    bi = jnp.arange(B)[:, None, None]
    hi = jnp.arange(H)[None, :, None]

    def to_chunks(a):
        a = a.reshape(B, H, nc, C, *a.shape[3:])
        return jnp.moveaxis(a, 2, 0)

    def one_chunk(args):
        q_c, sel_c, valid_c, c = args
        t = c * C + jnp.arange(C)
        own = (c * C) // L
        k_own = lax.dynamic_index_in_dim(kb, own, axis=2, keepdims=False)
        v_own = lax.dynamic_index_in_dim(vb, own, axis=2, keepdims=False)
        key_pos = own * L + jnp.arange(L)
        s_own = jnp.einsum('bhcd,bhld->bhcl', q_c, k_own).astype(jnp.float32) * scale
        s_own = jnp.where(key_pos[None, :] <= t[:, None], s_own, -jnp.inf)
        scores = []
        for j in range(k_sel):
            k_j = kb[bi, hi, sel_c[..., j]]
            s_j = jnp.einsum('bhcd,bhcld->bhcl', q_c, k_j).astype(jnp.float32) * scale
            scores.append(jnp.where(valid_c[:, j][:, None], s_j, -jnp.inf))
        p = jax.nn.softmax(jnp.concatenate(scores + [s_own], axis=-1), axis=-1).astype(v.dtype)
        p = p.reshape(B, H, C, k_sel + 1, L)
        out = jnp.einsum('bhcl,bhld->bhcd', p[..., k_sel, :], v_own)
        for j in range(k_sel):
            v_j = vb[bi, hi, sel_c[..., j]]
            out = out + jnp.einsum('bhcl,bhcld->bhcd', p[..., j, :], v_j)
        return out

    out = lax.map(one_chunk, (to_chunks(q), to_chunks(sel), valid.reshape(nc, C, k_sel), jnp.arange(nc)))
    return jnp.moveaxis(out, 0, 2).reshape(B, H, S, hd)


def gated_delta_rule(q, k, v, g, beta):
    B, H, S, dk = q.shape
    dv = v.shape[-1]
    C = DN_CHUNK
    n = S // C
    f32 = jnp.float32
    q = q.astype(f32).reshape(B, H, n, C, dk)
    k = k.astype(f32).reshape(B, H, n, C, dk)
    v = v.astype(f32).reshape(B, H, n, C, dv)
    g = jnp.cumsum(g.astype(f32).reshape(B, H, n, C), axis=-1)
    beta = beta.astype(f32).reshape(B, H, n, C)
    incl = jnp.tril(jnp.ones((C, C), dtype=bool))
    strict = jnp.tril(jnp.ones((C, C), dtype=bool), k=-1)
    gdiff = g[..., :, None] - g[..., None, :]
    decay = jnp.where(incl, jnp.exp(jnp.where(incl, gdiff, 0.0)), 0.0)
    k_beta = k * beta[..., None]
    v_beta = v * beta[..., None]
    lower = jnp.where(strict, jnp.einsum('bhncd,bhnmd->bhncm', k_beta, k) * decay, 0.0)
    a_mat = lower + jnp.eye(C, dtype=f32)
    rhs = jnp.concatenate([v_beta, k_beta * jnp.exp(g)[..., None]], axis=-1)
    sol = lax.linalg.triangular_solve(a_mat, rhs, left_side=True, lower=True, unit_diagonal=True)
    u = sol[..., :dv]
    w = sol[..., dv:]
    attn_intra = jnp.where(incl, jnp.einsum('bhncd,bhnmd->bhncm', q, k) * decay, 0.0)

    def step(state, xs):
        q_i, k_i, u_i, w_i, g_i, a_i = xs
        v_new = u_i - jnp.einsum('bhcd,bhde->bhce', w_i, state)
        o = (jnp.einsum('bhcd,bhde->bhce', q_i * jnp.exp(g_i)[..., None], state)
             + jnp.einsum('bhcm,bhme->bhce', a_i, v_new))
        g_last = g_i[..., -1]
        state = (state * jnp.exp(g_last)[..., None, None]
                 + jnp.einsum('bhcd,bhce->bhde', k_i * jnp.exp(g_last[..., None] - g_i)[..., None], v_new))
        return state, o

    xs = tuple(jnp.moveaxis(a, 2, 0) for a in (q, k, u, w, g, attn_intra))
    _, o = lax.scan(step, jnp.zeros((B, H, dk, dv), f32), xs)
    return jnp.moveaxis(o, 0, 2).reshape(B, H, S, dv)


def peer_ffn(h, w_q, keys1, keys2, u, v):
    B, S, D = h.shape
    T = B * S
    Hh, K = PEER_HEADS, PEER_TOPK
    hf = h.reshape(T, D)
    q = (hf @ w_q).astype(jnp.float32).reshape(T, Hh, 2, PEER_QUERY_DIM // 2)
    s1 = jnp.einsum('thd,nd->thn', q[:, :, 0], keys1.astype(jnp.float32))
    s2 = jnp.einsum('thd,nd->thn', q[:, :, 1], keys2.astype(jnp.float32))
    v1, i1 = lax.top_k(s1, K)
    v2, i2 = lax.top_k(s2, K)
    cand = (v1[..., :, None] + v2[..., None, :]).reshape(T, Hh, K * K)
    sc, ci = lax.top_k(cand, K)
    experts = (jnp.take_along_axis(i1, ci // K, axis=-1) * PEER_N_KEYS
               + jnp.take_along_axis(i2, ci % K, axis=-1))
    wts = jax.nn.softmax(sc, axis=-1)
    C = PEER_TOKEN_CHUNK
    nc = T // C

    def one_chunk(args):
        h_c, e_c, w_c = args
        act = jax.nn.gelu(jnp.einsum('td,thkd->thk', h_c, u[e_c]).astype(jnp.float32))
        return jnp.einsum('thk,thkd->td', (w_c * act).astype(v.dtype), v[e_c])

    out = lax.map(one_chunk, (hf.reshape(nc, C, D), experts.reshape(nc, C, Hh, K), wts.reshape(nc, C, Hh, K)))
    return out.reshape(B, S, D)


def hybrid_layer(x, mem, attn_norm_g, mem_norm_g, ffn_norm_g, w_in, moba_q_norm_g, moba_k_norm_g,
                 dn_conv_w, dn_a_log, dn_dt_bias, dn_out_norm_g, w_mem_kv, mem_q_norm_g, mem_k_norm_g,
                 w_branch, w_out, peer_w_q, peer_keys1, peer_keys2, peer_u, peer_v):
    B, S, D = x.shape
    pos = jnp.arange(S)
    h = rms_norm(x, attn_norm_g)
    proj = h @ w_in
    offsets = [int(o) for o in np.cumsum(IN_SIZES)[:-1]]
    moba_qkv, dn_qkv, b_raw, a_raw, z, mem_q, gate_raw = jnp.split(proj, offsets, axis=-1)

    mq, mk, mv = jnp.split(moba_qkv, 3, axis=-1)
    mq = rope(rms_norm(split_heads(mq, MOBA_HEADS), moba_q_norm_g), pos)
    mk = rope(rms_norm(split_heads(mk, MOBA_HEADS), moba_k_norm_g), pos)
    mv = split_heads(mv, MOBA_HEADS)
    moba_out = merge_heads(moba_attention(mq, mk, mv))

    dn_qkv = jax.nn.silu(causal_depthwise_conv(dn_qkv, dn_conv_w))
    dq, dk, dv = jnp.split(dn_qkv, 3, axis=-1)
    dq = l2_norm(split_heads(dq, DN_HEADS)) * (DN_HEAD_DIM ** -0.5)
    dk = l2_norm(split_heads(dk, DN_HEADS))
    dv = split_heads(dv, DN_HEADS)
    beta = jax.nn.sigmoid(b_raw.astype(jnp.float32)).transpose(0, 2, 1)
    g = (-jnp.exp(dn_a_log.astype(jnp.float32))[:, None]
         * jax.nn.softplus(a_raw.astype(jnp.float32).transpose(0, 2, 1) + dn_dt_bias.astype(jnp.float32)[:, None]))
    o = gated_delta_rule(dq, dk, dv, g, beta).transpose(0, 2, 1, 3)
    o = rms_norm(o, dn_out_norm_g) * jax.nn.silu(z.astype(jnp.float32).reshape(B, S, DN_HEADS, DN_HEAD_DIM))
    dn_out = o.reshape(B, S, DN_W).astype(x.dtype)

    mem_h = rms_norm(mem, mem_norm_g)
    mem_k, mem_v = jnp.split(mem_h @ w_mem_kv, 2, axis=-1)
    cq = rms_norm(split_heads(mem_q, MEM_HEADS), mem_q_norm_g)
    ck = rms_norm(split_heads(mem_k, MEM_HEADS), mem_k_norm_g)
    cv = split_heads(mem_v, MEM_HEADS)
    cs = jnp.einsum('bhsd,bhmd->bhsm', cq, ck).astype(jnp.float32) * (MEM_HEAD_DIM ** -0.5)
    cp = jax.nn.softmax(cs, axis=-1).astype(cv.dtype)
    mem_out = merge_heads(jnp.einsum('bhsm,bhmd->bhsd', cp, cv))

    branches = jnp.stack([moba_out, dn_out, mem_out], axis=2)
    branch_proj = jnp.einsum('bsnw,nwd->bsnd', branches, w_branch)
    gates = jax.nn.sigmoid(gate_raw.reshape(B, S, N_BRANCHES, D))
    mixed = jnp.sum(gates * branch_proj, axis=2)
    x = x + mixed @ w_out

    h2 = rms_norm(x, ffn_norm_g)
    x = x + peer_ffn(h2, peer_w_q, peer_keys1, peer_keys2, peer_u, peer_v)
    return x


def setup_inputs(seed: int = 0) -> dict:
    key = jax.random.key(seed)
    ks = jax.random.split(key, 24)
    f32 = jnp.float32

    def nrm(k, shape, scale):
        return jax.random.normal(k, shape, f32) * scale

    def gain(k, n):
        return 1.0 + 0.02 * jax.random.normal(k, (DEPTH, n), f32)

    x = nrm(ks[0], (BATCH, SEQ, D_MODEL), 1.0)
    mem = nrm(ks[1], (BATCH, MEM_LEN, D_MODEL), 1.0)
    attn_norm_g = gain(ks[2], D_MODEL)
    mem_norm_g = gain(ks[3], D_MODEL)
    ffn_norm_g = gain(ks[4], D_MODEL)
    w_in = nrm(ks[5], (DEPTH, D_MODEL, IN_WIDTH), D_MODEL ** -0.5)
    moba_q_norm_g = gain(ks[6], MOBA_HEAD_DIM)
    moba_k_norm_g = gain(ks[7], MOBA_HEAD_DIM)
    dn_conv_w = nrm(ks[8], (DEPTH, DN_CONV, 3 * DN_W), DN_CONV ** -0.5)
    dn_a_log = jnp.log(jax.random.uniform(ks[9], (DEPTH, DN_HEADS), f32, 1.0, 16.0))
    dt = jnp.exp(jax.random.uniform(ks[10], (DEPTH, DN_HEADS), f32, math.log(1e-3), math.log(1e-1)))
    dn_dt_bias = dt + jnp.log(-jnp.expm1(-dt))
    dn_out_norm_g = gain(ks[11], DN_HEAD_DIM)
    w_mem_kv = nrm(ks[12], (DEPTH, D_MODEL, 2 * MEM_W), D_MODEL ** -0.5)
    mem_q_norm_g = gain(ks[13], MEM_HEAD_DIM)
    mem_k_norm_g = gain(ks[14], MEM_HEAD_DIM)
    w_branch = nrm(ks[15], (DEPTH, N_BRANCHES, BRANCH_W, D_MODEL), BRANCH_W ** -0.5)
    w_out = nrm(ks[16], (DEPTH, D_MODEL, D_MODEL), D_MODEL ** -0.5)
    peer_w_q = nrm(ks[17], (DEPTH, D_MODEL, PEER_HEADS * PEER_QUERY_DIM), D_MODEL ** -0.5)
    peer_keys1 = nrm(ks[18], (DEPTH, PEER_N_KEYS, PEER_QUERY_DIM // 2), (PEER_QUERY_DIM // 2) ** -0.5)
    peer_keys2 = nrm(ks[19], (DEPTH, PEER_N_KEYS, PEER_QUERY_DIM // 2), (PEER_QUERY_DIM // 2) ** -0.5)
    peer_u = nrm(ks[20], (DEPTH, PEER_N_EXPERTS, D_MODEL), D_MODEL ** -0.5)
    peer_v = nrm(ks[21], (DEPTH, PEER_N_EXPERTS, D_MODEL), D_MODEL ** -0.5)
    return {"x": x, "mem": mem, "attn_norm_g": attn_norm_g, "mem_norm_g": mem_norm_g,
            "ffn_norm_g": ffn_norm_g, "w_in": w_in, "moba_q_norm_g": moba_q_norm_g,
            "moba_k_norm_g": moba_k_norm_g, "dn_conv_w": dn_conv_w, "dn_a_log": dn_a_log,
            "dn_dt_bias": dn_dt_bias, "dn_out_norm_g": dn_out_norm_g, "w_mem_kv": w_mem_kv,
            "mem_q_norm_g": mem_q_norm_g, "mem_k_norm_g": mem_k_norm_g, "w_branch": w_branch,
            "w_out": w_out, "peer_w_q": peer_w_q, "peer_keys1": peer_keys1, "peer_keys2": peer_keys2,
            "peer_u": peer_u, "peer_v": peer_v}


def reference(x, mem, attn_norm_g, mem_norm_g, ffn_norm_g, w_in, moba_q_norm_g, moba_k_norm_g,
              dn_conv_w, dn_a_log, dn_dt_bias, dn_out_norm_g, w_mem_kv, mem_q_norm_g, mem_k_norm_g,
              w_branch, w_out, peer_w_q, peer_keys1, peer_keys2, peer_u, peer_v):
    for l in range(DEPTH):
        x = hybrid_layer(x, mem, attn_norm_g[l], mem_norm_g[l], ffn_norm_g[l], w_in[l],
                         moba_q_norm_g[l], moba_k_norm_g[l], dn_conv_w[l], dn_a_log[l], dn_dt_bias[l],
                         dn_out_norm_g[l], w_mem_kv[l], mem_q_norm_g[l], mem_k_norm_g[l], w_branch[l],
                         w_out[l], peer_w_q[l], peer_keys1[l], peer_keys2[l], peer_u[l], peer_v[l])
    return x
```

```python
import functools
import math

import jax
import jax.numpy as jnp
from jax import lax
from jax.experimental import pallas as pl
from jax.experimental.pallas import tpu as pltpu

F32 = jnp.float32
BF16 = jnp.bfloat16
HI = lax.Precision.HIGHEST
EPS = 1e-6
ROPE_THETA = 10000.0
NT = (((1,), (1,)), ((), ()))
TN = (((0,), (0,)), ((), ()))

LANE = 128
MOBA_HEADS = 8
MOBA_BLOCK = 256
MOBA_TOPK = 3
DN_HEADS = 8
DN_CHUNK = 64
DN_CONV = 4
MEM_HEADS = 4
MEM_HEAD_DIM = 256
PEER_HEADS = 8
PEER_KEYS = 128
PEER_TOPK = 16

VMEM_LIMIT = 56 * 1024 * 1024


def _params(*sem):
    return pltpu.CompilerParams(dimension_semantics=sem, vmem_limit_bytes=VMEM_LIMIT)


def _rms(x, g):
    ms = jnp.mean(x * x, axis=-1, keepdims=True)
    return x * lax.rsqrt(ms + EPS) * g


def _norm_matmul_kernel(x_ref, g_ref, w_ref, o_ref, *rest, precision, emit_h):
    if emit_h:
        ho_ref, h_ref = rest
    else:
        (h_ref,) = rest

    @pl.when(pl.program_id(1) == 0)
    def _():
        h = _rms(x_ref[...], g_ref[...])
        h_ref[...] = h.astype(h_ref.dtype)
        if emit_h:
            ho_ref[...] = h.astype(ho_ref.dtype)

    o_ref[...] = jnp.dot(h_ref[...], w_ref[...], precision=precision,
                         preferred_element_type=F32).astype(o_ref.dtype)


def _norm_matmul(x, g, w, *, tm, tn, precision=None, emit_h=False):
    T, D = x.shape
    N = w.shape[1]
    out_shape = [jax.ShapeDtypeStruct((T, N), F32)]
    out_specs = [pl.BlockSpec((tm, tn), lambda i, j: (i, j))]
    if emit_h:
        out_shape.append(jax.ShapeDtypeStruct((T, D), BF16))
        out_specs.append(pl.BlockSpec((tm, D), lambda i, j: (i, 0)))
    res = pl.pallas_call(
        functools.partial(_norm_matmul_kernel, precision=precision, emit_h=emit_h),
        grid=(T // tm, N // tn),
        in_specs=[pl.BlockSpec((tm, D), lambda i, j: (i, 0)),
                  pl.BlockSpec((1, D), lambda i, j: (0, 0)),
                  pl.BlockSpec((D, tn), lambda i, j: (0, j))],
        out_specs=out_specs,
        out_shape=out_shape,
        scratch_shapes=[pltpu.VMEM((tm, D), w.dtype)],
        compiler_params=_params("parallel", "arbitrary"),
    )(x, g, w)
    return res if emit_h else res[0]


def _moba_prep_kernel(q_ref, k_ref, v_ref, gq_ref, gk_ref, cos_ref, sin_ref,
                      qo_ref, ko_ref, vo_ref, km_ref):
    cos = cos_ref[...]
    sin = sin_ref[...]

    def norm_rope(x, g):
        y = _rms(x, g)
        return y * cos + pltpu.roll(y, LANE // 2, 1) * sin

    q = norm_rope(q_ref[0], gq_ref[...])
    k = norm_rope(k_ref[0], gk_ref[...])
    qo_ref[0] = q
    ko_ref[0] = k.astype(BF16)
    vo_ref[0] = v_ref[0].astype(BF16)
    nb = k.shape[0] // MOBA_BLOCK
    km_ref[0, 0] = jnp.mean(k.reshape(nb, MOBA_BLOCK, LANE), axis=1)


def _moba_prep(proj3, gq, gk, cos2, sin2):
    B, S, _ = proj3.shape
    H = MOBA_HEADS
    ts = 2048
    nb_t = ts // MOBA_BLOCK
    col = lambda off: pl.BlockSpec((1, ts, LANE), lambda b, h, s: (b, s, off + h))
    return pl.pallas_call(
        _moba_prep_kernel,
        grid=(B, H, S // ts),
        in_specs=[col(0), col(H), col(2 * H),
                  pl.BlockSpec((1, LANE), lambda b, h, s: (0, 0)),
                  pl.BlockSpec((1, LANE), lambda b, h, s: (0, 0)),
                  pl.BlockSpec((ts, LANE), lambda b, h, s: (s, 0)),
                  pl.BlockSpec((ts, LANE), lambda b, h, s: (s, 0))],
        out_specs=[col(0), col(0), col(0),
                   pl.BlockSpec((1, 1, nb_t, LANE), lambda b, h, s: (b, h, s, 0))],
        out_shape=[jax.ShapeDtypeStruct((B, S, H * LANE), F32),
                   jax.ShapeDtypeStruct((B, S, H * LANE), BF16),
                   jax.ShapeDtypeStruct((B, S, H * LANE), BF16),
                   jax.ShapeDtypeStruct((B, H, S // MOBA_BLOCK, LANE), F32)],
        compiler_params=_params("parallel", "parallel", "parallel"),
    )(proj3, proj3, proj3, gq, gk, cos2, sin2)


def _moba_attn_kernel(q_ref, k_ref, v_ref, km_ref, o_ref, m_ref, l_ref, acc_ref):
    L = MOBA_BLOCK
    nb = km_ref.shape[2]
    i = pl.program_id(2)
    scale = LANE ** -0.5
    q = q_ref[0]
    qb = q.astype(BF16)

    gate = lax.dot_general(q, km_ref[0, 0], NT, precision=HI, preferred_element_type=F32)
    lane = lax.broadcasted_iota(jnp.int32, gate.shape, 1)
    rank = jnp.zeros(gate.shape, jnp.int32)
    for m in range(nb - 1):
        gm = gate[:, m:m + 1]
        beats = (gm > gate) | ((gm == gate) & (lane > m))
        rank = rank + jnp.where(beats & (m < i), 1, 0)
    sel = (lane < i) & (rank < MOBA_TOPK)
    bias = jnp.where(sel, 0.0, -jnp.inf)

    start = pl.multiple_of(i * L, L)
    s = lax.dot_general(qb, k_ref[0, pl.ds(start, L), :], NT, preferred_element_type=F32) * scale
    r = lax.broadcasted_iota(jnp.int32, (L, L), 0)
    c = lax.broadcasted_iota(jnp.int32, (L, L), 1)
    s = jnp.where(c <= r, s, -jnp.inf)
    m0 = jnp.max(s, axis=-1, keepdims=True)
    p = jnp.exp(s - m0)
    m_ref[...] = m0
    l_ref[...] = jnp.sum(p, axis=-1, keepdims=True)
    acc_ref[...] = jnp.dot(p.astype(BF16), v_ref[0, pl.ds(start, L), :], preferred_element_type=F32)

    for n in range(nb - 1):
        @pl.when(n < i)
        def _(n=n):
            s = lax.dot_general(qb, k_ref[0, n * L:(n + 1) * L, :], NT, preferred_element_type=F32) * scale
            s = s + bias[:, n:n + 1]
            m_old = m_ref[...]
            m_new = jnp.maximum(m_old, jnp.max(s, axis=-1, keepdims=True))
            alpha = jnp.exp(m_old - m_new)
            p = jnp.exp(s - m_new)
            l_ref[...] = alpha * l_ref[...] + jnp.sum(p, axis=-1, keepdims=True)
            acc_ref[...] = alpha * acc_ref[...] + jnp.dot(
                p.astype(BF16), v_ref[0, n * L:(n + 1) * L, :], preferred_element_type=F32)
            m_ref[...] = m_new

    o_ref[0] = (acc_ref[...] / l_ref[...]).astype(o_ref.dtype)


def _moba_attn(q, k, v, kmean):
    B, S, W = q.shape
    H = MOBA_HEADS
    L = MOBA_BLOCK
    nb = S // L
    return pl.pallas_call(
        _moba_attn_kernel,
        grid=(B, H, nb),
        in_specs=[pl.BlockSpec((1, L, LANE), lambda b, h, i: (b, i, h)),
                  pl.BlockSpec((1, S, LANE), lambda b, h, i: (b, 0, h)),
                  pl.BlockSpec((1, S, LANE), lambda b, h, i: (b, 0, h)),
                  pl.BlockSpec((1, 1, nb, LANE), lambda b, h, i: (b, h, 0, 0))],
        out_specs=pl.BlockSpec((1, L, LANE), lambda b, h, i: (b, i, h)),
        out_shape=jax.ShapeDtypeStruct((B, S, W), BF16),
        scratch_shapes=[pltpu.VMEM((L, 1), F32), pltpu.VMEM((L, 1), F32), pltpu.VMEM((L, LANE), F32)],
        compiler_params=_params("parallel", "parallel", "arbitrary"),
    )(q, k, v, kmean)


def _mem_attn_kernel(q_ref, k_ref, v_ref, gq_ref, gk_ref, o_ref):
    cq = _rms(q_ref[0], gq_ref[...]).astype(BF16)
    ck = _rms(k_ref[0], gk_ref[...]).astype(BF16)
    s = lax.dot_general(cq, ck, NT, preferred_element_type=F32) * (MEM_HEAD_DIM ** -0.5)
    m = jnp.max(s, axis=-1, keepdims=True)
    p = jnp.exp(s - m)
    l = jnp.sum(p, axis=-1, keepdims=True)
    o = jnp.dot(p.astype(BF16), v_ref[0].astype(BF16), preferred_element_type=F32) / l
    o_ref[0] = o.astype(o_ref.dtype)


def _mem_attn(proj3, kv3, gq, gk, q_col):
    B, S, _ = proj3.shape
    M = kv3.shape[1]
    hd = MEM_HEAD_DIM
    tq = 1024
    return pl.pallas_call(
        _mem_attn_kernel,
        grid=(B, MEM_HEADS, S // tq),
        in_specs=[pl.BlockSpec((1, tq, hd), lambda b, h, i: (b, i, q_col + h)),
                  pl.BlockSpec((1, M, hd), lambda b, h, i: (b, 0, h)),
                  pl.BlockSpec((1, M, hd), lambda b, h, i: (b, 0, MEM_HEADS + h)),
                  pl.BlockSpec((1, hd), lambda b, h, i: (0, 0)),
                  pl.BlockSpec((1, hd), lambda b, h, i: (0, 0))],
        out_specs=pl.BlockSpec((1, tq, hd), lambda b, h, i: (b, i, h)),
        out_shape=jax.ShapeDtypeStruct((B, S, MEM_HEADS * hd), BF16),
        compiler_params=_params("parallel", "parallel", "parallel"),
    )(proj3, kv3, kv3, gq, gk)


def _dn_prep_kernel(x_ref, w_ref, o_ref, pad_ref):
    S = x_ref.shape[1]
    cb = pl.program_id(1)
    x = x_ref[0]
    pad_ref[0:8, :] = jnp.zeros((8, LANE), F32)
    pad_ref[8:, :] = x
    w = w_ref[...]
    y = w[DN_CONV - 1:DN_CONV, :] * x
    for j in range(DN_CONV - 1):
        off = 8 - (DN_CONV - 1) + j
        y = y + w[j:j + 1, :] * pad_ref[off:off + S, :]
    y = y * jax.nn.sigmoid(y)
    nrm = lax.rsqrt(jnp.sum(y * y, axis=-1, keepdims=True) + EPS)
    scale = jnp.where(cb < DN_HEADS, nrm * (LANE ** -0.5), jnp.where(cb < 2 * DN_HEADS, nrm, 1.0))
    o_ref[0] = y * scale


def _dn_prep(proj3, conv_w, col0):
    B, S, _ = proj3.shape
    ncb = 3 * DN_HEADS
    return pl.pallas_call(
        _dn_prep_kernel,
        grid=(B, ncb),
        in_specs=[pl.BlockSpec((1, S, LANE), lambda b, c: (b, 0, col0 + c)),
                  pl.BlockSpec((DN_CONV, LANE), lambda b, c: (0, c))],
        out_specs=pl.BlockSpec((1, S, LANE), lambda b, c: (b, 0, c)),
        out_shape=jax.ShapeDtypeStruct((B, S, ncb * LANE), F32),
        scratch_shapes=[pltpu.VMEM((S + 8, LANE), F32)],
        compiler_params=_params("parallel", "parallel"),
    )(proj3, conv_w)


def _softplus(x):
    return jnp.maximum(x, 0.0) + jnp.log1p(jnp.exp(-jnp.abs(x)))


def _dn_scan_kernel(q_ref, k_ref, v_ref, b_ref, a_ref, at_ref, alr_ref, dtr_ref, alc_ref, dtc_ref,
                    z_ref, g_ref, o_ref, st_ref):
    C = DN_CHUNK

    @pl.when(pl.program_id(1) == 0)
    def _():
        st_ref[...] = jnp.zeros_like(st_ref)

    ii = lax.broadcasted_iota(jnp.int32, (C, C), 0)
    jj = lax.broadcasted_iota(jnp.int32, (C, C), 1)
    incl = ii >= jj
    strict = ii > jj
    tril = incl.astype(F32)
    triu = (ii <= jj).astype(F32)

    g_col = -jnp.exp(alr_ref[...]) * _softplus(a_ref[0] + dtr_ref[...])
    g_row = -jnp.exp(alc_ref[...]) * _softplus(at_ref[0, 0] + dtc_ref[...])
    gc_col = jnp.dot(tril, g_col, precision=HI, preferred_element_type=F32)
    gc_row = jnp.dot(g_row, triu, precision=HI, preferred_element_type=F32)
    beta = jax.nn.sigmoid(b_ref[0])
    gn = g_ref[...]

    def mm(a, b):
        return jnp.dot(a.astype(BF16), b.astype(BF16), preferred_element_type=F32)

    def mm_hi(a, b):
        return jnp.dot(a, b, precision=HI, preferred_element_type=F32)

    for h in range(DN_HEADS):
        sl = slice(h * LANE, (h + 1) * LANE)
        q = q_ref[0, :, sl]
        k = k_ref[0, :, sl]
        v = v_ref[0, :, sl]
        gcc = gc_col[:, h:h + 1]
        gcr = gc_row[h:h + 1, :]
        decay = jnp.where(incl, jnp.exp(jnp.where(incl, gcc - gcr, 0.0)), 0.0)
        bc = beta[:, h:h + 1]
        kb = k * bc
        vb = v * bc
        kk = lax.dot_general(kb.astype(BF16), k.astype(BF16), NT, preferred_element_type=F32)
        n = jnp.where(strict, kk * decay, 0.0)
        qk = lax.dot_general(q.astype(BF16), k.astype(BF16), NT, preferred_element_type=F32)
        attn = jnp.where(incl, qk * decay, 0.0)
        egc = jnp.exp(gcc)
        x = jnp.concatenate([vb, kb * egc], axis=1)
        x = x - mm_hi(n, x)
        p = mm_hi(n, n)
        levels = int(math.log2(C)) - 1
        for lvl in range(levels):
            x = x + mm_hi(p, x)
            if lvl + 1 < levels:
                p = mm_hi(p, p)
        u = x[:, :LANE]
        w = x[:, LANE:]
        s = st_ref[h]
        v_new = u - mm(w, s)
        o = mm(q * egc, s) + mm(attn, v_new)
        g_last = gcc[C - 1:C, :]
        kd = k * jnp.exp(g_last - gcc)
        st_ref[h] = s * jnp.exp(g_last) + lax.dot_general(
            kd.astype(BF16), v_new.astype(BF16), TN, preferred_element_type=F32)
        zz = z_ref[0, :, sl]
        o_ref[0, :, sl] = (_rms(o, gn) * (zz * jax.nn.sigmoid(zz))).astype(o_ref.dtype)


def _dn_scan(qkv, braw, araw, araw_t, a_log, dt_bias, proj3, z_col, onorm_g):
    B, S, _ = qkv.shape
    H = DN_HEADS
    C = DN_CHUNK
    W = H * LANE
    tile = lambda col: pl.BlockSpec((1, C, W), lambda b, c: (b, c, col))
    small = lambda shape: pl.BlockSpec(shape, lambda b, c: (0,) * len(shape))
    return pl.pallas_call(
        _dn_scan_kernel,
        grid=(B, S // C),
        in_specs=[tile(0), tile(1), tile(2),
                  pl.BlockSpec((1, C, H), lambda b, c: (b, c, 0)),
                  pl.BlockSpec((1, C, H), lambda b, c: (b, c, 0)),
                  pl.BlockSpec((1, 1, H, C), lambda b, c: (b, c, 0, 0)),
                  small((1, H)), small((1, H)), small((H, 1)), small((H, 1)),
                  tile(z_col), small((1, LANE))],
        out_specs=tile(0),
        out_shape=jax.ShapeDtypeStruct((B, S, W), BF16),
        scratch_shapes=[pltpu.VMEM((H, LANE, LANE), F32)],
        compiler_params=_params("parallel", "arbitrary"),
    )(qkv, qkv, qkv, braw, araw, araw_t, a_log.reshape(1, H), dt_bias.reshape(1, H),
      a_log.reshape(H, 1), dt_bias.reshape(H, 1), proj3, onorm_g)


def _merge_kernel(x_ref, b0_ref, b1_ref, b2_ref, g0_ref, g1_ref, g2_ref, wb_ref, wo_ref, o_ref):
    mixed = None
    for i, (b_ref, g_ref) in enumerate(((b0_ref, g0_ref), (b1_ref, g1_ref), (b2_ref, g2_ref))):
        bp = jnp.dot(b_ref[...], wb_ref[i], preferred_element_type=F32)
        t = jax.nn.sigmoid(g_ref[...]) * bp
        mixed = t if mixed is None else mixed + t
    o_ref[...] = x_ref[...] + jnp.dot(mixed.astype(BF16), wo_ref[...], preferred_element_type=F32)


def _merge(x2, moba_out, dn_out, mem_out, proj, gate_col, w_branch, w_out):
    T, D = x2.shape
    tm = 256
    row = lambda col: pl.BlockSpec((tm, D), lambda i: (i, col))
    return pl.pallas_call(
        _merge_kernel,
        grid=(T // tm,),
        in_specs=[row(0), row(0), row(0), row(0), row(gate_col), row(gate_col + 1), row(gate_col + 2),
                  pl.BlockSpec((3, D, D), lambda i: (0, 0, 0)),
                  pl.BlockSpec((D, D), lambda i: (0, 0))],
        out_specs=row(0),
        out_shape=jax.ShapeDtypeStruct((T, D), F32),
        compiler_params=_params("parallel"),
    )(x2, moba_out, dn_out, mem_out, proj, proj, proj, w_branch, w_out)


def _top16_rows(x, idx, n_rows):
    vals = []
    for _ in range(PEER_TOPK):
        m = jnp.max(x, axis=0, keepdims=True)
        first = jnp.min(jnp.where(x == m, idx, n_rows), axis=0, keepdims=True)
        x = jnp.where(idx == first, -jnp.inf, x)
        vals.append(m)
    return vals


def _peer_select_kernel(qp_ref, k1_ref, k2_ref, s1_ref, s2_ref, tau_ref, cc_ref):
    tm = qp_ref.shape[0]
    K = PEER_KEYS
    q = qp_ref[...]
    s1 = lax.dot_general(k1_ref[...], q[:, :K], NT, precision=HI, preferred_element_type=F32)
    s2 = lax.dot_general(k2_ref[...], q[:, K:], NT, precision=HI, preferred_element_type=F32)
    s1_ref[0] = s1
    s2_ref[0] = s2
    idx = lax.broadcasted_iota(jnp.int32, (K, tm), 0)
    v1 = _top16_rows(s1, idx, K)
    v2 = jnp.concatenate(_top16_rows(s2, idx, K), axis=0)
    cand = jnp.concatenate([v1[i] + v2 for i in range(PEER_TOPK)], axis=0)
    idx2 = lax.broadcasted_iota(jnp.int32, cand.shape, 0)
    top = _top16_rows(cand, idx2, cand.shape[0])
    smax = top[0]
    z = jnp.exp(top[0] - smax)
    for t in top[1:]:
        z = z + jnp.exp(t - smax)
    tau_ref[0] = top[PEER_TOPK - 1]
    cc_ref[0] = smax + jnp.log(z)


def _peer_select(qp, keys1, keys2):
    T = qp.shape[0]
    H, K = PEER_HEADS, PEER_KEYS
    tm = 512
    return pl.pallas_call(
        _peer_select_kernel,
        grid=(T // tm, H),
        in_specs=[pl.BlockSpec((tm, 2 * K), lambda i, h: (i, h)),
                  pl.BlockSpec((K, K), lambda i, h: (0, 0)),
                  pl.BlockSpec((K, K), lambda i, h: (0, 0))],
        out_specs=[pl.BlockSpec((1, K, tm), lambda i, h: (h, 0, i)),
                   pl.BlockSpec((1, K, tm), lambda i, h: (h, 0, i)),
                   pl.BlockSpec((1, 1, tm), lambda i, h: (h, 0, i)),
                   pl.BlockSpec((1, 1, tm), lambda i, h: (h, 0, i))],
        out_shape=[jax.ShapeDtypeStruct((H, K, T), F32), jax.ShapeDtypeStruct((H, K, T), F32),
                   jax.ShapeDtypeStruct((H, 1, T), F32), jax.ShapeDtypeStruct((H, 1, T), F32)],
        compiler_params=_params("parallel", "parallel"),
    )(qp, keys1, keys2)


def _gelu_tanh(x):
    return 0.5 * x * (1.0 + jnp.tanh(math.sqrt(2.0 / math.pi) * (x + 0.044715 * (x * x * x))))


def _peer_dense_kernel(h_ref, u_ref, vt_ref, s1_ref, s2_ref, tau_ref, cc_ref, x_ref, o_ref,
                       g_ref, wa_ref, acc_ref):
    K = PEER_KEYS
    j = pl.program_id(1)
    n_a = u_ref.shape[0] // K

    @pl.when(j == 0)
    def _():
        acc_ref[...] = jnp.zeros_like(acc_ref)

    g_ref[...] = lax.dot_general(u_ref[...], h_ref[...], NT, preferred_element_type=F32)

    def body(aa, carry):
        a = j * n_a + aa
        w = None
        for h in range(PEER_HEADS):
            sm = s2_ref[h] + s1_ref[h, pl.ds(a, 1), :]
            t = jnp.where(sm >= tau_ref[h], jnp.exp(sm - cc_ref[h]), 0.0)
            w = t if w is None else w + t
        r0 = pl.multiple_of(aa * K, K)
        wa_ref[pl.ds(r0, K), :] = (w * _gelu_tanh(g_ref[pl.ds(r0, K), :])).astype(BF16)
        return carry

    lax.fori_loop(0, n_a, body, 0)
    acc_ref[...] += jnp.dot(vt_ref[...], wa_ref[...], preferred_element_type=F32)

    @pl.when(j == pl.num_programs(1) - 1)
    def _():
        o_ref[...] = x_ref[...] + acc_ref[...].T


def _peer_dense(h2, u, vt, s1, s2, tau, cc, x1):
    T, D = h2.shape
    E = u.shape[0]
    H, K = PEER_HEADS, PEER_KEYS
    tm, te = 512, 1024
    return pl.pallas_call(
        _peer_dense_kernel,
        grid=(T // tm, E // te),
        in_specs=[pl.BlockSpec((tm, D), lambda i, j: (i, 0)),
                  pl.BlockSpec((te, D), lambda i, j: (j, 0)),
                  pl.BlockSpec((D, te), lambda i, j: (0, j)),
                  pl.BlockSpec((H, K, tm), lambda i, j: (0, 0, i)),
                  pl.BlockSpec((H, K, tm), lambda i, j: (0, 0, i)),
                  pl.BlockSpec((H, 1, tm), lambda i, j: (0, 0, i)),
                  pl.BlockSpec((H, 1, tm), lambda i, j: (0, 0, i)),
                  pl.BlockSpec((tm, D), lambda i, j: (i, 0))],
        out_specs=pl.BlockSpec((tm, D), lambda i, j: (i, 0)),
        out_shape=jax.ShapeDtypeStruct((T, D), F32),
        scratch_shapes=[pltpu.VMEM((te, tm), F32), pltpu.VMEM((te, tm), BF16), pltpu.VMEM((D, tm), F32)],
        compiler_params=_params("parallel", "arbitrary"),
    )(h2, u, vt, s1, s2, tau, cc, x1)


def _rope_tables(S):
    inv = 1.0 / (ROPE_THETA ** (jnp.arange(0, LANE, 2, dtype=F32) / LANE))
    ang = jnp.arange(S, dtype=F32)[:, None] * inv[None, :]
    cos, sin = jnp.cos(ang), jnp.sin(ang)
    return jnp.concatenate([cos, cos], axis=1), jnp.concatenate([-sin, sin], axis=1)


def _layer(x, mem, attn_norm_g, mem_norm_g, ffn_norm_g, w_in, moba_q_norm_g, moba_k_norm_g,
           dn_conv_w, dn_a_log, dn_dt_bias, dn_out_norm_g, w_mem_kv, mem_q_norm_g, mem_k_norm_g,
           w_branch, w_out, peer_w_q, peer_keys1, peer_keys2, peer_u, peer_v):
    B, S, D = x.shape
    T = B * S
    x2 = x.reshape(T, D)
    row = lambda g: g.reshape(1, -1)

    moba_w, dn_w = 3 * MOBA_HEADS * LANE, 3 * DN_HEADS * LANE
    n_small = 2 * DN_HEADS
    o_small = moba_w + dn_w
    w_main = jnp.concatenate([w_in[:, :o_small], w_in[:, o_small + n_small:]], axis=1).astype(BF16)
    w_small = jnp.pad(w_in[:, o_small:o_small + n_small], ((0, 0), (0, LANE - n_small)))
    proj = _norm_matmul(x2, row(attn_norm_g), w_main, tm=1024, tn=512)
    small = _norm_matmul(x2, row(attn_norm_g), w_small, tm=1024, tn=LANE, precision=HI)
    proj3 = proj.reshape(B, S, -1)
    dn_col = moba_w // LANE
    z_col = (moba_w + dn_w) // D
    memq_col = (moba_w + dn_w + D) // MEM_HEAD_DIM
    gate_col = (moba_w + dn_w + 2 * D) // D

    cos2, sin2 = _rope_tables(S)
    mq, mk, mv, kmean = _moba_prep(proj3, row(moba_q_norm_g), row(moba_k_norm_g), cos2, sin2)
    moba_out = _moba_attn(mq, mk, mv, kmean)

    dn_qkv = _dn_prep(proj3, dn_conv_w, dn_col)
    braw = small[:, :DN_HEADS].reshape(B, S, DN_HEADS)
    araw = small[:, DN_HEADS:n_small].reshape(B, S, DN_HEADS)
    araw_t = araw.reshape(B, S // DN_CHUNK, DN_CHUNK, DN_HEADS).transpose(0, 1, 3, 2)
    dn_out = _dn_scan(dn_qkv, braw, araw, araw_t, dn_a_log, dn_dt_bias, proj3, z_col, row(dn_out_norm_g))

    M = mem.shape[1]
    kv = _norm_matmul(mem.reshape(B * M, D), row(mem_norm_g), w_mem_kv.astype(BF16), tm=B * M, tn=512)
    mem_out = _mem_attn(proj3, kv.reshape(B, M, -1), row(mem_q_norm_g), row(mem_k_norm_g), memq_col)

    x1 = _merge(x2, moba_out.reshape(T, D), dn_out.reshape(T, D), mem_out.reshape(T, D), proj, gate_col,
                w_branch.astype(BF16), w_out.astype(BF16))

    qp, h2 = _norm_matmul(x1, row(ffn_norm_g), peer_w_q.astype(BF16), tm=1024, tn=512, emit_h=True)
    s1, s2, tau, cc = _peer_select(qp, peer_keys1, peer_keys2)
    out = _peer_dense(h2, peer_u.astype(BF16), peer_v.T.astype(BF16), s1, s2, tau, cc, x1)
    return out.reshape(B, S, D)


def kernel(x, mem, attn_norm_g, mem_norm_g, ffn_norm_g, w_in, moba_q_norm_g, moba_k_norm_g, dn_conv_w, dn_a_log, dn_dt_bias, dn_out_norm_g, w_mem_kv, mem_q_norm_g, mem_k_norm_g, w_branch, w_out, peer_w_q, peer_keys1, peer_keys2, peer_u, peer_v):
    for l in range(w_in.shape[0]):
        x = _layer(x, mem, attn_norm_g[l], mem_norm_g[l], ffn_norm_g[l], w_in[l], moba_q_norm_g[l],
                   moba_k_norm_g[l], dn_conv_w[l], dn_a_log[l], dn_dt_bias[l], dn_out_norm_g[l],
                   w_mem_kv[l], mem_q_norm_g[l], mem_k_norm_g[l], w_branch[l], w_out[l], peer_w_q[l],
                   peer_keys1[l], peer_keys2[l], peer_u[l], peer_v[l])
    return x
```

```python
import functools
import math

import jax
import jax.numpy as jnp
from jax import lax
from jax.experimental import pallas as pl
from jax.experimental.pallas import tpu as pltpu

F32 = jnp.float32
BF16 = jnp.bfloat16
HI = lax.Precision.HIGHEST
EPS = 1e-6
ROPE_THETA = 10000.0
NT = (((1,), (1,)), ((), ()))
TN = (((0,), (0,)), ((), ()))

LANE = 128
MOBA_HEADS = 8
MOBA_BLOCK = 256
MOBA_TOPK = 3
DN_HEADS = 8
DN_CHUNK = 64
DN_CONV = 4
MEM_HEADS = 4
MEM_HEAD_DIM = 256
PEER_HEADS = 8
PEER_KEYS = 128
PEER_TOPK = 16

VMEM_LIMIT = 56 * 1024 * 1024


def _params(*sem):
    return pltpu.CompilerParams(dimension_semantics=sem, vmem_limit_bytes=VMEM_LIMIT)


def _rms(x, g):
    ms = jnp.mean(x * x, axis=-1, keepdims=True)
    return x * lax.rsqrt(ms + EPS) * g


def _norm_matmul_kernel(x_ref, g_ref, w_ref, o_ref, *rest, precision, emit_h):
    if emit_h:
        ho_ref, h_ref = rest
    else:
        (h_ref,) = rest

    @pl.when(pl.program_id(1) == 0)
    def _():
        h = _rms(x_ref[...], g_ref[...])
        h_ref[...] = h.astype(h_ref.dtype)
        if emit_h:
            ho_ref[...] = h.astype(ho_ref.dtype)

    o_ref[...] = jnp.dot(h_ref[...], w_ref[...], precision=precision,
                         preferred_element_type=F32).astype(o_ref.dtype)


def _norm_matmul(x, g, w, *, tm, tn, name, precision=None, emit_h=False):
    T, D = x.shape
    N = w.shape[1]
    out_shape = [jax.ShapeDtypeStruct((T, N), F32)]
    out_specs = [pl.BlockSpec((tm, tn), lambda i, j: (i, j))]
    if emit_h:
        out_shape.append(jax.ShapeDtypeStruct((T, D), BF16))
        out_specs.append(pl.BlockSpec((tm, D), lambda i, j: (i, 0)))
    res = pl.pallas_call(
        functools.partial(_norm_matmul_kernel, precision=precision, emit_h=emit_h),
        grid=(T // tm, N // tn),
        in_specs=[pl.BlockSpec((tm, D), lambda i, j: (i, 0)),
                  pl.BlockSpec((1, D), lambda i, j: (0, 0)),
                  pl.BlockSpec((D, tn), lambda i, j: (0, j))],
        out_specs=out_specs,
        out_shape=out_shape,
        scratch_shapes=[pltpu.VMEM((tm, D), w.dtype)],
        compiler_params=_params("parallel", "arbitrary"),
        name=name,
    )(x, g, w)
    return res if emit_h else res[0]


MOBA_NEG = -1e30


def _moba_prep_kernel(q_ref, k_ref, v_ref, gq_ref, gk_ref, cos_ref, sin_ref,
                      qo_ref, ko_ref, vo_ref, km_ref):
    cos = cos_ref[...]
    sin = sin_ref[...]

    def norm_rope(x, g):
        y = _rms(x, g)
        return y * cos + pltpu.roll(y, LANE // 2, 1) * sin

    q = norm_rope(q_ref[0], gq_ref[...])
    k = norm_rope(k_ref[0], gk_ref[...])
    ts = k.shape[0]
    nb = ts // MOBA_BLOCK
    qo_ref[0] = q
    row = lax.broadcasted_iota(jnp.int32, (ts, LANE), 0) + pl.program_id(2) * ts
    lane = lax.broadcasted_iota(jnp.int32, (ts, LANE), 1)
    ko_ref[0, :, 0:LANE] = k.astype(BF16)
    ko_ref[0, :, LANE:2 * LANE] = jnp.where(lane == row // MOBA_BLOCK, 1.0, 0.0).astype(BF16)
    vo_ref[0] = v_ref[0].astype(BF16)
    km_ref[0, 0] = jnp.mean(k.reshape(nb, MOBA_BLOCK, LANE), axis=1)


def _moba_prep(proj3, gq, gk, cos2, sin2):
    B, S, _ = proj3.shape
    H = MOBA_HEADS
    ts = 2048
    nb_t = ts // MOBA_BLOCK
    col = lambda off: pl.BlockSpec((1, ts, LANE), lambda b, h, s: (b, s, off + h))
    return pl.pallas_call(
        _moba_prep_kernel,
        grid=(B, H, S // ts),
        in_specs=[col(0), col(H), col(2 * H),
                  pl.BlockSpec((1, LANE), lambda b, h, s: (0, 0)),
                  pl.BlockSpec((1, LANE), lambda b, h, s: (0, 0)),
                  pl.BlockSpec((ts, LANE), lambda b, h, s: (s, 0)),
                  pl.BlockSpec((ts, LANE), lambda b, h, s: (s, 0))],
        out_specs=[col(0),
                   pl.BlockSpec((1, ts, 2 * LANE), lambda b, h, s: (b, s, h)),
                   col(0),
                   pl.BlockSpec((1, 1, nb_t, LANE), lambda b, h, s: (b, h, s, 0))],
        out_shape=[jax.ShapeDtypeStruct((B, S, H * LANE), F32),
                   jax.ShapeDtypeStruct((B, S, H * 2 * LANE), BF16),
                   jax.ShapeDtypeStruct((B, S, H * LANE), BF16),
                   jax.ShapeDtypeStruct((B, H, S // MOBA_BLOCK, LANE), F32)],
        compiler_params=_params("parallel", "parallel", "parallel"),
        name="moba_prep",
    )(proj3, proj3, proj3, gq, gk, cos2, sin2)


def _moba_attn_kernel(qa_ref, qb_ref, k_ref, v_ref, km_ref, oa_ref, ob_ref,
                      qaug_ref, s_ref, m_ref, mb_ref, l_ref, acc_ref):
    L = MOBA_BLOCK
    nb = km_ref.shape[2]
    p = pl.program_id(2)
    tiles = (p, nb - 1 - p)
    km = km_ref[0, 0]
    row = lax.broadcasted_iota(jnp.int32, (nb, L), 0)
    eye = (lax.broadcasted_iota(jnp.int32, (nb, LANE), 0)
           == lax.broadcasted_iota(jnp.int32, (nb, LANE), 1)).astype(BF16)

    for t, q_ref in enumerate((qa_ref, qb_ref)):
        ti = tiles[t]
        q = q_ref[0]
        gate = lax.dot_general(km, q, NT, precision=HI, preferred_element_type=F32)
        rank = jnp.zeros((nb, L), jnp.int32)
        for m in range(nb - 1):
            gm = gate[m:m + 1, :]
            beats = (gm > gate) | ((gm == gate) & (row > m))
            rank = rank + jnp.where(beats & (m < ti), 1, 0)
        keep = ((row < ti) & (rank < MOBA_TOPK)) | (row == ti)
        keep_t = lax.dot_general(jnp.where(keep, 1.0, 0.0).astype(BF16), eye, TN,
                                 preferred_element_type=F32)
        qaug_ref[t, :, 0:LANE] = (q * (LANE ** -0.5 * math.log2(math.e))).astype(BF16)
        qaug_ref[t, :, LANE:2 * LANE] = jnp.where(keep_t > 0.5, 0.0, MOBA_NEG).astype(BF16)

    def scores(t, n):
        kb = k_ref[0, pl.ds(pl.multiple_of(n * L, L), L), :]
        return lax.dot_general(qaug_ref[t], kb, NT, preferred_element_type=F32)

    def half_max(s):
        return jnp.maximum(s[:, :LANE], s[:, LANE:])

    def slot(k):
        first = k < p
        return jnp.where(first, 0, 1), jnp.where(first, k, k - p)

    r = lax.broadcasted_iota(jnp.int32, (L, L), 0)
    c = lax.broadcasted_iota(jnp.int32, (L, L), 1)
    causal = jnp.where(c <= r, 0.0, MOBA_NEG)
    for t in range(2):
        s = scores(t, tiles[t]) + causal
        s_ref[nb - 1 + t] = s
        m_ref[t] = half_max(s)
    for k in range(nb - 1):
        t, n = slot(k)
        s = scores(t, n)
        s_ref[k] = s
        m_ref[t] = jnp.maximum(m_ref[t], half_max(s))

    for t in range(2):
        mb_ref[t] = jnp.broadcast_to(jnp.max(m_ref[t], axis=-1, keepdims=True), (L, LANE))
    l_ref[...] = jnp.zeros_like(l_ref)
    acc_ref[...] = jnp.zeros_like(acc_ref)
    for k in range(nb + 1):
        if k < nb - 1:
            t, n = slot(k)
        else:
            t = k - (nb - 1)
            n = tiles[t]
        mb = mb_ref[t]
        s = s_ref[k]
        pr = jnp.concatenate([jnp.exp2(s[:, :LANE] - mb), jnp.exp2(s[:, LANE:] - mb)], axis=1)
        l_ref[t] += pr[:, :LANE] + pr[:, LANE:]
        vb = v_ref[0, pl.ds(pl.multiple_of(n * L, L), L), :]
        acc_ref[t] += jnp.dot(pr.astype(BF16), vb, preferred_element_type=F32)

    for t, o_ref in enumerate((oa_ref, ob_ref)):
        l = jnp.sum(l_ref[t], axis=-1, keepdims=True)
        o_ref[0] = (acc_ref[t] / l).astype(o_ref.dtype)


def _moba_attn(q, k_aug, v, kmean):
    B, S, W = q.shape
    H = MOBA_HEADS
    L = MOBA_BLOCK
    nb = S // L
    half = nb // 2
    lo, hi = pl.pallas_call(
        _moba_attn_kernel,
        grid=(B, H, half),
        in_specs=[pl.BlockSpec((1, L, LANE), lambda b, h, p: (b, p, h)),
                  pl.BlockSpec((1, L, LANE), lambda b, h, p: (b, nb - 1 - p, h)),
                  pl.BlockSpec((1, S, 2 * LANE), lambda b, h, p: (b, 0, h)),
                  pl.BlockSpec((1, S, LANE), lambda b, h, p: (b, 0, h)),
                  pl.BlockSpec((1, 1, nb, LANE), lambda b, h, p: (b, h, 0, 0))],
        out_specs=[pl.BlockSpec((1, L, LANE), lambda b, h, p: (b, p, h)),
                   pl.BlockSpec((1, L, LANE), lambda b, h, p: (b, half - 1 - p, h))],
        out_shape=[jax.ShapeDtypeStruct((B, S // 2, W), BF16), jax.ShapeDtypeStruct((B, S // 2, W), BF16)],
        scratch_shapes=[pltpu.VMEM((2, L, 2 * LANE), BF16),
                        pltpu.VMEM((nb + 1, L, L), F32),
                        pltpu.VMEM((2, L, LANE), F32),
                        pltpu.VMEM((2, L, LANE), F32),
                        pltpu.VMEM((2, L, LANE), F32),
                        pltpu.VMEM((2, L, LANE), F32)],
        compiler_params=_params("parallel", "parallel", "arbitrary"),
        name="moba_attn",
    )(q, q, k_aug, v, kmean)
    return jnp.concatenate([lo, hi], axis=1)


def _mem_attn_kernel(q_ref, k_ref, v_ref, gq_ref, gk_ref, o_ref):
    cq = _rms(q_ref[0], gq_ref[...]).astype(BF16)
    ck = _rms(k_ref[0], gk_ref[...]).astype(BF16)
    s = lax.dot_general(cq, ck, NT, preferred_element_type=F32) * (MEM_HEAD_DIM ** -0.5)
    m = jnp.max(s, axis=-1, keepdims=True)
    p = jnp.exp(s - m)
    l = jnp.sum(p, axis=-1, keepdims=True)
    o = jnp.dot(p.astype(BF16), v_ref[0].astype(BF16), preferred_element_type=F32) / l
    o_ref[0] = o.astype(o_ref.dtype)


def _mem_attn(proj3, kv3, gq, gk, q_col):
    B, S, _ = proj3.shape
    M = kv3.shape[1]
    hd = MEM_HEAD_DIM
    tq = 1024
    return pl.pallas_call(
        _mem_attn_kernel,
        grid=(B, MEM_HEADS, S // tq),
        in_specs=[pl.BlockSpec((1, tq, hd), lambda b, h, i: (b, i, q_col + h)),
                  pl.BlockSpec((1, M, hd), lambda b, h, i: (b, 0, h)),
                  pl.BlockSpec((1, M, hd), lambda b, h, i: (b, 0, MEM_HEADS + h)),
                  pl.BlockSpec((1, hd), lambda b, h, i: (0, 0)),
                  pl.BlockSpec((1, hd), lambda b, h, i: (0, 0))],
        out_specs=pl.BlockSpec((1, tq, hd), lambda b, h, i: (b, i, h)),
        out_shape=jax.ShapeDtypeStruct((B, S, MEM_HEADS * hd), BF16),
        compiler_params=_params("parallel", "parallel", "parallel"),
        name="mem_attn",
    )(proj3, kv3, kv3, gq, gk)


def _dn_prep_kernel(x_ref, w_ref, o_ref, pad_ref):
    S = x_ref.shape[1]
    cb = pl.program_id(1)
    x = x_ref[0]
    pad_ref[0:8, :] = jnp.zeros((8, LANE), F32)
    pad_ref[8:, :] = x
    w = w_ref[...]
    y = w[DN_CONV - 1:DN_CONV, :] * x
    for j in range(DN_CONV - 1):
        off = 8 - (DN_CONV - 1) + j
        y = y + w[j:j + 1, :] * pad_ref[off:off + S, :]
    y = y * jax.nn.sigmoid(y)
    nrm = lax.rsqrt(jnp.sum(y * y, axis=-1, keepdims=True) + EPS)
    scale = jnp.where(cb < DN_HEADS, nrm * (LANE ** -0.5), jnp.where(cb < 2 * DN_HEADS, nrm, 1.0))
    o_ref[0] = y * scale


def _dn_prep(proj3, conv_w, col0):
    B, S, _ = proj3.shape
    ncb = 3 * DN_HEADS
    return pl.pallas_call(
        _dn_prep_kernel,
        grid=(B, ncb),
        in_specs=[pl.BlockSpec((1, S, LANE), lambda b, c: (b, 0, col0 + c)),
                  pl.BlockSpec((DN_CONV, LANE), lambda b, c: (0, c))],
        out_specs=pl.BlockSpec((1, S, LANE), lambda b, c: (b, 0, c)),
        out_shape=jax.ShapeDtypeStruct((B, S, ncb * LANE), F32),
        scratch_shapes=[pltpu.VMEM((S + 8, LANE), F32)],
        compiler_params=_params("parallel", "parallel"),
        name="dn_prep",
    )(proj3, conv_w)


def _softplus(x):
    return jnp.maximum(x, 0.0) + jnp.log1p(jnp.exp(-jnp.abs(x)))


def _split_bf16(a):
    hi = a.astype(BF16)
    return hi, (a - hi.astype(F32)).astype(BF16)


def _mm_split3(a, b):
    ah, al = _split_bf16(a)
    bh, bl = _split_bf16(b)
    return jnp.dot(jnp.concatenate([ah, ah, al], axis=1), jnp.concatenate([bh, bl, bh], axis=0),
                   preferred_element_type=F32)


def _dn_scan_kernel(q_ref, k_ref, v_ref, b_ref, a_ref, at_ref, alr_ref, dtr_ref, alc_ref, dtc_ref,
                    z_ref, g_ref, o_ref, st_ref):
    C = DN_CHUNK

    @pl.when(pl.program_id(1) == 0)
    def _():
        st_ref[...] = jnp.zeros_like(st_ref)

    ii = lax.broadcasted_iota(jnp.int32, (C, C), 0)
    jj = lax.broadcasted_iota(jnp.int32, (C, C), 1)
    incl = ii >= jj
    strict = ii > jj
    tril = incl.astype(F32)
    triu = (ii <= jj).astype(F32)

    g_col = -jnp.exp(alr_ref[...]) * _softplus(a_ref[0] + dtr_ref[...])
    g_row = -jnp.exp(alc_ref[...]) * _softplus(at_ref[0, 0] + dtc_ref[...])
    gc_col = jnp.dot(tril, g_col, precision=HI, preferred_element_type=F32)
    gc_row = jnp.dot(g_row, triu, precision=HI, preferred_element_type=F32)
    beta = jax.nn.sigmoid(b_ref[0])
    gn = g_ref[...]

    def mm(a, b):
        return jnp.dot(a.astype(BF16), b.astype(BF16), preferred_element_type=F32)

    heads = range(DN_HEADS)
    sl = [slice(h * LANE, (h + 1) * LANE) for h in heads]
    q = [q_ref[0, :, sl[h]] for h in heads]
    k = [k_ref[0, :, sl[h]] for h in heads]
    gcc = [gc_col[:, h:h + 1] for h in heads]
    decay = [jnp.where(incl, jnp.exp(jnp.where(incl, gcc[h] - gc_row[h:h + 1, :], 0.0)), 0.0) for h in heads]
    kb = [k[h] * beta[:, h:h + 1] for h in heads]
    vb = [v_ref[0, :, sl[h]] * beta[:, h:h + 1] for h in heads]
    kbf = [k[h].astype(BF16) for h in heads]
    kk = [lax.dot_general(kb[h].astype(BF16), kbf[h], NT, preferred_element_type=F32) for h in heads]
    qk = [lax.dot_general(q[h].astype(BF16), kbf[h], NT, preferred_element_type=F32) for h in heads]
    n = [jnp.where(strict, kk[h] * decay[h], 0.0) for h in heads]
    attn = [jnp.where(incl, qk[h] * decay[h], 0.0) for h in heads]
    egc = [jnp.exp(gcc[h]) for h in heads]
    x = [jnp.concatenate([vb[h], kb[h] * egc[h]], axis=1) for h in heads]
    nx = [_mm_split3(n[h], x[h]) for h in heads]
    p = [_mm_split3(n[h], n[h]) for h in heads]
    x = [x[h] - nx[h] for h in heads]
    levels = int(math.log2(C)) - 1
    for lvl in range(levels):
        px = [_mm_split3(p[h], x[h]) for h in heads]
        if lvl + 1 < levels:
            p = [_mm_split3(p[h], p[h]) for h in heads]
        x = [x[h] + px[h] for h in heads]
    s = [st_ref[h] for h in heads]
    sb = [s[h].astype(BF16) for h in heads]
    ws = [mm(x[h][:, LANE:], sb[h]) for h in heads]
    qs = [mm(q[h] * egc[h], sb[h]) for h in heads]
    v_new = [(x[h][:, :LANE] - ws[h]).astype(BF16) for h in heads]
    av = [mm(attn[h], v_new[h]) for h in heads]
    g_last = [gcc[h][C - 1:C, :] for h in heads]
    kd = [(k[h] * jnp.exp(g_last[h] - gcc[h])).astype(BF16) for h in heads]
    kv = [lax.dot_general(kd[h], v_new[h], TN, preferred_element_type=F32) for h in heads]
    for h in heads:
        st_ref[h] = s[h] * jnp.exp(g_last[h]) + kv[h]
        zz = z_ref[0, :, sl[h]]
        o_ref[0, :, sl[h]] = (_rms(qs[h] + av[h], gn) * (zz * jax.nn.sigmoid(zz))).astype(o_ref.dtype)


def _dn_scan(qkv, braw, araw, araw_t, a_log, dt_bias, proj3, z_col, onorm_g):
    B, S, _ = qkv.shape
    H = DN_HEADS
    C = DN_CHUNK
    W = H * LANE
    tile = lambda col: pl.BlockSpec((1, C, W), lambda b, c: (b, c, col))
    small = lambda shape: pl.BlockSpec(shape, lambda b, c: (0,) * len(shape))
    return pl.pallas_call(
        _dn_scan_kernel,
        grid=(B, S // C),
        in_specs=[tile(0), tile(1), tile(2),
                  pl.BlockSpec((1, C, H), lambda b, c: (b, c, 0)),
                  pl.BlockSpec((1, C, H), lambda b, c: (b, c, 0)),
                  pl.BlockSpec((1, 1, H, C), lambda b, c: (b, c, 0, 0)),
                  small((1, H)), small((1, H)), small((H, 1)), small((H, 1)),
                  tile(z_col), small((1, LANE))],
        out_specs=tile(0),
        out_shape=jax.ShapeDtypeStruct((B, S, W), BF16),
        scratch_shapes=[pltpu.VMEM((H, LANE, LANE), F32)],
        compiler_params=_params("parallel", "arbitrary"),
        name="dn_scan",
    )(qkv, qkv, qkv, braw, araw, araw_t, a_log.reshape(1, H), dt_bias.reshape(1, H),
      a_log.reshape(H, 1), dt_bias.reshape(H, 1), proj3, onorm_g)


def _merge_kernel(x_ref, b0_ref, b1_ref, b2_ref, g0_ref, g1_ref, g2_ref, wb_ref, wo_ref, o_ref):
    mixed = None
    for i, (b_ref, g_ref) in enumerate(((b0_ref, g0_ref), (b1_ref, g1_ref), (b2_ref, g2_ref))):
        bp = jnp.dot(b_ref[...], wb_ref[i], preferred_element_type=F32)
        t = jax.nn.sigmoid(g_ref[...]) * bp
        mixed = t if mixed is None else mixed + t
    o_ref[...] = x_ref[...] + jnp.dot(mixed.astype(BF16), wo_ref[...], preferred_element_type=F32)


def _merge(x2, moba_out, dn_out, mem_out, proj, gate_col, w_branch, w_out):
    T, D = x2.shape
    tm = 256
    row = lambda col: pl.BlockSpec((tm, D), lambda i: (i, col))
    return pl.pallas_call(
        _merge_kernel,
        grid=(T // tm,),
        in_specs=[row(0), row(0), row(0), row(0), row(gate_col), row(gate_col + 1), row(gate_col + 2),
                  pl.BlockSpec((3, D, D), lambda i: (0, 0, 0)),
                  pl.BlockSpec((D, D), lambda i: (0, 0))],
        out_specs=row(0),
        out_shape=jax.ShapeDtypeStruct((T, D), F32),
        compiler_params=_params("parallel"),
        name="merge",
    )(x2, moba_out, dn_out, mem_out, proj, proj, proj, w_branch, w_out)


def _compare_exchange(a, b):
    if a is None:
        return b, None
    if b is None:
        return a, None
    return jnp.maximum(a, b), jnp.minimum(a, b)


def _bitonic_merge_desc(xs):
    n = len(xs)
    j = n // 2
    while j >= 1:
        for i in range(n):
            l = i ^ j
            if l > i:
                xs[i], xs[l] = _compare_exchange(xs[i], xs[l])
        j //= 2
    return xs


def _bitonic_sort_desc(xs):
    xs = list(xs)
    n = len(xs)
    k = 2
    while k <= n:
        j = k // 2
        while j >= 1:
            for i in range(n):
                l = i ^ j
                if l > i:
                    hi, lo = _compare_exchange(xs[i], xs[l])
                    xs[i], xs[l] = (hi, lo) if (i & k) == 0 else (lo, hi)
            j //= 2
        k *= 2
    return xs


def _top16_over_rows(pieces):
    K = PEER_TOPK
    xs = _bitonic_sort_desc(list(pieces) + [None] * (K - len(pieces)))
    for shift in (4, 2, 1):
        other = [None if x is None else pltpu.roll(x, shift, 0) for x in xs]
        merged = []
        for i in range(K):
            a, b = xs[i], other[K - 1 - i]
            merged.append(b if a is None else a if b is None else jnp.maximum(a, b))
        xs = _bitonic_merge_desc(merged)
    return xs


def _rows_from_replicated(vals, sub):
    out = vals[0]
    for r in range(1, 8):
        out = jnp.where(sub == r, vals[r], out)
    return out


def _peer_select_kernel(qp_ref, k1_ref, k2_ref, s1_ref, s2_ref, tau_ref, cc_ref):
    tm = qp_ref.shape[0]
    K = PEER_KEYS
    q = qp_ref[...]
    s1_ref[0] = lax.dot_general(k1_ref[...], q[:, :K], NT, precision=HI, preferred_element_type=F32)
    s2_ref[0] = lax.dot_general(k2_ref[...], q[:, K:], NT, precision=HI, preferred_element_type=F32)
    sub = lax.broadcasted_iota(jnp.int32, (8, LANE), 0)
    ninf = -jnp.inf
    for c in range(tm // LANE):
        cs = slice(c * LANE, (c + 1) * LANE)
        v1 = _top16_over_rows([s1_ref[0, 8 * r:8 * r + 8, cs] for r in range(K // 8)])
        v2 = _top16_over_rows([s2_ref[0, 8 * r:8 * r + 8, cs] for r in range(K // 8)])
        v1lo, v1hi = _rows_from_replicated(v1[:8], sub), _rows_from_replicated(v1[8:], sub)
        v2lo, v2hi = _rows_from_replicated(v2[:8], sub), _rows_from_replicated(v2[8:], sub)
        cands = [
            v1[0] + v2lo, v1[0] + v2hi,
            v1[1] + v2lo,
            jnp.where(sub >= 2, v2[0] + v1lo, ninf), v2[0] + v1hi,
            jnp.where(sub >= 2, v2[1] + v1lo, ninf),
            jnp.where((sub >= 2) & (sub <= 4), v1[2] + v2lo, ninf),
            jnp.where((sub >= 2) & (sub <= 3), v1[3] + v2lo, ninf),
            jnp.where(sub == 2, v1[4] + v2lo, ninf),
        ]
        top = _top16_over_rows(cands)
        smax = top[0]
        z = jnp.exp(top[0] - smax)
        for t in top[1:]:
            z = z + jnp.exp(t - smax)
        tau_ref[0, :, cs] = top[PEER_TOPK - 1][0:1, :]
        cc_ref[0, :, cs] = (smax + jnp.log(z))[0:1, :]


def _peer_select(qp, keys1, keys2):
    T = qp.shape[0]
    H, K = PEER_HEADS, PEER_KEYS
    tm = 512
    return pl.pallas_call(
        _peer_select_kernel,
        grid=(T // tm, H),
        in_specs=[pl.BlockSpec((tm, 2 * K), lambda i, h: (i, h)),
                  pl.BlockSpec((K, K), lambda i, h: (0, 0)),
                  pl.BlockSpec((K, K), lambda i, h: (0, 0))],
        out_specs=[pl.BlockSpec((1, K, tm), lambda i, h: (h, 0, i)),
                   pl.BlockSpec((1, K, tm), lambda i, h: (h, 0, i)),
                   pl.BlockSpec((1, 1, tm), lambda i, h: (h, 0, i)),
                   pl.BlockSpec((1, 1, tm), lambda i, h: (h, 0, i))],
        out_shape=[jax.ShapeDtypeStruct((H, K, T), F32), jax.ShapeDtypeStruct((H, K, T), F32),
                   jax.ShapeDtypeStruct((H, 1, T), F32), jax.ShapeDtypeStruct((H, 1, T), F32)],
        compiler_params=_params("parallel", "parallel"),
        name="peer_select",
    )(qp, keys1, keys2)


GELU_C0 = math.sqrt(2.0 / math.pi)
GELU_C1 = GELU_C0 * 0.044715
PEER_ROWS = 256
PEER_LANES = 256


def _peer_dense_kernel(h_ref, u_ref, vt_ref, s1_ref, s2_ref, tau_ref, cc_ref, x_ref, o_ref,
                       e1_ref, e2_ref, taub_ref, rowb_ref, g_ref, w_ref, acc_ref):
    K = PEER_KEYS
    H = PEER_HEADS
    SUB = 8
    j = pl.program_id(1)
    te, tm = g_ref.shape
    n_a = te // K

    @pl.when(j == 0)
    def _():
        acc_ref[...] = jnp.zeros_like(acc_ref)
        for h in range(H):
            s1 = s1_ref[h]
            m1 = jnp.max(s1, axis=0, keepdims=True)
            e1_ref[h] = jnp.exp(s1 - m1)
            e2_ref[h] = 0.5 * jnp.exp(s2_ref[h] - (cc_ref[h] - m1))
            taub_ref[h] = jnp.broadcast_to(tau_ref[h], (SUB, tm))

    g_ref[...] = lax.dot_general(u_ref[...], h_ref[...], NT, preferred_element_type=F32)

    a0 = pl.multiple_of(j * n_a, n_a)
    for h in range(H):
        s1t = s1_ref[h, pl.ds(a0, n_a), :]
        e1t = e1_ref[h, pl.ds(a0, n_a), :]
        for r in range(n_a):
            rowb_ref[h, r, 0, :, 0:tm] = jnp.broadcast_to(s1t[r:r + 1, :], (SUB, tm))
            rowb_ref[h, r, 1, :, 0:tm] = jnp.broadcast_to(e1t[r:r + 1, :], (SUB, tm))

    def slab(sb, carry):
        b0 = pl.multiple_of(sb * SUB, SUB)
        for l0 in range(0, tm, PEER_LANES):
            ls = slice(l0, l0 + PEER_LANES)
            w = [None] * n_a
            for h in range(H):
                s2s = s2_ref[h, pl.ds(b0, SUB), ls]
                e2s = e2_ref[h, pl.ds(b0, SUB), ls]
                tb = taub_ref[h, :, ls]
                for r in range(n_a):
                    t = jnp.where(s2s + rowb_ref[h, r, 0, :, ls] >= tb, e2s * rowb_ref[h, r, 1, :, ls], 0.0)
                    w[r] = t if w[r] is None else w[r] + t
            for r in range(n_a):
                w_ref[r, pl.ds(b0, SUB), ls] = w[r]
        return carry

    lax.fori_loop(0, K // SUB, slab, 0)

    total = None
    for c in range(te // PEER_ROWS):
        rows = slice(c * PEER_ROWS, (c + 1) * PEER_ROWS)
        x = g_ref[rows, :]
        th = jnp.tanh(x * (GELU_C0 + GELU_C1 * (x * x)))
        w = w_ref[c * (PEER_ROWS // K):(c + 1) * (PEER_ROWS // K)].reshape(PEER_ROWS, tm)
        wa = (w * (x + x * th)).astype(BF16)
        d = jnp.dot(vt_ref[:, rows], wa, preferred_element_type=F32)
        total = d if total is None else total + d
    acc_ref[...] += total

    @pl.when(j == pl.num_programs(1) - 1)
    def _():
        o_ref[...] = x_ref[...] + acc_ref[...].T


def _peer_dense(h2, u, vt, s1, s2, tau, cc, x1):
    T, D = h2.shape
    E = u.shape[0]
    H, K = PEER_HEADS, PEER_KEYS
    tm, te = 512, 1024
    return pl.pallas_call(
        _peer_dense_kernel,
        grid=(T // tm, E // te),
        in_specs=[pl.BlockSpec((tm, D), lambda i, j: (i, 0)),
                  pl.BlockSpec((te, D), lambda i, j: (j, 0)),
                  pl.BlockSpec((D, te), lambda i, j: (0, j)),
                  pl.BlockSpec((H, K, tm), lambda i, j: (0, 0, i)),
                  pl.BlockSpec((H, K, tm), lambda i, j: (0, 0, i)),
                  pl.BlockSpec((H, 1, tm), lambda i, j: (0, 0, i)),
                  pl.BlockSpec((H, 1, tm), lambda i, j: (0, 0, i)),
                  pl.BlockSpec((tm, D), lambda i, j: (i, 0))],
        out_specs=pl.BlockSpec((tm, D), lambda i, j: (i, 0)),
        out_shape=jax.ShapeDtypeStruct((T, D), F32),
        scratch_shapes=[pltpu.VMEM((H, K, tm), F32),
                        pltpu.VMEM((H, K, tm), F32),
                        pltpu.VMEM((H, 8, tm), F32),
                        pltpu.VMEM((H, te // K, 2, 8, tm + LANE), F32),
                        pltpu.VMEM((te, tm), F32),
                        pltpu.VMEM((te // K, K, tm), F32),
                        pltpu.VMEM((D, tm), F32)],
        compiler_params=_params("parallel", "arbitrary"),
        name="peer_dense",
    )(h2, u, vt, s1, s2, tau, cc, x1)


def _rope_tables(S):
    inv = 1.0 / (ROPE_THETA ** (jnp.arange(0, LANE, 2, dtype=F32) / LANE))
    ang = jnp.arange(S, dtype=F32)[:, None] * inv[None, :]
    cos, sin = jnp.cos(ang), jnp.sin(ang)
    return jnp.concatenate([cos, cos], axis=1), jnp.concatenate([-sin, sin], axis=1)


def _layer(x, mem, attn_norm_g, mem_norm_g, ffn_norm_g, w_in, moba_q_norm_g, moba_k_norm_g,
           dn_conv_w, dn_a_log, dn_dt_bias, dn_out_norm_g, w_mem_kv, mem_q_norm_g, mem_k_norm_g,
           w_branch, w_out, peer_w_q, peer_keys1, peer_keys2, peer_u, peer_v):
    B, S, D = x.shape
    T = B * S
    x2 = x.reshape(T, D)
    row = lambda g: g.reshape(1, -1)

    moba_w, dn_w = 3 * MOBA_HEADS * LANE, 3 * DN_HEADS * LANE
    n_small = 2 * DN_HEADS
    o_small = moba_w + dn_w
    w_main = jnp.concatenate([w_in[:, :o_small], w_in[:, o_small + n_small:]], axis=1).astype(BF16)
    w_small = jnp.pad(w_in[:, o_small:o_small + n_small], ((0, 0), (0, LANE - n_small)))
    proj = _norm_matmul(x2, row(attn_norm_g), w_main, tm=1024, tn=512, name="in_proj")
    small = _norm_matmul(x2, row(attn_norm_g), w_small, tm=1024, tn=LANE, precision=HI, name="in_proj_scalars")
    proj3 = proj.reshape(B, S, -1)
    dn_col = moba_w // LANE
    z_col = (moba_w + dn_w) // D
    memq_col = (moba_w + dn_w + D) // MEM_HEAD_DIM
    gate_col = (moba_w + dn_w + 2 * D) // D

    cos2, sin2 = _rope_tables(S)
    mq, mk, mv, kmean = _moba_prep(proj3, row(moba_q_norm_g), row(moba_k_norm_g), cos2, sin2)
    moba_out = _moba_attn(mq, mk, mv, kmean)

    dn_qkv = _dn_prep(proj3, dn_conv_w, dn_col)
    braw = small[:, :DN_HEADS].reshape(B, S, DN_HEADS)
    araw = small[:, DN_HEADS:n_small].reshape(B, S, DN_HEADS)
    araw_t = araw.reshape(B, S // DN_CHUNK, DN_CHUNK, DN_HEADS).transpose(0, 1, 3, 2)
    dn_out = _dn_scan(dn_qkv, braw, araw, araw_t, dn_a_log, dn_dt_bias, proj3, z_col, row(dn_out_norm_g))

    M = mem.shape[1]
    kv = _norm_matmul(mem.reshape(B * M, D), row(mem_norm_g), w_mem_kv.astype(BF16), tm=B * M, tn=512, name="mem_kv")
    mem_out = _mem_attn(proj3, kv.reshape(B, M, -1), row(mem_q_norm_g), row(mem_k_norm_g), memq_col)

    x1 = _merge(x2, moba_out.reshape(T, D), dn_out.reshape(T, D), mem_out.reshape(T, D), proj, gate_col,
                w_branch.astype(BF16), w_out.astype(BF16))

    qp, h2 = _norm_matmul(x1, row(ffn_norm_g), peer_w_q.astype(BF16), tm=1024, tn=512, emit_h=True, name="peer_query")
    s1, s2, tau, cc = _peer_select(qp, peer_keys1, peer_keys2)
    out = _peer_dense(h2, peer_u.astype(BF16), peer_v.T.astype(BF16), s1, s2, tau, cc, x1)
    return out.reshape(B, S, D)


def kernel(x, mem, attn_norm_g, mem_norm_g, ffn_norm_g, w_in, moba_q_norm_g, moba_k_norm_g, dn_conv_w, dn_a_log, dn_dt_bias, dn_out_norm_g, w_mem_kv, mem_q_norm_g, mem_k_norm_g, w_branch, w_out, peer_w_q, peer_keys1, peer_keys2, peer_u, peer_v):
    for l in range(w_in.shape[0]):
        x = _layer(x, mem, attn_norm_g[l], mem_norm_g[l], ffn_norm_g[l], w_in[l], moba_q_norm_g[l],
                   moba_k_norm_g[l], dn_conv_w[l], dn_a_log[l], dn_dt_bias[l], dn_out_norm_g[l],
                   w_mem_kv[l], mem_q_norm_g[l], mem_k_norm_g[l], w_branch[l], w_out[l], peer_w_q[l],
                   peer_keys1[l], peer_keys2[l], peer_u[l], peer_v[l])
    return x
```

```python
import functools
import math

import jax
import jax.numpy as jnp
from jax import lax
from jax.experimental import pallas as pl
from jax.experimental.pallas import tpu as pltpu

F32 = jnp.float32
BF16 = jnp.bfloat16
HI = lax.Precision.HIGHEST
EPS = 1e-6
ROPE_THETA = 10000.0
NT = (((1,), (1,)), ((), ()))
TN = (((0,), (0,)), ((), ()))

LANE = 128
MOBA_HEADS = 8
MOBA_BLOCK = 256
MOBA_TOPK = 3
DN_HEADS = 8
DN_CHUNK = 64
DN_CONV = 4
MEM_HEADS = 4
MEM_HEAD_DIM = 256
PEER_HEADS = 8
PEER_KEYS = 128
PEER_TOPK = 16

VMEM_LIMIT = 56 * 1024 * 1024


def _params(*sem):
    return pltpu.CompilerParams(dimension_semantics=sem, vmem_limit_bytes=VMEM_LIMIT)


def _rms(x, g):
    ms = jnp.mean(x * x, axis=-1, keepdims=True)
    return x * lax.rsqrt(ms + EPS) * g


def _norm_matmul_kernel(x_ref, g_ref, w_ref, o_ref, *rest, precision, emit_h):
    if emit_h:
        ho_ref, h_ref = rest
    else:
        (h_ref,) = rest

    @pl.when(pl.program_id(1) == 0)
    def _():
        h = _rms(x_ref[...], g_ref[...])
        h_ref[...] = h.astype(h_ref.dtype)
        if emit_h:
            ho_ref[...] = h.astype(ho_ref.dtype)

    o_ref[...] = jnp.dot(h_ref[...], w_ref[...], precision=precision,
                         preferred_element_type=F32).astype(o_ref.dtype)


def _norm_matmul(x, g, w, *, tm, tn, name, out_dtype=F32, precision=None, emit_h=False):
    T, D = x.shape
    N = w.shape[1]
    out_shape = [jax.ShapeDtypeStruct((T, N), out_dtype)]
    out_specs = [pl.BlockSpec((tm, tn), lambda i, j: (i, j))]
    if emit_h:
        out_shape.append(jax.ShapeDtypeStruct((T, D), BF16))
        out_specs.append(pl.BlockSpec((tm, D), lambda i, j: (i, 0)))
    res = pl.pallas_call(
        functools.partial(_norm_matmul_kernel, precision=precision, emit_h=emit_h),
        grid=(T // tm, N // tn),
        in_specs=[pl.BlockSpec((tm, D), lambda i, j: (i, 0)),
                  pl.BlockSpec((1, D), lambda i, j: (0, 0)),
                  pl.BlockSpec((D, tn), lambda i, j: (0, j))],
        out_specs=out_specs,
        out_shape=out_shape,
        scratch_shapes=[pltpu.VMEM((tm, D), w.dtype)],
        compiler_params=_params("parallel", "arbitrary"),
        name=name,
    )(x, g, w)
    return res if emit_h else res[0]


MOBA_NEG = -1e30


def _moba_prep_kernel(q_ref, k_ref, gq_ref, gk_ref, cos_ref, sin_ref, qo_ref, ko_ref, km_ref):
    cos = cos_ref[...]
    sin = sin_ref[...]

    def norm_rope(x, g):
        y = _rms(x, g)
        return y * cos + pltpu.roll(y, LANE // 2, 1) * sin

    q = norm_rope(q_ref[0], gq_ref[...])
    k = norm_rope(k_ref[0], gk_ref[...])
    ts = k.shape[0]
    nb = ts // MOBA_BLOCK
    qo_ref[0] = q
    row = lax.broadcasted_iota(jnp.int32, (ts, LANE), 0) + pl.program_id(2) * ts
    lane = lax.broadcasted_iota(jnp.int32, (ts, LANE), 1)
    ko_ref[0, :, 0:LANE] = k.astype(BF16)
    ko_ref[0, :, LANE:2 * LANE] = jnp.where(lane == row // MOBA_BLOCK, 1.0, 0.0).astype(BF16)
    km_ref[0, 0] = jnp.mean(k.reshape(nb, MOBA_BLOCK, LANE), axis=1)


def _moba_prep(proj3, gq, gk, cos2, sin2):
    B, S, _ = proj3.shape
    H = MOBA_HEADS
    ts = 2048
    nb_t = ts // MOBA_BLOCK
    col = lambda off: pl.BlockSpec((1, ts, LANE), lambda b, h, s: (b, s, off + h))
    return pl.pallas_call(
        _moba_prep_kernel,
        grid=(B, H, S // ts),
        in_specs=[col(0), col(H),
                  pl.BlockSpec((1, LANE), lambda b, h, s: (0, 0)),
                  pl.BlockSpec((1, LANE), lambda b, h, s: (0, 0)),
                  pl.BlockSpec((ts, LANE), lambda b, h, s: (s, 0)),
                  pl.BlockSpec((ts, LANE), lambda b, h, s: (s, 0))],
        out_specs=[col(0),
                   pl.BlockSpec((1, ts, 2 * LANE), lambda b, h, s: (b, s, h)),
                   pl.BlockSpec((1, 1, nb_t, LANE), lambda b, h, s: (b, h, s, 0))],
        out_shape=[jax.ShapeDtypeStruct((B, S, H * LANE), F32),
                   jax.ShapeDtypeStruct((B, S, H * 2 * LANE), BF16),
                   jax.ShapeDtypeStruct((B, H, S // MOBA_BLOCK, LANE), F32)],
        compiler_params=_params("parallel", "parallel", "parallel"),
        name="moba_prep",
    )(proj3, proj3, gq, gk, cos2, sin2)


def _moba_gate_kernel(q_ref, km_ref, o_ref):
    L = MOBA_BLOCK
    q = q_ref[0]
    km = km_ref[0, 0]
    nb = km.shape[0]
    S = q.shape[0]
    gate = lax.dot_general(km, q, NT, precision=HI, preferred_element_type=F32)
    row = lax.broadcasted_iota(jnp.int32, (nb, S), 0)
    own = lax.broadcasted_iota(jnp.int32, (nb, S), 1) // L
    rank = jnp.zeros((nb, S), jnp.int32)
    for m in range(nb - 1):
        gm = gate[m:m + 1, :]
        cnt = jnp.where(row > m, jnp.where(gm >= gate, 1, 0), jnp.where(gm > gate, 1, 0))
        rank = rank + jnp.where(own > m, cnt, 0)
    keep = jnp.where(row < own, jnp.where(rank < MOBA_TOPK, 1.0, 0.0), jnp.where(row == own, 1.0, 0.0))
    eye = (lax.broadcasted_iota(jnp.int32, (nb, LANE), 0)
           == lax.broadcasted_iota(jnp.int32, (nb, LANE), 1)).astype(BF16)
    keep_t = lax.dot_general(keep.astype(BF16), eye, TN, preferred_element_type=F32)
    o_ref[0, :, 0:LANE] = (q * (LANE ** -0.5 * math.log2(math.e))).astype(BF16)
    o_ref[0, :, LANE:2 * LANE] = jnp.where(keep_t > 0.5, 0.0, MOBA_NEG).astype(BF16)


def _moba_gate(q, kmean):
    B, S, W = q.shape
    H = MOBA_HEADS
    nb = kmean.shape[2]
    return pl.pallas_call(
        _moba_gate_kernel,
        grid=(B, H),
        in_specs=[pl.BlockSpec((1, S, LANE), lambda b, h: (b, 0, h)),
                  pl.BlockSpec((1, 1, nb, LANE), lambda b, h: (b, h, 0, 0))],
        out_specs=pl.BlockSpec((1, S, 2 * LANE), lambda b, h: (b, 0, h)),
        out_shape=jax.ShapeDtypeStruct((B, S, 2 * W), BF16),
        compiler_params=_params("parallel", "parallel"),
        name="moba_gate",
    )(q, kmean)


def _moba_attn_kernel(qa_ref, qb_ref, k_ref, v_ref, oa_ref, ob_ref, qaug_ref, s_ref):
    L = MOBA_BLOCK
    nb = k_ref.shape[1] // L
    half = nb // 2
    p = pl.program_id(2)
    tiles = (p, nb - 1 - p)
    qaug_ref[0] = qa_ref[0]
    qaug_ref[1] = qb_ref[0]

    def scores(qa, n):
        kb = k_ref[0, pl.ds(pl.multiple_of(n * L, L), L), :]
        return lax.dot_general(qa, kb, NT, preferred_element_type=F32)

    def half_max(s):
        return jnp.maximum(s[:, :LANE], s[:, LANE:])

    def dyn_slot(j):
        first = j < p
        return first, jnp.where(first, 0, 1), jnp.where(first, j, half + j - p)

    r = lax.broadcasted_iota(jnp.int32, (L, L), 0)
    c = lax.broadcasted_iota(jnp.int32, (L, L), 1)
    causal = jnp.where(c <= r, 0.0, MOBA_NEG)

    m = [None, None]
    for t in range(2):
        s = scores(qaug_ref[t], tiles[t]) + causal
        s_ref[nb - 1 + t] = s
        m[t] = half_max(s)
    for n in range(half):
        s = scores(qaug_ref[1], n)
        s_ref[n] = s
        m[1] = jnp.maximum(m[1], half_max(s))
    for j in range(half - 1):
        first, t, n = dyn_slot(j)
        s = scores(qaug_ref[t], n)
        s_ref[half + j] = s
        hm = half_max(s)
        m[0] = jnp.maximum(m[0], jnp.where(first, hm, MOBA_NEG))
        m[1] = jnp.maximum(m[1], jnp.where(first, MOBA_NEG, hm))
    mb = [jnp.broadcast_to(jnp.max(m[t], axis=-1, keepdims=True), (L, LANE)) for t in range(2)]

    def probs(k, mbt):
        s = s_ref[k]
        return jnp.concatenate([jnp.exp2(s[:, :LANE] - mbt), jnp.exp2(s[:, LANE:] - mbt)], axis=1)

    def value(n):
        return v_ref[0, pl.ds(pl.multiple_of(n * L, L), L), :]

    l = [None, None]
    acc = [None, None]
    for t in range(2):
        pr = probs(nb - 1 + t, mb[t])
        l[t] = pr[:, :LANE] + pr[:, LANE:]
        acc[t] = jnp.dot(pr.astype(BF16), value(tiles[t]), preferred_element_type=F32)
    for n in range(half):
        pr = probs(n, mb[1])
        l[1] = l[1] + (pr[:, :LANE] + pr[:, LANE:])
        acc[1] = acc[1] + jnp.dot(pr.astype(BF16), value(n), preferred_element_type=F32)
    for j in range(half - 1):
        first, t, n = dyn_slot(j)
        pr = probs(half + j, jnp.where(first, mb[0], mb[1]))
        ps = pr[:, :LANE] + pr[:, LANE:]
        d = jnp.dot(pr.astype(BF16), value(n), preferred_element_type=F32)
        l[0] = l[0] + jnp.where(first, ps, 0.0)
        l[1] = l[1] + jnp.where(first, 0.0, ps)
        acc[0] = acc[0] + jnp.where(first, d, 0.0)
        acc[1] = acc[1] + jnp.where(first, 0.0, d)

    for t, o_ref in enumerate((oa_ref, ob_ref)):
        o_ref[0] = (acc[t] / jnp.sum(l[t], axis=-1, keepdims=True)).astype(o_ref.dtype)


def _moba_attn(q_aug, k_aug, v):
    B, S, _ = v.shape
    H = MOBA_HEADS
    W = H * LANE
    L = MOBA_BLOCK
    nb = S // L
    half = nb // 2
    lo, hi = pl.pallas_call(
        _moba_attn_kernel,
        grid=(B, H, half),
        in_specs=[pl.BlockSpec((1, L, 2 * LANE), lambda b, h, p: (b, p, h)),
                  pl.BlockSpec((1, L, 2 * LANE), lambda b, h, p: (b, nb - 1 - p, h)),
                  pl.BlockSpec((1, S, 2 * LANE), lambda b, h, p: (b, 0, h)),
                  pl.BlockSpec((1, S, LANE), lambda b, h, p: (b, 0, h))],
        out_specs=[pl.BlockSpec((1, L, LANE), lambda b, h, p: (b, p, h)),
                   pl.BlockSpec((1, L, LANE), lambda b, h, p: (b, half - 1 - p, h))],
        out_shape=[jax.ShapeDtypeStruct((B, S // 2, W), BF16), jax.ShapeDtypeStruct((B, S // 2, W), BF16)],
        scratch_shapes=[pltpu.VMEM((2, L, 2 * LANE), BF16),
                        pltpu.VMEM((nb + 1, L, L), F32)],
        compiler_params=_params("parallel", "parallel", "arbitrary"),
        name="moba_attn",
    )(q_aug, q_aug, k_aug, v)
    return jnp.concatenate([lo, hi], axis=1)


def _mem_attn_kernel(q_ref, k_ref, v_ref, gq_ref, gk_ref, o_ref):
    cq = _rms(q_ref[0], gq_ref[...]).astype(BF16)
    ck = _rms(k_ref[0], gk_ref[...]).astype(BF16)
    s = lax.dot_general(cq, ck, NT, preferred_element_type=F32) * (MEM_HEAD_DIM ** -0.5)
    m = jnp.max(s, axis=-1, keepdims=True)
    p = jnp.exp(s - m)
    l = jnp.sum(p, axis=-1, keepdims=True)
    o = jnp.dot(p.astype(BF16), v_ref[0].astype(BF16), preferred_element_type=F32) / l
    o_ref[0] = o.astype(o_ref.dtype)


def _mem_attn(proj3, kv3, gq, gk, q_col):
    B, S, _ = proj3.shape
    M = kv3.shape[1]
    hd = MEM_HEAD_DIM
    tq = 1024
    return pl.pallas_call(
        _mem_attn_kernel,
        grid=(B, MEM_HEADS, S // tq),
        in_specs=[pl.BlockSpec((1, tq, hd), lambda b, h, i: (b, i, q_col + h)),
                  pl.BlockSpec((1, M, hd), lambda b, h, i: (b, 0, h)),
                  pl.BlockSpec((1, M, hd), lambda b, h, i: (b, 0, MEM_HEADS + h)),
                  pl.BlockSpec((1, hd), lambda b, h, i: (0, 0)),
                  pl.BlockSpec((1, hd), lambda b, h, i: (0, 0))],
        out_specs=pl.BlockSpec((1, tq, hd), lambda b, h, i: (b, i, h)),
        out_shape=jax.ShapeDtypeStruct((B, S, MEM_HEADS * hd), BF16),
        compiler_params=_params("parallel", "parallel", "parallel"),
        name="mem_attn",
    )(proj3, kv3, kv3, gq, gk)


def _dn_prep_kernel(x_ref, w_ref, o_ref, pad_ref):
    S = x_ref.shape[1]
    cb = pl.program_id(1)
    x = x_ref[0]
    pad_ref[0:8, :] = jnp.zeros((8, LANE), F32)
    pad_ref[8:, :] = x
    w = w_ref[...]
    y = w[DN_CONV - 1:DN_CONV, :] * x
    for j in range(DN_CONV - 1):
        off = 8 - (DN_CONV - 1) + j
        y = y + w[j:j + 1, :] * pad_ref[off:off + S, :]
    y = y * jax.nn.sigmoid(y)
    nrm = lax.rsqrt(jnp.sum(y * y, axis=-1, keepdims=True) + EPS)
    scale = jnp.where(cb < DN_HEADS, nrm * (LANE ** -0.5), jnp.where(cb < 2 * DN_HEADS, nrm, 1.0))
    o_ref[0] = y * scale


def _dn_prep(proj3, conv_w, col0):
    B, S, _ = proj3.shape
    ncb = 3 * DN_HEADS
    return pl.pallas_call(
        _dn_prep_kernel,
        grid=(B, ncb),
        in_specs=[pl.BlockSpec((1, S, LANE), lambda b, c: (b, 0, col0 + c)),
                  pl.BlockSpec((DN_CONV, LANE), lambda b, c: (0, c))],
        out_specs=pl.BlockSpec((1, S, LANE), lambda b, c: (b, 0, c)),
        out_shape=jax.ShapeDtypeStruct((B, S, ncb * LANE), F32),
        scratch_shapes=[pltpu.VMEM((S + 8, LANE), F32)],
        compiler_params=_params("parallel", "parallel"),
        name="dn_prep",
    )(proj3, conv_w)


def _softplus(x):
    return jnp.maximum(x, 0.0) + jnp.log1p(jnp.exp(-jnp.abs(x)))


def _split_bf16(a):
    hi = a.astype(BF16)
    return hi, (a - hi.astype(F32)).astype(BF16)


def _mm_split3(a, b):
    ah, al = _split_bf16(a)
    bh, bl = _split_bf16(b)
    return jnp.dot(jnp.concatenate([ah, ah, al], axis=1), jnp.concatenate([bh, bl, bh], axis=0),
                   preferred_element_type=F32)


def _dn_scan_kernel(q_ref, k_ref, v_ref, b_ref, a_ref, at_ref, alr_ref, dtr_ref, alc_ref, dtc_ref,
                    z_ref, g_ref, o_ref, st_ref):
    C = DN_CHUNK

    @pl.when(pl.program_id(1) == 0)
    def _():
        st_ref[...] = jnp.zeros_like(st_ref)

    ii = lax.broadcasted_iota(jnp.int32, (C, C), 0)
    jj = lax.broadcasted_iota(jnp.int32, (C, C), 1)
    incl = ii >= jj
    strict = ii > jj
    tril = incl.astype(F32)
    triu = (ii <= jj).astype(F32)

    g_col = -jnp.exp(alr_ref[...]) * _softplus(a_ref[0] + dtr_ref[...])
    g_row = -jnp.exp(alc_ref[...]) * _softplus(at_ref[0, 0] + dtc_ref[...])
    gc_col = jnp.dot(tril, g_col, precision=HI, preferred_element_type=F32)
    gc_row = jnp.dot(g_row, triu, precision=HI, preferred_element_type=F32)
    beta = jax.nn.sigmoid(b_ref[0])
    gn = g_ref[...]

    def mm(a, b):
        return jnp.dot(a.astype(BF16), b.astype(BF16), preferred_element_type=F32)

    heads = range(DN_HEADS)
    sl = [slice(h * LANE, (h + 1) * LANE) for h in heads]
    q = [q_ref[0, :, sl[h]] for h in heads]
    k = [k_ref[0, :, sl[h]] for h in heads]
    gcc = [gc_col[:, h:h + 1] for h in heads]
    decay = [jnp.where(incl, jnp.exp(jnp.where(incl, gcc[h] - gc_row[h:h + 1, :], 0.0)), 0.0) for h in heads]
    kb = [k[h] * beta[:, h:h + 1] for h in heads]
    vb = [v_ref[0, :, sl[h]] * beta[:, h:h + 1] for h in heads]
    kbf = [k[h].astype(BF16) for h in heads]
    kk = [lax.dot_general(kb[h].astype(BF16), kbf[h], NT, preferred_element_type=F32) for h in heads]
    qk = [lax.dot_general(q[h].astype(BF16), kbf[h], NT, preferred_element_type=F32) for h in heads]
    n = [jnp.where(strict, kk[h] * decay[h], 0.0) for h in heads]
    attn = [jnp.where(incl, qk[h] * decay[h], 0.0) for h in heads]
    egc = [jnp.exp(gcc[h]) for h in heads]
    x = [jnp.concatenate([vb[h], kb[h] * egc[h]], axis=1) for h in heads]
    nx = [_mm_split3(n[h], x[h]) for h in heads]
    p = [_mm_split3(n[h], n[h]) for h in heads]
    x = [x[h] - nx[h] for h in heads]
    levels = int(math.log2(C)) - 1
    for lvl in range(levels):
        px = [_mm_split3(p[h], x[h]) for h in heads]
        if lvl + 1 < levels:
            p = [_mm_split3(p[h], p[h]) for h in heads]
        x = [x[h] + px[h] for h in heads]
    s = [st_ref[h] for h in heads]
    sb = [s[h].astype(BF16) for h in heads]
    ws = [mm(x[h][:, LANE:], sb[h]) for h in heads]
    qs = [mm(q[h] * egc[h], sb[h]) for h in heads]
    v_new = [(x[h][:, :LANE] - ws[h]).astype(BF16) for h in heads]
    av = [mm(attn[h], v_new[h]) for h in heads]
    g_last = [gcc[h][C - 1:C, :] for h in heads]
    kd = [(k[h] * jnp.exp(g_last[h] - gcc[h])).astype(BF16) for h in heads]
    kv = [lax.dot_general(kd[h], v_new[h], TN, preferred_element_type=F32) for h in heads]
    for h in heads:
        st_ref[h] = s[h] * jnp.exp(g_last[h]) + kv[h]
        zz = z_ref[0, :, sl[h]].astype(F32)
        o_ref[0, :, sl[h]] = (_rms(qs[h] + av[h], gn) * (zz * jax.nn.sigmoid(zz))).astype(o_ref.dtype)


def _dn_scan(qkv, braw, araw, araw_t, a_log, dt_bias, proj3, z_col, onorm_g):
    B, S, _ = qkv.shape
    H = DN_HEADS
    C = DN_CHUNK
    W = H * LANE
    tile = lambda col: pl.BlockSpec((1, C, W), lambda b, c: (b, c, col))
    small = lambda shape: pl.BlockSpec(shape, lambda b, c: (0,) * len(shape))
    return pl.pallas_call(
        _dn_scan_kernel,
        grid=(B, S // C),
        in_specs=[tile(0), tile(1), tile(2),
                  pl.BlockSpec((1, C, H), lambda b, c: (b, c, 0)),
                  pl.BlockSpec((1, C, H), lambda b, c: (b, c, 0)),
                  pl.BlockSpec((1, 1, H, C), lambda b, c: (b, c, 0, 0)),
                  small((1, H)), small((1, H)), small((H, 1)), small((H, 1)),
                  tile(z_col), small((1, LANE))],
        out_specs=tile(0),
        out_shape=jax.ShapeDtypeStruct((B, S, W), BF16),
        scratch_shapes=[pltpu.VMEM((H, LANE, LANE), F32)],
        compiler_params=_params("parallel", "arbitrary"),
        name="dn_scan",
    )(qkv, qkv, qkv, braw, araw, araw_t, a_log.reshape(1, H), dt_bias.reshape(1, H),
      a_log.reshape(H, 1), dt_bias.reshape(H, 1), proj3, onorm_g)


def _merge_kernel(x_ref, b0_ref, b1_ref, b2_ref, g0_ref, g1_ref, g2_ref, wb_ref, wo_ref, o_ref):
    mixed = None
    for i, (b_ref, g_ref) in enumerate(((b0_ref, g0_ref), (b1_ref, g1_ref), (b2_ref, g2_ref))):
        bp = jnp.dot(b_ref[...], wb_ref[i], preferred_element_type=F32)
        t = jax.nn.sigmoid(g_ref[...].astype(F32)) * bp
        mixed = t if mixed is None else mixed + t
    o_ref[...] = x_ref[...] + jnp.dot(mixed.astype(BF16), wo_ref[...], preferred_element_type=F32)


def _merge(x2, moba_out, dn_out, mem_out, proj, gate_col, w_branch, w_out):
    T, D = x2.shape
    tm = 256
    row = lambda col: pl.BlockSpec((tm, D), lambda i: (i, col))
    return pl.pallas_call(
        _merge_kernel,
        grid=(T // tm,),
        in_specs=[row(0), row(0), row(0), row(0), row(gate_col), row(gate_col + 1), row(gate_col + 2),
                  pl.BlockSpec((3, D, D), lambda i: (0, 0, 0)),
                  pl.BlockSpec((D, D), lambda i: (0, 0))],
        out_specs=row(0),
        out_shape=jax.ShapeDtypeStruct((T, D), F32),
        compiler_params=_params("parallel"),
        name="merge",
    )(x2, moba_out, dn_out, mem_out, proj, proj, proj, w_branch, w_out)


def _compare_exchange(a, b):
    if a is None:
        return b, None
    if b is None:
        return a, None
    return jnp.maximum(a, b), jnp.minimum(a, b)


def _bitonic_merge_desc(xs):
    n = len(xs)
    j = n // 2
    while j >= 1:
        for i in range(n):
            l = i ^ j
            if l > i:
                xs[i], xs[l] = _compare_exchange(xs[i], xs[l])
        j //= 2
    return xs


def _bitonic_sort_desc(xs):
    xs = list(xs)
    n = len(xs)
    k = 2
    while k <= n:
        j = k // 2
        while j >= 1:
            for i in range(n):
                l = i ^ j
                if l > i:
                    hi, lo = _compare_exchange(xs[i], xs[l])
                    xs[i], xs[l] = (hi, lo) if (i & k) == 0 else (lo, hi)
            j //= 2
        k *= 2
    return xs


def _top16_over_rows(pieces):
    K = PEER_TOPK
    xs = _bitonic_sort_desc(list(pieces) + [None] * (K - len(pieces)))
    for shift in (4, 2, 1):
        other = [None if x is None else pltpu.roll(x, shift, 0) for x in xs]
        merged = []
        for i in range(K):
            a, b = xs[i], other[K - 1 - i]
            merged.append(b if a is None else a if b is None else jnp.maximum(a, b))
        xs = _bitonic_merge_desc(merged)
    return xs


def _rows_from_replicated(vals, sub):
    out = vals[0]
    for r in range(1, 8):
        out = jnp.where(sub == r, vals[r], out)
    return out


def _peer_select_kernel(qp_ref, k1_ref, k2_ref, s1_ref, s2_ref, tau_ref, cc_ref):
    tm = qp_ref.shape[0]
    K = PEER_KEYS
    q = qp_ref[...]
    s1_ref[0] = lax.dot_general(k1_ref[...], q[:, :K], NT, precision=HI, preferred_element_type=F32)
    s2_ref[0] = lax.dot_general(k2_ref[...], q[:, K:], NT, precision=HI, preferred_element_type=F32)
    sub = lax.broadcasted_iota(jnp.int32, (8, LANE), 0)
    ninf = -jnp.inf
    for c in range(tm // LANE):
        cs = slice(c * LANE, (c + 1) * LANE)
        v1 = _top16_over_rows([s1_ref[0, 8 * r:8 * r + 8, cs] for r in range(K // 8)])
        v2 = _top16_over_rows([s2_ref[0, 8 * r:8 * r + 8, cs] for r in range(K // 8)])
        v1lo, v1hi = _rows_from_replicated(v1[:8], sub), _rows_from_replicated(v1[8:], sub)
        v2lo, v2hi = _rows_from_replicated(v2[:8], sub), _rows_from_replicated(v2[8:], sub)
        cands = [
            v1[0] + v2lo, v1[0] + v2hi,
            v1[1] + v2lo,
            jnp.where(sub >= 2, v2[0] + v1lo, ninf), v2[0] + v1hi,
            jnp.where(sub >= 2, v2[1] + v1lo, ninf),
            jnp.where((sub >= 2) & (sub <= 4), v1[2] + v2lo, ninf),
            jnp.where((sub >= 2) & (sub <= 3), v1[3] + v2lo, ninf),
            jnp.where(sub == 2, v1[4] + v2lo, ninf),
        ]
        top = _top16_over_rows(cands)
        smax = top[0]
        z = jnp.exp(top[0] - smax)
        for t in top[1:]:
            z = z + jnp.exp(t - smax)
        tau_ref[0, :, cs] = top[PEER_TOPK - 1][0:1, :]
        cc_ref[0, :, cs] = (smax + jnp.log(z))[0:1, :]


def _peer_select(qp, keys1, keys2):
    T = qp.shape[0]
    H, K = PEER_HEADS, PEER_KEYS
    tm = 512
    return pl.pallas_call(
        _peer_select_kernel,
        grid=(T // tm, H),
        in_specs=[pl.BlockSpec((tm, 2 * K), lambda i, h: (i, h)),
                  pl.BlockSpec((K, K), lambda i, h: (0, 0)),
                  pl.BlockSpec((K, K), lambda i, h: (0, 0))],
        out_specs=[pl.BlockSpec((1, K, tm), lambda i, h: (h, 0, i)),
                   pl.BlockSpec((1, K, tm), lambda i, h: (h, 0, i)),
                   pl.BlockSpec((1, 1, tm), lambda i, h: (h, 0, i)),
                   pl.BlockSpec((1, 1, tm), lambda i, h: (h, 0, i))],
        out_shape=[jax.ShapeDtypeStruct((H, K, T), F32), jax.ShapeDtypeStruct((H, K, T), F32),
                   jax.ShapeDtypeStruct((H, 1, T), F32), jax.ShapeDtypeStruct((H, 1, T), F32)],
        compiler_params=_params("parallel", "parallel"),
        name="peer_select",
    )(qp, keys1, keys2)


GELU_C0 = math.sqrt(2.0 / math.pi)
GELU_C1 = GELU_C0 * 0.044715
PEER_ROWS = 256
PEER_LANES = 256


def _peer_dense_kernel(h_ref, u_ref, vt_ref, s1_ref, s2_ref, tau_ref, cc_ref, x_ref, o_ref,
                       e1_ref, e2_ref, taub_ref, rowb_ref, g_ref, w_ref, acc_ref):
    K = PEER_KEYS
    H = PEER_HEADS
    SUB = 8
    j = pl.program_id(1)
    te, tm = g_ref.shape
    n_a = te // K

    @pl.when(j == 0)
    def _():
        acc_ref[...] = jnp.zeros_like(acc_ref)
        for h in range(H):
            s1 = s1_ref[h]
            m1 = jnp.max(s1, axis=0, keepdims=True)
            e1_ref[h] = jnp.exp(s1 - m1)
            e2_ref[h] = 0.5 * jnp.exp(s2_ref[h] - (cc_ref[h] - m1))
            taub_ref[h] = jnp.broadcast_to(tau_ref[h], (SUB, tm))

    g_ref[...] = lax.dot_general(u_ref[...], h_ref[...], NT, preferred_element_type=F32)

    a0 = pl.multiple_of(j * n_a, n_a)
    for h in range(H):
        s1t = s1_ref[h, pl.ds(a0, n_a), :]
        e1t = e1_ref[h, pl.ds(a0, n_a), :]
        for r in range(n_a):
            rowb_ref[h, r, 0, :, 0:tm] = jnp.broadcast_to(s1t[r:r + 1, :], (SUB, tm))
            rowb_ref[h, r, 1, :, 0:tm] = jnp.broadcast_to(e1t[r:r + 1, :], (SUB, tm))

    def slab(sb, carry):
        b0 = pl.multiple_of(sb * SUB, SUB)
        for l0 in range(0, tm, PEER_LANES):
            ls = slice(l0, l0 + PEER_LANES)
            w = [None] * n_a
            for h in range(H):
                s2s = s2_ref[h, pl.ds(b0, SUB), ls]
                e2s = e2_ref[h, pl.ds(b0, SUB), ls]
                tb = taub_ref[h, :, ls]
                for r in range(n_a):
                    t = jnp.where(s2s + rowb_ref[h, r, 0, :, ls] >= tb, e2s * rowb_ref[h, r, 1, :, ls], 0.0)
                    w[r] = t if w[r] is None else w[r] + t
            for r in range(n_a):
                w_ref[r, pl.ds(b0, SUB), ls] = w[r]
        return carry

    lax.fori_loop(0, K // SUB, slab, 0, unroll=4)

    total = None
    for c in range(te // PEER_ROWS):
        rows = slice(c * PEER_ROWS, (c + 1) * PEER_ROWS)
        x = g_ref[rows, :]
        th = jnp.tanh(x * (GELU_C0 + GELU_C1 * (x * x)))
        w = w_ref[c * (PEER_ROWS // K):(c + 1) * (PEER_ROWS // K)].reshape(PEER_ROWS, tm)
        wa = (w * (x + x * th)).astype(BF16)
        d = jnp.dot(vt_ref[:, rows], wa, preferred_element_type=F32)
        total = d if total is None else total + d
    acc_ref[...] += total

    @pl.when(j == pl.num_programs(1) - 1)
    def _():
        o_ref[...] = x_ref[...] + acc_ref[...].T


def _peer_dense(h2, u, vt, s1, s2, tau, cc, x1):
    T, D = h2.shape
    E = u.shape[0]
    H, K = PEER_HEADS, PEER_KEYS
    tm, te = 512, 1024
    return pl.pallas_call(
        _peer_dense_kernel,
        grid=(T // tm, E // te),
        in_specs=[pl.BlockSpec((tm, D), lambda i, j: (i, 0)),
                  pl.BlockSpec((te, D), lambda i, j: (j, 0)),
                  pl.BlockSpec((D, te), lambda i, j: (0, j)),
                  pl.BlockSpec((H, K, tm), lambda i, j: (0, 0, i)),
                  pl.BlockSpec((H, K, tm), lambda i, j: (0, 0, i)),
                  pl.BlockSpec((H, 1, tm), lambda i, j: (0, 0, i)),
                  pl.BlockSpec((H, 1, tm), lambda i, j: (0, 0, i)),
                  pl.BlockSpec((tm, D), lambda i, j: (i, 0))],
        out_specs=pl.BlockSpec((tm, D), lambda i, j: (i, 0)),
        out_shape=jax.ShapeDtypeStruct((T, D), F32),
        scratch_shapes=[pltpu.VMEM((H, K, tm), F32),
                        pltpu.VMEM((H, K, tm), F32),
                        pltpu.VMEM((H, 8, tm), F32),
                        pltpu.VMEM((H, te // K, 2, 8, tm + LANE), F32),
                        pltpu.VMEM((te, tm), F32),
                        pltpu.VMEM((te // K, K, tm), F32),
                        pltpu.VMEM((D, tm), F32)],
        compiler_params=_params("parallel", "arbitrary"),
        name="peer_dense",
    )(h2, u, vt, s1, s2, tau, cc, x1)


def _rope_tables(S):
    inv = 1.0 / (ROPE_THETA ** (jnp.arange(0, LANE, 2, dtype=F32) / LANE))
    ang = jnp.arange(S, dtype=F32)[:, None] * inv[None, :]
    cos, sin = jnp.cos(ang), jnp.sin(ang)
    return jnp.concatenate([cos, cos], axis=1), jnp.concatenate([-sin, sin], axis=1)


def _layer(x, mem, attn_norm_g, mem_norm_g, ffn_norm_g, w_in, moba_q_norm_g, moba_k_norm_g,
           dn_conv_w, dn_a_log, dn_dt_bias, dn_out_norm_g, w_mem_kv, mem_q_norm_g, mem_k_norm_g,
           w_branch, w_out, peer_w_q, peer_keys1, peer_keys2, peer_u, peer_v):
    B, S, D = x.shape
    T = B * S
    x2 = x.reshape(T, D)
    row = lambda g: g.reshape(1, -1)

    moba_w, dn_w = 3 * MOBA_HEADS * LANE, 3 * DN_HEADS * LANE
    n_small = 2 * DN_HEADS
    o_small = moba_w + dn_w
    qk_w = 2 * MOBA_HEADS * LANE
    o_z = o_small + n_small
    w_f32 = jnp.concatenate([w_in[:, :qk_w], w_in[:, moba_w:o_small], w_in[:, o_z + D:o_z + 2 * D]],
                            axis=1).astype(BF16)
    w_b16 = jnp.concatenate([w_in[:, qk_w:moba_w], w_in[:, o_z:o_z + D], w_in[:, o_z + 2 * D:]],
                            axis=1).astype(BF16)
    w_small = jnp.pad(w_in[:, o_small:o_z], ((0, 0), (0, LANE - n_small)))
    proj = _norm_matmul(x2, row(attn_norm_g), w_f32, tm=1024, tn=512, name="in_proj")
    projb = _norm_matmul(x2, row(attn_norm_g), w_b16, tm=1024, tn=512, out_dtype=BF16, name="in_proj_bf16")
    small = _norm_matmul(x2, row(attn_norm_g), w_small, tm=1024, tn=LANE, precision=HI, name="in_proj_scalars")
    proj3 = proj.reshape(B, S, -1)
    projb3 = projb.reshape(B, S, -1)
    dn_col = qk_w // LANE
    memq_col = (qk_w + dn_w) // MEM_HEAD_DIM
    z_col = 1
    gate_col = 2

    cos2, sin2 = _rope_tables(S)
    mq, mk, kmean = _moba_prep(proj3, row(moba_q_norm_g), row(moba_k_norm_g), cos2, sin2)
    moba_out = _moba_attn(_moba_gate(mq, kmean), mk, projb3)

    dn_qkv = _dn_prep(proj3, dn_conv_w, dn_col)
    braw = small[:, :DN_HEADS].reshape(B, S, DN_HEADS)
    araw = small[:, DN_HEADS:n_small].reshape(B, S, DN_HEADS)
    araw_t = araw.reshape(B, S // DN_CHUNK, DN_CHUNK, DN_HEADS).transpose(0, 1, 3, 2)
    dn_out = _dn_scan(dn_qkv, braw, araw, araw_t, dn_a_log, dn_dt_bias, projb3, z_col, row(dn_out_norm_g))

    M = mem.shape[1]
    kv = _norm_matmul(mem.reshape(B * M, D), row(mem_norm_g), w_mem_kv.astype(BF16), tm=B * M, tn=512, name="mem_kv")
    mem_out = _mem_attn(proj3, kv.reshape(B, M, -1), row(mem_q_norm_g), row(mem_k_norm_g), memq_col)

    x1 = _merge(x2, moba_out.reshape(T, D), dn_out.reshape(T, D), mem_out.reshape(T, D), projb, gate_col,
                w_branch.astype(BF16), w_out.astype(BF16))

    qp, h2 = _norm_matmul(x1, row(ffn_norm_g), peer_w_q.astype(BF16), tm=1024, tn=512, emit_h=True, name="peer_query")
    s1, s2, tau, cc = _peer_select(qp, peer_keys1, peer_keys2)
    out = _peer_dense(h2, peer_u.astype(BF16), peer_v.T.astype(BF16), s1, s2, tau, cc, x1)
    return out.reshape(B, S, D)


def kernel(x, mem, attn_norm_g, mem_norm_g, ffn_norm_g, w_in, moba_q_norm_g, moba_k_norm_g, dn_conv_w, dn_a_log, dn_dt_bias, dn_out_norm_g, w_mem_kv, mem_q_norm_g, mem_k_norm_g, w_branch, w_out, peer_w_q, peer_keys1, peer_keys2, peer_u, peer_v):
    for l in range(w_in.shape[0]):
        x = _layer(x, mem, attn_norm_g[l], mem_norm_g[l], ffn_norm_g[l], w_in[l], moba_q_norm_g[l],
                   moba_k_norm_g[l], dn_conv_w[l], dn_a_log[l], dn_dt_bias[l], dn_out_norm_g[l],
                   w_mem_kv[l], mem_q_norm_g[l], mem_k_norm_g[l], w_branch[l], w_out[l], peer_w_q[l],
                   peer_keys1[l], peer_keys2[l], peer_u[l], peer_v[l])
    return x
```

```python
import functools
import math

import jax
import jax.numpy as jnp
from jax import lax
from jax.experimental import pallas as pl
from jax.experimental.pallas import tpu as pltpu

F32 = jnp.float32
BF16 = jnp.bfloat16
HI = lax.Precision.HIGHEST
EPS = 1e-6
ROPE_THETA = 10000.0
NT = (((1,), (1,)), ((), ()))
TN = (((0,), (0,)), ((), ()))

LANE = 128
MOBA_HEADS = 8
MOBA_BLOCK = 256
MOBA_TOPK = 3
DN_HEADS = 8
DN_CHUNK = 64
DN_CONV = 4
MEM_HEADS = 4
MEM_HEAD_DIM = 256
PEER_HEADS = 8
PEER_KEYS = 128
PEER_TOPK = 16

VMEM_LIMIT = 56 * 1024 * 1024


def _params(*sem):
    return pltpu.CompilerParams(dimension_semantics=sem, vmem_limit_bytes=VMEM_LIMIT)


def _rms(x, g):
    ms = jnp.mean(x * x, axis=-1, keepdims=True)
    return x * lax.rsqrt(ms + EPS) * g


def _norm_matmul_kernel(x_ref, g_ref, w_ref, o_ref, *rest, precision, emit_h):
    if emit_h:
        ho_ref, h_ref = rest
    else:
        (h_ref,) = rest

    @pl.when(pl.program_id(1) == 0)
    def _():
        h = _rms(x_ref[...], g_ref[...])
        h_ref[...] = h.astype(h_ref.dtype)
        if emit_h:
            ho_ref[...] = h.astype(ho_ref.dtype)

    o_ref[...] = jnp.dot(h_ref[...], w_ref[...], precision=precision,
                         preferred_element_type=F32).astype(o_ref.dtype)


def _norm_matmul(x, g, w, *, tm, tn, name, out_dtype=F32, precision=None, emit_h=False):
    T, D = x.shape
    N = w.shape[1]
    out_shape = [jax.ShapeDtypeStruct((T, N), out_dtype)]
    out_specs = [pl.BlockSpec((tm, tn), lambda i, j: (i, j))]
    if emit_h:
        out_shape.append(jax.ShapeDtypeStruct((T, D), BF16))
        out_specs.append(pl.BlockSpec((tm, D), lambda i, j: (i, 0)))
    res = pl.pallas_call(
        functools.partial(_norm_matmul_kernel, precision=precision, emit_h=emit_h),
        grid=(T // tm, N // tn),
        in_specs=[pl.BlockSpec((tm, D), lambda i, j: (i, 0)),
                  pl.BlockSpec((1, D), lambda i, j: (0, 0)),
                  pl.BlockSpec((D, tn), lambda i, j: (0, j))],
        out_specs=out_specs,
        out_shape=out_shape,
        scratch_shapes=[pltpu.VMEM((tm, D), w.dtype)],
        compiler_params=_params("parallel", "arbitrary"),
        name=name,
    )(x, g, w)
    return res if emit_h else res[0]


MOBA_NEG = -1e30


def _moba_prep_kernel(q_ref, k_ref, gq_ref, gk_ref, cos_ref, sin_ref, qo_ref, ko_ref, km_ref):
    cos = cos_ref[...]
    sin = sin_ref[...]

    def norm_rope(x, g):
        y = _rms(x, g)
        return y * cos + pltpu.roll(y, LANE // 2, 1) * sin

    q = norm_rope(q_ref[0], gq_ref[...])
    k = norm_rope(k_ref[0], gk_ref[...])
    ts = k.shape[0]
    nb = ts // MOBA_BLOCK
    qo_ref[0] = q
    row = lax.broadcasted_iota(jnp.int32, (ts, LANE), 0) + pl.program_id(2) * ts
    lane = lax.broadcasted_iota(jnp.int32, (ts, LANE), 1)
    ko_ref[0, :, 0:LANE] = k.astype(BF16)
    ko_ref[0, :, LANE:2 * LANE] = jnp.where(lane == row // MOBA_BLOCK, 1.0, 0.0).astype(BF16)
    km_ref[0, 0] = jnp.mean(k.reshape(nb, MOBA_BLOCK, LANE), axis=1)


def _moba_prep(proj3, gq, gk, cos2, sin2):
    B, S, _ = proj3.shape
    H = MOBA_HEADS
    ts = 2048
    nb_t = ts // MOBA_BLOCK
    col = lambda off: pl.BlockSpec((1, ts, LANE), lambda b, h, s: (b, s, off + h))
    return pl.pallas_call(
        _moba_prep_kernel,
        grid=(B, H, S // ts),
        in_specs=[col(0), col(H),
                  pl.BlockSpec((1, LANE), lambda b, h, s: (0, 0)),
                  pl.BlockSpec((1, LANE), lambda b, h, s: (0, 0)),
                  pl.BlockSpec((ts, LANE), lambda b, h, s: (s, 0)),
                  pl.BlockSpec((ts, LANE), lambda b, h, s: (s, 0))],
        out_specs=[col(0),
                   pl.BlockSpec((1, ts, 2 * LANE), lambda b, h, s: (b, s, h)),
                   pl.BlockSpec((1, 1, nb_t, LANE), lambda b, h, s: (b, h, s, 0))],
        out_shape=[jax.ShapeDtypeStruct((B, S, H * LANE), F32),
                   jax.ShapeDtypeStruct((B, S, H * 2 * LANE), BF16),
                   jax.ShapeDtypeStruct((B, H, S // MOBA_BLOCK, LANE), F32)],
        compiler_params=_params("parallel", "parallel", "parallel"),
        name="moba_prep",
    )(proj3, proj3, gq, gk, cos2, sin2)


def _moba_gate_kernel(q_ref, km_ref, o_ref):
    L = MOBA_BLOCK
    q = q_ref[0]
    km = km_ref[0, 0]
    nb = km.shape[0]
    S = q.shape[0]
    gate = lax.dot_general(km, q, NT, precision=HI, preferred_element_type=F32)
    row = lax.broadcasted_iota(jnp.int32, (nb, S), 0)
    own = lax.broadcasted_iota(jnp.int32, (nb, S), 1) // L
    rank = jnp.zeros((nb, S), jnp.int32)
    for m in range(nb - 1):
        gm = gate[m:m + 1, :]
        cnt = jnp.where(row > m, jnp.where(gm >= gate, 1, 0), jnp.where(gm > gate, 1, 0))
        rank = rank + jnp.where(own > m, cnt, 0)
    keep = jnp.where(row < own, jnp.where(rank < MOBA_TOPK, 1.0, 0.0), jnp.where(row == own, 1.0, 0.0))
    eye = (lax.broadcasted_iota(jnp.int32, (nb, LANE), 0)
           == lax.broadcasted_iota(jnp.int32, (nb, LANE), 1)).astype(BF16)
    keep_t = lax.dot_general(keep.astype(BF16), eye, TN, preferred_element_type=F32)
    o_ref[0, :, 0:LANE] = (q * (LANE ** -0.5 * math.log2(math.e))).astype(BF16)
    o_ref[0, :, LANE:2 * LANE] = jnp.where(keep_t > 0.5, 0.0, MOBA_NEG).astype(BF16)


def _moba_gate(q, kmean):
    B, S, W = q.shape
    H = MOBA_HEADS
    nb = kmean.shape[2]
    return pl.pallas_call(
        _moba_gate_kernel,
        grid=(B, H),
        in_specs=[pl.BlockSpec((1, S, LANE), lambda b, h: (b, 0, h)),
                  pl.BlockSpec((1, 1, nb, LANE), lambda b, h: (b, h, 0, 0))],
        out_specs=pl.BlockSpec((1, S, 2 * LANE), lambda b, h: (b, 0, h)),
        out_shape=jax.ShapeDtypeStruct((B, S, 2 * W), BF16),
        compiler_params=_params("parallel", "parallel"),
        name="moba_gate",
    )(q, kmean)


def _moba_attn_kernel(qa_ref, qb_ref, k_ref, v_ref, oa_ref, ob_ref, qaug_ref, s_ref):
    L = MOBA_BLOCK
    nb = k_ref.shape[1] // L
    half = nb // 2
    p = pl.program_id(2)
    tiles = (p, nb - 1 - p)
    qaug_ref[0] = qa_ref[0]
    qaug_ref[1] = qb_ref[0]

    def scores(qa, n):
        kb = k_ref[0, pl.ds(pl.multiple_of(n * L, L), L), :]
        return lax.dot_general(qa, kb, NT, preferred_element_type=F32)

    def half_max(s):
        return jnp.maximum(s[:, :LANE], s[:, LANE:])

    def dyn_slot(j):
        first = j < p
        return first, jnp.where(first, 0, 1), jnp.where(first, j, half + j - p)

    r = lax.broadcasted_iota(jnp.int32, (L, L), 0)
    c = lax.broadcasted_iota(jnp.int32, (L, L), 1)
    causal = jnp.where(c <= r, 0.0, MOBA_NEG)

    m = [None, None]
    for t in range(2):
        s = scores(qaug_ref[t], tiles[t]) + causal
        s_ref[nb - 1 + t] = s
        m[t] = half_max(s)
    for n in range(half):
        s = scores(qaug_ref[1], n)
        s_ref[n] = s
        m[1] = jnp.maximum(m[1], half_max(s))
    for j in range(half - 1):
        first, t, n = dyn_slot(j)
        s = scores(qaug_ref[t], n)
        s_ref[half + j] = s
        hm = half_max(s)
        m[0] = jnp.maximum(m[0], jnp.where(first, hm, MOBA_NEG))
        m[1] = jnp.maximum(m[1], jnp.where(first, MOBA_NEG, hm))
    mb = [jnp.broadcast_to(jnp.max(m[t], axis=-1, keepdims=True), (L, LANE)) for t in range(2)]

    def probs(k, mbt):
        s = s_ref[k]
        return jnp.concatenate([jnp.exp2(s[:, :LANE] - mbt), jnp.exp2(s[:, LANE:] - mbt)], axis=1)

    def value(n):
        return v_ref[0, pl.ds(pl.multiple_of(n * L, L), L), :]

    l = [None, None]
    acc = [None, None]
    for t in range(2):
        pr = probs(nb - 1 + t, mb[t])
        l[t] = pr[:, :LANE] + pr[:, LANE:]
        acc[t] = jnp.dot(pr.astype(BF16), value(tiles[t]), preferred_element_type=F32)
    for n in range(half):
        pr = probs(n, mb[1])
        l[1] = l[1] + (pr[:, :LANE] + pr[:, LANE:])
        acc[1] = acc[1] + jnp.dot(pr.astype(BF16), value(n), preferred_element_type=F32)
    for j in range(half - 1):
        first, t, n = dyn_slot(j)
        pr = probs(half + j, jnp.where(first, mb[0], mb[1]))
        ps = pr[:, :LANE] + pr[:, LANE:]
        d = jnp.dot(pr.astype(BF16), value(n), preferred_element_type=F32)
        l[0] = l[0] + jnp.where(first, ps, 0.0)
        l[1] = l[1] + jnp.where(first, 0.0, ps)
        acc[0] = acc[0] + jnp.where(first, d, 0.0)
        acc[1] = acc[1] + jnp.where(first, 0.0, d)

    for t, o_ref in enumerate((oa_ref, ob_ref)):
        o_ref[0] = (acc[t] / jnp.sum(l[t], axis=-1, keepdims=True)).astype(o_ref.dtype)


def _moba_attn(q_aug, k_aug, v):
    B, S, _ = v.shape
    H = MOBA_HEADS
    W = H * LANE
    L = MOBA_BLOCK
    nb = S // L
    half = nb // 2
    lo, hi = pl.pallas_call(
        _moba_attn_kernel,
        grid=(B, H, half),
        in_specs=[pl.BlockSpec((1, L, 2 * LANE), lambda b, h, p: (b, p, h)),
                  pl.BlockSpec((1, L, 2 * LANE), lambda b, h, p: (b, nb - 1 - p, h)),
                  pl.BlockSpec((1, S, 2 * LANE), lambda b, h, p: (b, 0, h)),
                  pl.BlockSpec((1, S, LANE), lambda b, h, p: (b, 0, h))],
        out_specs=[pl.BlockSpec((1, L, LANE), lambda b, h, p: (b, p, h)),
                   pl.BlockSpec((1, L, LANE), lambda b, h, p: (b, half - 1 - p, h))],
        out_shape=[jax.ShapeDtypeStruct((B, S // 2, W), BF16), jax.ShapeDtypeStruct((B, S // 2, W), BF16)],
        scratch_shapes=[pltpu.VMEM((2, L, 2 * LANE), BF16),
                        pltpu.VMEM((nb + 1, L, L), F32)],
        compiler_params=_params("parallel", "parallel", "arbitrary"),
        name="moba_attn",
    )(q_aug, q_aug, k_aug, v)
    return jnp.concatenate([lo, hi], axis=1)


def _mem_attn_kernel(q_ref, k_ref, v_ref, gq_ref, gk_ref, o_ref):
    cq = _rms(q_ref[0], gq_ref[...]).astype(BF16)
    ck = _rms(k_ref[0], gk_ref[...]).astype(BF16)
    s = lax.dot_general(cq, ck, NT, preferred_element_type=F32) * (MEM_HEAD_DIM ** -0.5)
    m = jnp.max(s, axis=-1, keepdims=True)
    p = jnp.exp(s - m)
    l = jnp.sum(p, axis=-1, keepdims=True)
    o = jnp.dot(p.astype(BF16), v_ref[0].astype(BF16), preferred_element_type=F32) / l
    o_ref[0] = o.astype(o_ref.dtype)


def _mem_attn(proj3, kv3, gq, gk, q_col):
    B, S, _ = proj3.shape
    M = kv3.shape[1]
    hd = MEM_HEAD_DIM
    tq = 1024
    return pl.pallas_call(
        _mem_attn_kernel,
        grid=(B, MEM_HEADS, S // tq),
        in_specs=[pl.BlockSpec((1, tq, hd), lambda b, h, i: (b, i, q_col + h)),
                  pl.BlockSpec((1, M, hd), lambda b, h, i: (b, 0, h)),
                  pl.BlockSpec((1, M, hd), lambda b, h, i: (b, 0, MEM_HEADS + h)),
                  pl.BlockSpec((1, hd), lambda b, h, i: (0, 0)),
                  pl.BlockSpec((1, hd), lambda b, h, i: (0, 0))],
        out_specs=pl.BlockSpec((1, tq, hd), lambda b, h, i: (b, i, h)),
        out_shape=jax.ShapeDtypeStruct((B, S, MEM_HEADS * hd), BF16),
        compiler_params=_params("parallel", "parallel", "parallel"),
        name="mem_attn",
    )(proj3, kv3, kv3, gq, gk)


def _dn_prep_kernel(x_ref, w_ref, o_ref, pad_ref):
    S = x_ref.shape[1]
    cb = pl.program_id(1)
    x = x_ref[0]
    pad_ref[0:8, :] = jnp.zeros((8, LANE), F32)
    pad_ref[8:, :] = x
    w = w_ref[...]
    y = w[DN_CONV - 1:DN_CONV, :] * x
    for j in range(DN_CONV - 1):
        off = 8 - (DN_CONV - 1) + j
        y = y + w[j:j + 1, :] * pad_ref[off:off + S, :]
    y = y * jax.nn.sigmoid(y)
    nrm = lax.rsqrt(jnp.sum(y * y, axis=-1, keepdims=True) + EPS)
    scale = jnp.where(cb < DN_HEADS, nrm * (LANE ** -0.5), jnp.where(cb < 2 * DN_HEADS, nrm, 1.0))
    o_ref[0] = y * scale


def _dn_prep(proj3, conv_w, col0):
    B, S, _ = proj3.shape
    ncb = 3 * DN_HEADS
    return pl.pallas_call(
        _dn_prep_kernel,
        grid=(B, ncb),
        in_specs=[pl.BlockSpec((1, S, LANE), lambda b, c: (b, 0, col0 + c)),
                  pl.BlockSpec((DN_CONV, LANE), lambda b, c: (0, c))],
        out_specs=pl.BlockSpec((1, S, LANE), lambda b, c: (b, 0, c)),
        out_shape=jax.ShapeDtypeStruct((B, S, ncb * LANE), F32),
        scratch_shapes=[pltpu.VMEM((S + 8, LANE), F32)],
        compiler_params=_params("parallel", "parallel"),
        name="dn_prep",
    )(proj3, conv_w)


def _softplus(x):
    return jnp.maximum(x, 0.0) + jnp.log1p(jnp.exp(-jnp.abs(x)))


def _split_bf16(a):
    hi = a.astype(BF16)
    return hi, (a - hi.astype(F32)).astype(BF16)


def _mm_split3(a, b):
    ah, al = _split_bf16(a)
    bh, bl = _split_bf16(b)
    return jnp.dot(jnp.concatenate([ah, ah, al], axis=1), jnp.concatenate([bh, bl, bh], axis=0),
                   preferred_element_type=F32)


def _dn_scan_kernel(q_ref, k_ref, v_ref, b_ref, a_ref, at_ref, alr_ref, dtr_ref, alc_ref, dtc_ref,
                    z_ref, g_ref, o_ref, st_ref):
    C = DN_CHUNK
    H = DN_HEADS
    R = q_ref.shape[0]

    @pl.when(pl.program_id(1) == 0)
    def _():
        st_ref[...] = jnp.zeros_like(st_ref)

    ii = lax.broadcasted_iota(jnp.int32, (C, C), 0)
    jj = lax.broadcasted_iota(jnp.int32, (C, C), 1)
    incl = ii >= jj
    strict = ii > jj
    tril = incl.astype(F32)
    triu = (ii <= jj).astype(F32)
    gn = g_ref[...]

    gc_col, gc_row, beta = [], [], []
    for r in range(R):
        g_col = -jnp.exp(alr_ref[...]) * _softplus(a_ref[r] + dtr_ref[...])
        g_row = -jnp.exp(alc_ref[...]) * _softplus(at_ref[r, 0] + dtc_ref[...])
        gc_col.append(jnp.dot(tril, g_col, precision=HI, preferred_element_type=F32))
        gc_row.append(jnp.dot(g_row, triu, precision=HI, preferred_element_type=F32))
        beta.append(jax.nn.sigmoid(b_ref[r]))

    def mm(a, b):
        return jnp.dot(a.astype(BF16), b.astype(BF16), preferred_element_type=F32)

    units = [(r, h) for r in range(R) for h in range(H)]
    U = range(len(units))
    sl = [slice(h * LANE, (h + 1) * LANE) for _, h in units]
    q = [q_ref[r, :, sl[u]] for u, (r, h) in enumerate(units)]
    k = [k_ref[r, :, sl[u]] for u, (r, h) in enumerate(units)]
    gcc = [gc_col[r][:, h:h + 1] for r, h in units]
    bet = [beta[r][:, h:h + 1] for r, h in units]
    decay = [jnp.where(incl, jnp.exp(jnp.where(incl, gcc[u] - gc_row[r][h:h + 1, :], 0.0)), 0.0)
             for u, (r, h) in enumerate(units)]
    kb = [k[u] * bet[u] for u in U]
    vb = [v_ref[r, :, sl[u]] * bet[u] for u, (r, h) in enumerate(units)]
    kbf = [k[u].astype(BF16) for u in U]
    kk = [lax.dot_general(kb[u].astype(BF16), kbf[u], NT, preferred_element_type=F32) for u in U]
    qk = [lax.dot_general(q[u].astype(BF16), kbf[u], NT, preferred_element_type=F32) for u in U]
    n = [jnp.where(strict, kk[u] * decay[u], 0.0) for u in U]
    attn = [jnp.where(incl, qk[u] * decay[u], 0.0) for u in U]
    egc = [jnp.exp(gcc[u]) for u in U]
    x = [jnp.concatenate([vb[u], kb[u] * egc[u]], axis=1) for u in U]
    nx = [_mm_split3(n[u], x[u]) for u in U]
    p = [_mm_split3(n[u], n[u]) for u in U]
    x = [x[u] - nx[u] for u in U]
    levels = int(math.log2(C)) - 1
    for lvl in range(levels):
        px = [_mm_split3(p[u], x[u]) for u in U]
        if lvl + 1 < levels:
            p = [_mm_split3(p[u], p[u]) for u in U]
        x = [x[u] + px[u] for u in U]
    s = [st_ref[u] for u in U]
    sb = [s[u].astype(BF16) for u in U]
    ws = [mm(x[u][:, LANE:], sb[u]) for u in U]
    qs = [mm(q[u] * egc[u], sb[u]) for u in U]
    v_new = [(x[u][:, :LANE] - ws[u]).astype(BF16) for u in U]
    av = [mm(attn[u], v_new[u]) for u in U]
    g_last = [gcc[u][C - 1:C, :] for u in U]
    kd = [(k[u] * jnp.exp(g_last[u] - gcc[u])).astype(BF16) for u in U]
    kv = [lax.dot_general(kd[u], v_new[u], TN, preferred_element_type=F32) for u in U]
    for u, (r, h) in enumerate(units):
        st_ref[u] = s[u] * jnp.exp(g_last[u]) + kv[u]
        zz = z_ref[r, :, sl[u]].astype(F32)
        o_ref[r, :, sl[u]] = (_rms(qs[u] + av[u], gn) * (zz * jax.nn.sigmoid(zz))).astype(o_ref.dtype)


DN_ROWS = 2


def _dn_scan(qkv, braw, araw, araw_t, a_log, dt_bias, proj3, z_col, onorm_g):
    B, S, _ = qkv.shape
    H = DN_HEADS
    C = DN_CHUNK
    W = H * LANE
    R = DN_ROWS
    tile = lambda col: pl.BlockSpec((R, C, W), lambda b, c: (b, c, col))
    small = lambda shape: pl.BlockSpec(shape, lambda b, c: (0,) * len(shape))
    return pl.pallas_call(
        _dn_scan_kernel,
        grid=(B // R, S // C),
        in_specs=[tile(0), tile(1), tile(2),
                  pl.BlockSpec((R, C, H), lambda b, c: (b, c, 0)),
                  pl.BlockSpec((R, C, H), lambda b, c: (b, c, 0)),
                  pl.BlockSpec((R, 1, H, C), lambda b, c: (b, c, 0, 0)),
                  small((1, H)), small((1, H)), small((H, 1)), small((H, 1)),
                  tile(z_col), small((1, LANE))],
        out_specs=tile(0),
        out_shape=jax.ShapeDtypeStruct((B, S, W), BF16),
        scratch_shapes=[pltpu.VMEM((R * H, LANE, LANE), F32)],
        compiler_params=_params("parallel", "arbitrary"),
        name="dn_scan",
    )(qkv, qkv, qkv, braw, araw, araw_t, a_log.reshape(1, H), dt_bias.reshape(1, H),
      a_log.reshape(H, 1), dt_bias.reshape(H, 1), proj3, onorm_g)


def _merge_kernel(x_ref, b0_ref, b1_ref, b2_ref, g0_ref, g1_ref, g2_ref, wb_ref, wo_ref, o_ref):
    mixed = None
    for i, (b_ref, g_ref) in enumerate(((b0_ref, g0_ref), (b1_ref, g1_ref), (b2_ref, g2_ref))):
        bp = jnp.dot(b_ref[...], wb_ref[i], preferred_element_type=F32)
        t = jax.nn.sigmoid(g_ref[...].astype(F32)) * bp
        mixed = t if mixed is None else mixed + t
    o_ref[...] = x_ref[...] + jnp.dot(mixed.astype(BF16), wo_ref[...], preferred_element_type=F32)


def _merge(x2, moba_out, dn_out, mem_out, proj, gate_col, w_branch, w_out):
    T, D = x2.shape
    tm = 256
    row = lambda col: pl.BlockSpec((tm, D), lambda i: (i, col))
    return pl.pallas_call(
        _merge_kernel,
        grid=(T // tm,),
        in_specs=[row(0), row(0), row(0), row(0), row(gate_col), row(gate_col + 1), row(gate_col + 2),
                  pl.BlockSpec((3, D, D), lambda i: (0, 0, 0)),
                  pl.BlockSpec((D, D), lambda i: (0, 0))],
        out_specs=row(0),
        out_shape=jax.ShapeDtypeStruct((T, D), F32),
        compiler_params=_params("parallel"),
        name="merge",
    )(x2, moba_out, dn_out, mem_out, proj, proj, proj, w_branch, w_out)


def _compare_exchange(a, b):
    if a is None:
        return b, None
    if b is None:
        return a, None
    return jnp.maximum(a, b), jnp.minimum(a, b)


def _bitonic_merge_desc(xs):
    n = len(xs)
    j = n // 2
    while j >= 1:
        for i in range(n):
            l = i ^ j
            if l > i:
                xs[i], xs[l] = _compare_exchange(xs[i], xs[l])
        j //= 2
    return xs


def _bitonic_sort_desc(xs):
    xs = list(xs)
    n = len(xs)
    k = 2
    while k <= n:
        j = k // 2
        while j >= 1:
            for i in range(n):
                l = i ^ j
                if l > i:
                    hi, lo = _compare_exchange(xs[i], xs[l])
                    xs[i], xs[l] = (hi, lo) if (i & k) == 0 else (lo, hi)
            j //= 2
        k *= 2
    return xs


def _top16_over_rows(pieces):
    K = PEER_TOPK
    xs = _bitonic_sort_desc(list(pieces) + [None] * (K - len(pieces)))
    for shift in (4, 2, 1):
        other = [None if x is None else pltpu.roll(x, shift, 0) for x in xs]
        merged = []
        for i in range(K):
            a, b = xs[i], other[K - 1 - i]
            merged.append(b if a is None else a if b is None else jnp.maximum(a, b))
        xs = _bitonic_merge_desc(merged)
    return xs


def _rows_from_replicated(vals, sub):
    out = vals[0]
    for r in range(1, 8):
        out = jnp.where(sub == r, vals[r], out)
    return out


def _peer_select_kernel(qp_ref, k1_ref, k2_ref, s1_ref, s2_ref, tau_ref, cc_ref):
    tm = qp_ref.shape[0]
    K = PEER_KEYS
    q = qp_ref[...]
    s1_ref[0] = lax.dot_general(k1_ref[...], q[:, :K], NT, precision=HI, preferred_element_type=F32)
    s2_ref[0] = lax.dot_general(k2_ref[...], q[:, K:], NT, precision=HI, preferred_element_type=F32)
    sub = lax.broadcasted_iota(jnp.int32, (8, LANE), 0)
    ninf = -jnp.inf
    for c in range(tm // LANE):
        cs = slice(c * LANE, (c + 1) * LANE)
        v1 = _top16_over_rows([s1_ref[0, 8 * r:8 * r + 8, cs] for r in range(K // 8)])
        v2 = _top16_over_rows([s2_ref[0, 8 * r:8 * r + 8, cs] for r in range(K // 8)])
        v1lo, v1hi = _rows_from_replicated(v1[:8], sub), _rows_from_replicated(v1[8:], sub)
        v2lo, v2hi = _rows_from_replicated(v2[:8], sub), _rows_from_replicated(v2[8:], sub)
        cands = [
            v1[0] + v2lo, v1[0] + v2hi,
            v1[1] + v2lo,
            jnp.where(sub >= 2, v2[0] + v1lo, ninf), v2[0] + v1hi,
            jnp.where(sub >= 2, v2[1] + v1lo, ninf),
            jnp.where((sub >= 2) & (sub <= 4), v1[2] + v2lo, ninf),
            jnp.where((sub >= 2) & (sub <= 3), v1[3] + v2lo, ninf),
            jnp.where(sub == 2, v1[4] + v2lo, ninf),
        ]
        top = _top16_over_rows(cands)
        smax = top[0]
        z = jnp.exp(top[0] - smax)
        for t in top[1:]:
            z = z + jnp.exp(t - smax)
        tau_ref[0, :, cs] = top[PEER_TOPK - 1][0:1, :]
        cc_ref[0, :, cs] = (smax + jnp.log(z))[0:1, :]


def _peer_select(qp, keys1, keys2):
    T = qp.shape[0]
    H, K = PEER_HEADS, PEER_KEYS
    tm = 512
    return pl.pallas_call(
        _peer_select_kernel,
        grid=(T // tm, H),
        in_specs=[pl.BlockSpec((tm, 2 * K), lambda i, h: (i, h)),
                  pl.BlockSpec((K, K), lambda i, h: (0, 0)),
                  pl.BlockSpec((K, K), lambda i, h: (0, 0))],
        out_specs=[pl.BlockSpec((1, K, tm), lambda i, h: (h, 0, i)),
                   pl.BlockSpec((1, K, tm), lambda i, h: (h, 0, i)),
                   pl.BlockSpec((1, 1, tm), lambda i, h: (h, 0, i)),
                   pl.BlockSpec((1, 1, tm), lambda i, h: (h, 0, i))],
        out_shape=[jax.ShapeDtypeStruct((H, K, T), F32), jax.ShapeDtypeStruct((H, K, T), F32),
                   jax.ShapeDtypeStruct((H, 1, T), F32), jax.ShapeDtypeStruct((H, 1, T), F32)],
        compiler_params=_params("parallel", "parallel"),
        name="peer_select",
    )(qp, keys1, keys2)


GELU_C0 = math.sqrt(2.0 / math.pi)
GELU_C1 = GELU_C0 * 0.044715
PEER_ROWS = 256
PEER_LANES = 256


def _peer_dense_kernel(h_ref, u_ref, vt_ref, s1_ref, s2_ref, tau_ref, cc_ref, x_ref, o_ref,
                       e1_ref, e2_ref, taub_ref, rowb_ref, g_ref, w_ref, acc_ref):
    K = PEER_KEYS
    H = PEER_HEADS
    SUB = 8
    j = pl.program_id(1)
    te, tm = g_ref.shape
    n_a = te // K

    @pl.when(j == 0)
    def _():
        acc_ref[...] = jnp.zeros_like(acc_ref)
        for h in range(H):
            s1 = s1_ref[h]
            m1 = jnp.max(s1, axis=0, keepdims=True)
            e1_ref[h] = jnp.exp(s1 - m1)
            e2_ref[h] = 0.5 * jnp.exp(s2_ref[h] - (cc_ref[h] - m1))
            taub_ref[h] = jnp.broadcast_to(tau_ref[h], (SUB, tm))

    g_ref[...] = lax.dot_general(u_ref[...], h_ref[...], NT, preferred_element_type=F32)

    a0 = pl.multiple_of(j * n_a, n_a)
    for h in range(H):
        s1t = s1_ref[h, pl.ds(a0, n_a), :]
        e1t = e1_ref[h, pl.ds(a0, n_a), :]
        for r in range(n_a):
            rowb_ref[h, r, 0, :, 0:tm] = jnp.broadcast_to(s1t[r:r + 1, :], (SUB, tm))
            rowb_ref[h, r, 1, :, 0:tm] = jnp.broadcast_to(e1t[r:r + 1, :], (SUB, tm))

    def slab(sb, carry):
        b0 = pl.multiple_of(sb * SUB, SUB)
        for l0 in range(0, tm, PEER_LANES):
            ls = slice(l0, l0 + PEER_LANES)
            w = [None] * n_a
            for h in range(H):
                s2s = s2_ref[h, pl.ds(b0, SUB), ls]
                e2s = e2_ref[h, pl.ds(b0, SUB), ls]
                tb = taub_ref[h, :, ls]
                for r in range(n_a):
                    t = jnp.where(s2s + rowb_ref[h, r, 0, :, ls] >= tb, e2s * rowb_ref[h, r, 1, :, ls], 0.0)
                    w[r] = t if w[r] is None else w[r] + t
            for r in range(n_a):
                w_ref[r, pl.ds(b0, SUB), ls] = w[r]
        return carry

    lax.fori_loop(0, K // SUB, slab, 0, unroll=4)

    total = None
    for c in range(te // PEER_ROWS):
        rows = slice(c * PEER_ROWS, (c + 1) * PEER_ROWS)
        x = g_ref[rows, :]
        th = jnp.tanh(x * (GELU_C0 + GELU_C1 * (x * x)))
        w = w_ref[c * (PEER_ROWS // K):(c + 1) * (PEER_ROWS // K)].reshape(PEER_ROWS, tm)
        wa = (w * (x + x * th)).astype(BF16)
        d = jnp.dot(vt_ref[:, rows], wa, preferred_element_type=F32)
        total = d if total is None else total + d
    acc_ref[...] += total

    @pl.when(j == pl.num_programs(1) - 1)
    def _():
        o_ref[...] = x_ref[...] + acc_ref[...].T


def _peer_dense(h2, u, vt, s1, s2, tau, cc, x1):
    T, D = h2.shape
    E = u.shape[0]
    H, K = PEER_HEADS, PEER_KEYS
    tm, te = 512, 1024
    return pl.pallas_call(
        _peer_dense_kernel,
        grid=(T // tm, E // te),
        in_specs=[pl.BlockSpec((tm, D), lambda i, j: (i, 0)),
                  pl.BlockSpec((te, D), lambda i, j: (j, 0)),
                  pl.BlockSpec((D, te), lambda i, j: (0, j)),
                  pl.BlockSpec((H, K, tm), lambda i, j: (0, 0, i)),
                  pl.BlockSpec((H, K, tm), lambda i, j: (0, 0, i)),
                  pl.BlockSpec((H, 1, tm), lambda i, j: (0, 0, i)),
                  pl.BlockSpec((H, 1, tm), lambda i, j: (0, 0, i)),
                  pl.BlockSpec((tm, D), lambda i, j: (i, 0))],
        out_specs=pl.BlockSpec((tm, D), lambda i, j: (i, 0)),
        out_shape=jax.ShapeDtypeStruct((T, D), F32),
        scratch_shapes=[pltpu.VMEM((H, K, tm), F32),
                        pltpu.VMEM((H, K, tm), F32),
                        pltpu.VMEM((H, 8, tm), F32),
                        pltpu.VMEM((H, te // K, 2, 8, tm + LANE), F32),
                        pltpu.VMEM((te, tm), F32),
                        pltpu.VMEM((te // K, K, tm), F32),
                        pltpu.VMEM((D, tm), F32)],
        compiler_params=_params("parallel", "arbitrary"),
        name="peer_dense",
    )(h2, u, vt, s1, s2, tau, cc, x1)


def _rope_tables(S):
    inv = 1.0 / (ROPE_THETA ** (jnp.arange(0, LANE, 2, dtype=F32) / LANE))
    ang = jnp.arange(S, dtype=F32)[:, None] * inv[None, :]
    cos, sin = jnp.cos(ang), jnp.sin(ang)
    return jnp.concatenate([cos, cos], axis=1), jnp.concatenate([-sin, sin], axis=1)


def _layer(x, mem, attn_norm_g, mem_norm_g, ffn_norm_g, w_in, moba_q_norm_g, moba_k_norm_g,
           dn_conv_w, dn_a_log, dn_dt_bias, dn_out_norm_g, w_mem_kv, mem_q_norm_g, mem_k_norm_g,
           w_branch, w_out, peer_w_q, peer_keys1, peer_keys2, peer_u, peer_v):
    B, S, D = x.shape
    T = B * S
    x2 = x.reshape(T, D)
    row = lambda g: g.reshape(1, -1)

    moba_w, dn_w = 3 * MOBA_HEADS * LANE, 3 * DN_HEADS * LANE
    n_small = 2 * DN_HEADS
    o_small = moba_w + dn_w
    qk_w = 2 * MOBA_HEADS * LANE
    o_z = o_small + n_small
    w_f32 = jnp.concatenate([w_in[:, :qk_w], w_in[:, moba_w:o_small], w_in[:, o_z + D:o_z + 2 * D]],
                            axis=1).astype(BF16)
    w_b16 = jnp.concatenate([w_in[:, qk_w:moba_w], w_in[:, o_z:o_z + D], w_in[:, o_z + 2 * D:]],
                            axis=1).astype(BF16)
    w_small = jnp.pad(w_in[:, o_small:o_z], ((0, 0), (0, LANE - n_small)))
    proj = _norm_matmul(x2, row(attn_norm_g), w_f32, tm=1024, tn=1024, name="in_proj")
    projb = _norm_matmul(x2, row(attn_norm_g), w_b16, tm=1024, tn=1024, out_dtype=BF16, name="in_proj_bf16")
    small = _norm_matmul(x2, row(attn_norm_g), w_small, tm=1024, tn=LANE, precision=HI, name="in_proj_scalars")
    proj3 = proj.reshape(B, S, -1)
    projb3 = projb.reshape(B, S, -1)
    dn_col = qk_w // LANE
    memq_col = (qk_w + dn_w) // MEM_HEAD_DIM
    z_col = 1
    gate_col = 2

    cos2, sin2 = _rope_tables(S)
    mq, mk, kmean = _moba_prep(proj3, row(moba_q_norm_g), row(moba_k_norm_g), cos2, sin2)
    moba_out = _moba_attn(_moba_gate(mq, kmean), mk, projb3)

    dn_qkv = _dn_prep(proj3, dn_conv_w, dn_col)
    braw = small[:, :DN_HEADS].reshape(B, S, DN_HEADS)
    araw = small[:, DN_HEADS:n_small].reshape(B, S, DN_HEADS)
    araw_t = araw.reshape(B, S // DN_CHUNK, DN_CHUNK, DN_HEADS).transpose(0, 1, 3, 2)
    dn_out = _dn_scan(dn_qkv, braw, araw, araw_t, dn_a_log, dn_dt_bias, projb3, z_col, row(dn_out_norm_g))

    M = mem.shape[1]
    kv = _norm_matmul(mem.reshape(B * M, D), row(mem_norm_g), w_mem_kv.astype(BF16), tm=B * M, tn=512, name="mem_kv")
    mem_out = _mem_attn(proj3, kv.reshape(B, M, -1), row(mem_q_norm_g), row(mem_k_norm_g), memq_col)

    x1 = _merge(x2, moba_out.reshape(T, D), dn_out.reshape(T, D), mem_out.reshape(T, D), projb, gate_col,
                w_branch.astype(BF16), w_out.astype(BF16))

    qp, h2 = _norm_matmul(x1, row(ffn_norm_g), peer_w_q.astype(BF16), tm=1024, tn=1024, emit_h=True, name="peer_query")
    s1, s2, tau, cc = _peer_select(qp, peer_keys1, peer_keys2)
    out = _peer_dense(h2, peer_u.astype(BF16), peer_v.T.astype(BF16), s1, s2, tau, cc, x1)
    return out.reshape(B, S, D)


def kernel(x, mem, attn_norm_g, mem_norm_g, ffn_norm_g, w_in, moba_q_norm_g, moba_k_norm_g, dn_conv_w, dn_a_log, dn_dt_bias, dn_out_norm_g, w_mem_kv, mem_q_norm_g, mem_k_norm_g, w_branch, w_out, peer_w_q, peer_keys1, peer_keys2, peer_u, peer_v):
    for l in range(w_in.shape[0]):
        x = _layer(x, mem, attn_norm_g[l], mem_norm_g[l], ffn_norm_g[l], w_in[l], moba_q_norm_g[l],
                   moba_k_norm_g[l], dn_conv_w[l], dn_a_log[l], dn_dt_bias[l], dn_out_norm_g[l],
                   w_mem_kv[l], mem_q_norm_g[l], mem_k_norm_g[l], w_branch[l], w_out[l], peer_w_q[l],
                   peer_keys1[l], peer_keys2[l], peer_u[l], peer_v[l])
    return x
```

```python
import functools
import math

import jax
import jax.numpy as jnp
from jax import lax
from jax.experimental import pallas as pl
from jax.experimental.pallas import tpu as pltpu

F32 = jnp.float32
BF16 = jnp.bfloat16
HI = lax.Precision.HIGHEST
EPS = 1e-6
ROPE_THETA = 10000.0
NT = (((1,), (1,)), ((), ()))
TN = (((0,), (0,)), ((), ()))

LANE = 128
MOBA_HEADS = 8
MOBA_BLOCK = 256
MOBA_TOPK = 3
DN_HEADS = 8
DN_CHUNK = 64
DN_CONV = 4
MEM_HEADS = 4
MEM_HEAD_DIM = 256
PEER_HEADS = 8
PEER_KEYS = 128
PEER_TOPK = 16

VMEM_LIMIT = 56 * 1024 * 1024


def _params(*sem):
    return pltpu.CompilerParams(dimension_semantics=sem, vmem_limit_bytes=VMEM_LIMIT)


def _rms(x, g):
    ms = jnp.mean(x * x, axis=-1, keepdims=True)
    return x * lax.rsqrt(ms + EPS) * g


def _norm_matmul_kernel(x_ref, g_ref, w_ref, o_ref, *rest, precision, emit_h):
    if emit_h:
        ho_ref, h_ref = rest
    else:
        (h_ref,) = rest

    @pl.when(pl.program_id(1) == 0)
    def _():
        h = _rms(x_ref[...], g_ref[...])
        h_ref[...] = h.astype(h_ref.dtype)
        if emit_h:
            ho_ref[...] = h.astype(ho_ref.dtype)

    o_ref[...] = jnp.dot(h_ref[...], w_ref[...], precision=precision,
                         preferred_element_type=F32).astype(o_ref.dtype)


def _norm_matmul(x, g, w, *, tm, tn, name, out_dtype=F32, precision=None, emit_h=False):
    T, D = x.shape
    N = w.shape[1]
    out_shape = [jax.ShapeDtypeStruct((T, N), out_dtype)]
    out_specs = [pl.BlockSpec((tm, tn), lambda i, j: (i, j))]
    if emit_h:
        out_shape.append(jax.ShapeDtypeStruct((T, D), BF16))
        out_specs.append(pl.BlockSpec((tm, D), lambda i, j: (i, 0)))
    res = pl.pallas_call(
        functools.partial(_norm_matmul_kernel, precision=precision, emit_h=emit_h),
        grid=(T // tm, N // tn),
        in_specs=[pl.BlockSpec((tm, D), lambda i, j: (i, 0)),
                  pl.BlockSpec((1, D), lambda i, j: (0, 0)),
                  pl.BlockSpec((D, tn), lambda i, j: (0, j))],
        out_specs=out_specs,
        out_shape=out_shape,
        scratch_shapes=[pltpu.VMEM((tm, D), w.dtype)],
        compiler_params=_params("parallel", "arbitrary"),
        name=name,
    )(x, g, w)
    return res if emit_h else res[0]


MOBA_NEG = -1e30


def _moba_prep_kernel(q_ref, k_ref, gq_ref, gk_ref, cos_ref, sin_ref, qo_ref, ko_ref, km_ref):
    cos = cos_ref[...]
    sin = sin_ref[...]

    def norm_rope(x, g):
        y = _rms(x, g)
        return y * cos + pltpu.roll(y, LANE // 2, 1) * sin

    q = norm_rope(q_ref[0], gq_ref[...])
    k = norm_rope(k_ref[0], gk_ref[...])
    ts = k.shape[0]
    nb = ts // MOBA_BLOCK
    qo_ref[0] = q
    row = lax.broadcasted_iota(jnp.int32, (ts, LANE), 0) + pl.program_id(2) * ts
    lane = lax.broadcasted_iota(jnp.int32, (ts, LANE), 1)
    ko_ref[0, :, 0:LANE] = k.astype(BF16)
    ko_ref[0, :, LANE:2 * LANE] = jnp.where(lane == row // MOBA_BLOCK, 1.0, 0.0).astype(BF16)
    km_ref[0, 0] = jnp.mean(k.reshape(nb, MOBA_BLOCK, LANE), axis=1)


def _moba_prep(proj3, gq, gk, cos2, sin2):
    B, S, _ = proj3.shape
    H = MOBA_HEADS
    ts = 2048
    nb_t = ts // MOBA_BLOCK
    col = lambda off: pl.BlockSpec((1, ts, LANE), lambda b, h, s: (b, s, off + h))
    return pl.pallas_call(
        _moba_prep_kernel,
        grid=(B, H, S // ts),
        in_specs=[col(0), col(H),
                  pl.BlockSpec((1, LANE), lambda b, h, s: (0, 0)),
                  pl.BlockSpec((1, LANE), lambda b, h, s: (0, 0)),
                  pl.BlockSpec((ts, LANE), lambda b, h, s: (s, 0)),
                  pl.BlockSpec((ts, LANE), lambda b, h, s: (s, 0))],
        out_specs=[col(0),
                   pl.BlockSpec((1, ts, 2 * LANE), lambda b, h, s: (b, s, h)),
                   pl.BlockSpec((1, 1, nb_t, LANE), lambda b, h, s: (b, h, s, 0))],
        out_shape=[jax.ShapeDtypeStruct((B, S, H * LANE), F32),
                   jax.ShapeDtypeStruct((B, S, H * 2 * LANE), BF16),
                   jax.ShapeDtypeStruct((B, H, S // MOBA_BLOCK, LANE), F32)],
        compiler_params=_params("parallel", "parallel", "parallel"),
        name="moba_prep",
    )(proj3, proj3, gq, gk, cos2, sin2)


def _moba_gate_kernel(q_ref, km_ref, o_ref):
    L = MOBA_BLOCK
    q = q_ref[0]
    km = km_ref[0, 0]
    nb = km.shape[0]
    S = q.shape[0]
    gate = lax.dot_general(km, q, NT, precision=HI, preferred_element_type=F32)
    row = lax.broadcasted_iota(jnp.int32, (nb, S), 0)
    own = lax.broadcasted_iota(jnp.int32, (nb, S), 1) // L
    rank = jnp.zeros((nb, S), jnp.int32)
    for m in range(nb - 1):
        gm = gate[m:m + 1, :]
        cnt = jnp.where(row > m, jnp.where(gm >= gate, 1, 0), jnp.where(gm > gate, 1, 0))
        rank = rank + jnp.where(own > m, cnt, 0)
    keep = jnp.where(row < own, jnp.where(rank < MOBA_TOPK, 1.0, 0.0), jnp.where(row == own, 1.0, 0.0))
    eye = (lax.broadcasted_iota(jnp.int32, (nb, LANE), 0)
           == lax.broadcasted_iota(jnp.int32, (nb, LANE), 1)).astype(BF16)
    keep_t = lax.dot_general(keep.astype(BF16), eye, TN, preferred_element_type=F32)
    o_ref[0, :, 0:LANE] = (q * (LANE ** -0.5 * math.log2(math.e))).astype(BF16)
    o_ref[0, :, LANE:2 * LANE] = jnp.where(keep_t > 0.5, 0.0, MOBA_NEG).astype(BF16)


def _moba_gate(q, kmean):
    B, S, W = q.shape
    H = MOBA_HEADS
    nb = kmean.shape[2]
    return pl.pallas_call(
        _moba_gate_kernel,
        grid=(B, H),
        in_specs=[pl.BlockSpec((1, S, LANE), lambda b, h: (b, 0, h)),
                  pl.BlockSpec((1, 1, nb, LANE), lambda b, h: (b, h, 0, 0))],
        out_specs=pl.BlockSpec((1, S, 2 * LANE), lambda b, h: (b, 0, h)),
        out_shape=jax.ShapeDtypeStruct((B, S, 2 * W), BF16),
        compiler_params=_params("parallel", "parallel"),
        name="moba_gate",
    )(q, kmean)


def _moba_attn_kernel(qa_ref, qb_ref, k_ref, v_ref, oa_ref, ob_ref, qaug_ref, s_ref):
    L = MOBA_BLOCK
    nb = k_ref.shape[1] // L
    half = nb // 2
    p = pl.program_id(2)
    tiles = (p, nb - 1 - p)
    qaug_ref[0] = qa_ref[0]
    qaug_ref[1] = qb_ref[0]

    def scores(qa, n):
        kb = k_ref[0, pl.ds(pl.multiple_of(n * L, L), L), :]
        return lax.dot_general(qa, kb, NT, preferred_element_type=F32)

    def half_max(s):
        return jnp.maximum(s[:, :LANE], s[:, LANE:])

    def dyn_slot(j):
        first = j < p
        return first, jnp.where(first, 0, 1), jnp.where(first, j, half + j - p)

    r = lax.broadcasted_iota(jnp.int32, (L, L), 0)
    c = lax.broadcasted_iota(jnp.int32, (L, L), 1)
    causal = jnp.where(c <= r, 0.0, MOBA_NEG)

    m = [None, None]
    for t in range(2):
        s = scores(qaug_ref[t], tiles[t]) + causal
        s_ref[nb - 1 + t] = s
        m[t] = half_max(s)
    for n in range(half):
        s = scores(qaug_ref[1], n)
        s_ref[n] = s
        m[1] = jnp.maximum(m[1], half_max(s))
    for j in range(half - 1):
        first, t, n = dyn_slot(j)
        s = scores(qaug_ref[t], n)
        s_ref[half + j] = s
        hm = half_max(s)
        m[0] = jnp.maximum(m[0], jnp.where(first, hm, MOBA_NEG))
        m[1] = jnp.maximum(m[1], jnp.where(first, MOBA_NEG, hm))
    mb = [jnp.broadcast_to(jnp.max(m[t], axis=-1, keepdims=True), (L, LANE)) for t in range(2)]

    def probs(k, mbt):
        s = s_ref[k]
        return jnp.concatenate([jnp.exp2(s[:, :LANE] - mbt), jnp.exp2(s[:, LANE:] - mbt)], axis=1)

    def value(n):
        return v_ref[0, pl.ds(pl.multiple_of(n * L, L), L), :]

    l = [None, None]
    acc = [None, None]
    for t in range(2):
        pr = probs(nb - 1 + t, mb[t])
        l[t] = pr[:, :LANE] + pr[:, LANE:]
        acc[t] = jnp.dot(pr.astype(BF16), value(tiles[t]), preferred_element_type=F32)
    for n in range(half):
        pr = probs(n, mb[1])
        l[1] = l[1] + (pr[:, :LANE] + pr[:, LANE:])
        acc[1] = acc[1] + jnp.dot(pr.astype(BF16), value(n), preferred_element_type=F32)
    for j in range(half - 1):
        first, t, n = dyn_slot(j)
        pr = probs(half + j, jnp.where(first, mb[0], mb[1]))
        ps = pr[:, :LANE] + pr[:, LANE:]
        d = jnp.dot(pr.astype(BF16), value(n), preferred_element_type=F32)
        l[0] = l[0] + jnp.where(first, ps, 0.0)
        l[1] = l[1] + jnp.where(first, 0.0, ps)
        acc[0] = acc[0] + jnp.where(first, d, 0.0)
        acc[1] = acc[1] + jnp.where(first, 0.0, d)

    for t, o_ref in enumerate((oa_ref, ob_ref)):
        o_ref[0] = (acc[t] / jnp.sum(l[t], axis=-1, keepdims=True)).astype(o_ref.dtype)


def _moba_attn(q_aug, k_aug, v):
    B, S, _ = v.shape
    H = MOBA_HEADS
    W = H * LANE
    L = MOBA_BLOCK
    nb = S // L
    half = nb // 2
    lo, hi = pl.pallas_call(
        _moba_attn_kernel,
        grid=(B, H, half),
        in_specs=[pl.BlockSpec((1, L, 2 * LANE), lambda b, h, p: (b, p, h)),
                  pl.BlockSpec((1, L, 2 * LANE), lambda b, h, p: (b, nb - 1 - p, h)),
                  pl.BlockSpec((1, S, 2 * LANE), lambda b, h, p: (b, 0, h)),
                  pl.BlockSpec((1, S, LANE), lambda b, h, p: (b, 0, h))],
        out_specs=[pl.BlockSpec((1, L, LANE), lambda b, h, p: (b, p, h)),
                   pl.BlockSpec((1, L, LANE), lambda b, h, p: (b, half - 1 - p, h))],
        out_shape=[jax.ShapeDtypeStruct((B, S // 2, W), BF16), jax.ShapeDtypeStruct((B, S // 2, W), BF16)],
        scratch_shapes=[pltpu.VMEM((2, L, 2 * LANE), BF16),
                        pltpu.VMEM((nb + 1, L, L), F32)],
        compiler_params=_params("parallel", "parallel", "arbitrary"),
        name="moba_attn",
    )(q_aug, q_aug, k_aug, v)
    return jnp.concatenate([lo, hi], axis=1)


def _mem_attn_kernel(q_ref, k_ref, v_ref, gq_ref, gk_ref, o_ref):
    cq = _rms(q_ref[0], gq_ref[...]).astype(BF16)
    ck = _rms(k_ref[0], gk_ref[...]).astype(BF16)
    s = lax.dot_general(cq, ck, NT, preferred_element_type=F32) * (MEM_HEAD_DIM ** -0.5)
    m = jnp.max(s, axis=-1, keepdims=True)
    p = jnp.exp(s - m)
    l = jnp.sum(p, axis=-1, keepdims=True)
    o = jnp.dot(p.astype(BF16), v_ref[0].astype(BF16), preferred_element_type=F32) / l
    o_ref[0] = o.astype(o_ref.dtype)


def _mem_attn(proj3, kv3, gq, gk, q_col):
    B, S, _ = proj3.shape
    M = kv3.shape[1]
    hd = MEM_HEAD_DIM
    tq = 1024
    return pl.pallas_call(
        _mem_attn_kernel,
        grid=(B, MEM_HEADS, S // tq),
        in_specs=[pl.BlockSpec((1, tq, hd), lambda b, h, i: (b, i, q_col + h)),
                  pl.BlockSpec((1, M, hd), lambda b, h, i: (b, 0, h)),
                  pl.BlockSpec((1, M, hd), lambda b, h, i: (b, 0, MEM_HEADS + h)),
                  pl.BlockSpec((1, hd), lambda b, h, i: (0, 0)),
                  pl.BlockSpec((1, hd), lambda b, h, i: (0, 0))],
        out_specs=pl.BlockSpec((1, tq, hd), lambda b, h, i: (b, i, h)),
        out_shape=jax.ShapeDtypeStruct((B, S, MEM_HEADS * hd), BF16),
        compiler_params=_params("parallel", "parallel", "parallel"),
        name="mem_attn",
    )(proj3, kv3, kv3, gq, gk)


def _dn_prep_kernel(x_ref, w_ref, o_ref, pad_ref):
    S = x_ref.shape[1]
    cb = pl.program_id(1)
    x = x_ref[0]
    pad_ref[0:8, :] = jnp.zeros((8, LANE), F32)
    pad_ref[8:, :] = x
    w = w_ref[...]
    y = w[DN_CONV - 1:DN_CONV, :] * x
    for j in range(DN_CONV - 1):
        off = 8 - (DN_CONV - 1) + j
        y = y + w[j:j + 1, :] * pad_ref[off:off + S, :]
    y = y * jax.nn.sigmoid(y)
    nrm = lax.rsqrt(jnp.sum(y * y, axis=-1, keepdims=True) + EPS)
    scale = jnp.where(cb < DN_HEADS, nrm * (LANE ** -0.5), jnp.where(cb < 2 * DN_HEADS, nrm, 1.0))
    o_ref[0] = y * scale


def _dn_prep(proj3, conv_w, col0):
    B, S, _ = proj3.shape
    ncb = 3 * DN_HEADS
    return pl.pallas_call(
        _dn_prep_kernel,
        grid=(B, ncb),
        in_specs=[pl.BlockSpec((1, S, LANE), lambda b, c: (b, 0, col0 + c)),
                  pl.BlockSpec((DN_CONV, LANE), lambda b, c: (0, c))],
        out_specs=pl.BlockSpec((1, S, LANE), lambda b, c: (b, 0, c)),
        out_shape=jax.ShapeDtypeStruct((B, S, ncb * LANE), F32),
        scratch_shapes=[pltpu.VMEM((S + 8, LANE), F32)],
        compiler_params=_params("parallel", "parallel"),
        name="dn_prep",
    )(proj3, conv_w)


def _softplus(x):
    return jnp.maximum(x, 0.0) + jnp.log1p(jnp.exp(-jnp.abs(x)))


def _split_bf16(a):
    hi = a.astype(BF16)
    return hi, (a - hi.astype(F32)).astype(BF16)


def _mm_split3(a, b):
    ah, al = _split_bf16(a)
    bh, bl = _split_bf16(b)
    return jnp.dot(jnp.concatenate([ah, ah, al], axis=1), jnp.concatenate([bh, bl, bh], axis=0),
                   preferred_element_type=F32)


def _dn_scan_kernel(q_ref, k_ref, v_ref, b_ref, a_ref, at_ref, alr_ref, dtr_ref, alc_ref, dtc_ref,
                    z_ref, g_ref, o_ref, st_ref):
    C = DN_CHUNK
    H = DN_HEADS
    R = q_ref.shape[0]

    @pl.when(pl.program_id(1) == 0)
    def _():
        st_ref[...] = jnp.zeros_like(st_ref)

    ii = lax.broadcasted_iota(jnp.int32, (C, C), 0)
    jj = lax.broadcasted_iota(jnp.int32, (C, C), 1)
    incl = ii >= jj
    strict = ii > jj
    tril = incl.astype(F32)
    triu = (ii <= jj).astype(F32)
    gn = g_ref[...]

    gc_col, gc_row, beta = [], [], []
    for r in range(R):
        g_col = -jnp.exp(alr_ref[...]) * _softplus(a_ref[r] + dtr_ref[...])
        g_row = -jnp.exp(alc_ref[...]) * _softplus(at_ref[r, 0] + dtc_ref[...])
        gc_col.append(jnp.dot(tril, g_col, precision=HI, preferred_element_type=F32))
        gc_row.append(jnp.dot(g_row, triu, precision=HI, preferred_element_type=F32))
        beta.append(jax.nn.sigmoid(b_ref[r]))

    def mm(a, b):
        return jnp.dot(a.astype(BF16), b.astype(BF16), preferred_element_type=F32)

    units = [(r, h) for r in range(R) for h in range(H)]
    U = range(len(units))
    sl = [slice(h * LANE, (h + 1) * LANE) for _, h in units]
    q = [q_ref[r, :, sl[u]] for u, (r, h) in enumerate(units)]
    k = [k_ref[r, :, sl[u]] for u, (r, h) in enumerate(units)]
    gcc = [gc_col[r][:, h:h + 1] for r, h in units]
    bet = [beta[r][:, h:h + 1] for r, h in units]
    decay = [jnp.where(incl, jnp.exp(jnp.where(incl, gcc[u] - gc_row[r][h:h + 1, :], 0.0)), 0.0)
             for u, (r, h) in enumerate(units)]
    kb = [k[u] * bet[u] for u in U]
    vb = [v_ref[r, :, sl[u]] * bet[u] for u, (r, h) in enumerate(units)]
    kbf = [k[u].astype(BF16) for u in U]
    kk = [lax.dot_general(kb[u].astype(BF16), kbf[u], NT, preferred_element_type=F32) for u in U]
    qk = [lax.dot_general(q[u].astype(BF16), kbf[u], NT, preferred_element_type=F32) for u in U]
    n = [jnp.where(strict, kk[u] * decay[u], 0.0) for u in U]
    attn = [jnp.where(incl, qk[u] * decay[u], 0.0) for u in U]
    egc = [jnp.exp(gcc[u]) for u in U]
    x = [jnp.concatenate([vb[u], kb[u] * egc[u]], axis=1) for u in U]
    nx = [_mm_split3(n[u], x[u]) for u in U]
    p = [_mm_split3(n[u], n[u]) for u in U]
    x = [x[u] - nx[u] for u in U]
    levels = int(math.log2(C)) - 1
    for lvl in range(levels):
        px = [_mm_split3(p[u], x[u]) for u in U]
        if lvl + 1 < levels:
            p = [_mm_split3(p[u], p[u]) for u in U]
        x = [x[u] + px[u] for u in U]
    s = [st_ref[u] for u in U]
    sb = [s[u].astype(BF16) for u in U]
    ws = [mm(x[u][:, LANE:], sb[u]) for u in U]
    qs = [mm(q[u] * egc[u], sb[u]) for u in U]
    v_new = [(x[u][:, :LANE] - ws[u]).astype(BF16) for u in U]
    av = [mm(attn[u], v_new[u]) for u in U]
    g_last = [gcc[u][C - 1:C, :] for u in U]
    kd = [(k[u] * jnp.exp(g_last[u] - gcc[u])).astype(BF16) for u in U]
    kv = [lax.dot_general(kd[u], v_new[u], TN, preferred_element_type=F32) for u in U]
    for u, (r, h) in enumerate(units):
        st_ref[u] = s[u] * jnp.exp(g_last[u]) + kv[u]
        zz = z_ref[r, :, sl[u]].astype(F32)
        o_ref[r, :, sl[u]] = (_rms(qs[u] + av[u], gn) * (zz * jax.nn.sigmoid(zz))).astype(o_ref.dtype)


DN_ROWS = 2


def _dn_scan(qkv, braw, araw, araw_t, a_log, dt_bias, proj3, z_col, onorm_g):
    B, S, _ = qkv.shape
    H = DN_HEADS
    C = DN_CHUNK
    W = H * LANE
    R = DN_ROWS
    tile = lambda col: pl.BlockSpec((R, C, W), lambda b, c: (b, c, col))
    small = lambda shape: pl.BlockSpec(shape, lambda b, c: (0,) * len(shape))
    return pl.pallas_call(
        _dn_scan_kernel,
        grid=(B // R, S // C),
        in_specs=[tile(0), tile(1), tile(2),
                  pl.BlockSpec((R, C, H), lambda b, c: (b, c, 0)),
                  pl.BlockSpec((R, C, H), lambda b, c: (b, c, 0)),
                  pl.BlockSpec((R, 1, H, C), lambda b, c: (b, c, 0, 0)),
                  small((1, H)), small((1, H)), small((H, 1)), small((H, 1)),
                  tile(z_col), small((1, LANE))],
        out_specs=tile(0),
        out_shape=jax.ShapeDtypeStruct((B, S, W), BF16),
        scratch_shapes=[pltpu.VMEM((R * H, LANE, LANE), F32)],
        compiler_params=_params("parallel", "arbitrary"),
        name="dn_scan",
    )(qkv, qkv, qkv, braw, araw, araw_t, a_log.reshape(1, H), dt_bias.reshape(1, H),
      a_log.reshape(H, 1), dt_bias.reshape(H, 1), proj3, onorm_g)


def _merge_kernel(x_ref, b0_ref, b1_ref, b2_ref, g0_ref, g1_ref, g2_ref, wb_ref, wo_ref, o_ref):
    mixed = None
    for i, (b_ref, g_ref) in enumerate(((b0_ref, g0_ref), (b1_ref, g1_ref), (b2_ref, g2_ref))):
        bp = jnp.dot(b_ref[...], wb_ref[i], preferred_element_type=F32)
        t = jax.nn.sigmoid(g_ref[...].astype(F32)) * bp
        mixed = t if mixed is None else mixed + t
    o_ref[...] = x_ref[...] + jnp.dot(mixed.astype(BF16), wo_ref[...], preferred_element_type=F32)


def _merge(x2, moba_out, dn_out, mem_out, proj, gate_col, w_branch, w_out):
    T, D = x2.shape
    tm = 256
    row = lambda col: pl.BlockSpec((tm, D), lambda i: (i, col))
    return pl.pallas_call(
        _merge_kernel,
        grid=(T // tm,),
        in_specs=[row(0), row(0), row(0), row(0), row(gate_col), row(gate_col + 1), row(gate_col + 2),
                  pl.BlockSpec((3, D, D), lambda i: (0, 0, 0)),
                  pl.BlockSpec((D, D), lambda i: (0, 0))],
        out_specs=row(0),
        out_shape=jax.ShapeDtypeStruct((T, D), F32),
        compiler_params=_params("parallel"),
        name="merge",
    )(x2, moba_out, dn_out, mem_out, proj, proj, proj, w_branch, w_out)


def _compare_exchange(a, b):
    if a is None:
        return b, None
    if b is None:
        return a, None
    return jnp.maximum(a, b), jnp.minimum(a, b)


def _bitonic_merge_desc(xs):
    n = len(xs)
    j = n // 2
    while j >= 1:
        for i in range(n):
            l = i ^ j
            if l > i:
                xs[i], xs[l] = _compare_exchange(xs[i], xs[l])
        j //= 2
    return xs


def _bitonic_sort_desc(xs):
    xs = list(xs)
    n = len(xs)
    k = 2
    while k <= n:
        j = k // 2
        while j >= 1:
            for i in range(n):
                l = i ^ j
                if l > i:
                    hi, lo = _compare_exchange(xs[i], xs[l])
                    xs[i], xs[l] = (hi, lo) if (i & k) == 0 else (lo, hi)
            j //= 2
        k *= 2
    return xs


def _top16_over_rows(pieces):
    K = PEER_TOPK
    xs = _bitonic_sort_desc(list(pieces) + [None] * (K - len(pieces)))
    for shift in (4, 2, 1):
        other = [None if x is None else pltpu.roll(x, shift, 0) for x in xs]
        merged = []
        for i in range(K):
            a, b = xs[i], other[K - 1 - i]
            merged.append(b if a is None else a if b is None else jnp.maximum(a, b))
        xs = _bitonic_merge_desc(merged)
    return xs


def _rows_from_replicated(vals, sub):
    out = vals[0]
    for r in range(1, 8):
        out = jnp.where(sub == r, vals[r], out)
    return out


def _peer_select_kernel(qp_ref, k1_ref, k2_ref, s1_ref, s2_ref, tau_ref, cc_ref):
    tm = qp_ref.shape[0]
    K = PEER_KEYS
    q = qp_ref[...]
    s1_ref[0] = lax.dot_general(k1_ref[...], q[:, :K], NT, precision=HI, preferred_element_type=F32)
    s2_ref[0] = lax.dot_general(k2_ref[...], q[:, K:], NT, precision=HI, preferred_element_type=F32)
    sub = lax.broadcasted_iota(jnp.int32, (8, LANE), 0)
    ninf = -jnp.inf
    for c in range(tm // LANE):
        cs = slice(c * LANE, (c + 1) * LANE)
        v1 = _top16_over_rows([s1_ref[0, 8 * r:8 * r + 8, cs] for r in range(K // 8)])
        v2 = _top16_over_rows([s2_ref[0, 8 * r:8 * r + 8, cs] for r in range(K // 8)])
        v1lo, v1hi = _rows_from_replicated(v1[:8], sub), _rows_from_replicated(v1[8:], sub)
        v2lo, v2hi = _rows_from_replicated(v2[:8], sub), _rows_from_replicated(v2[8:], sub)
        cands = [
            v1[0] + v2lo, v1[0] + v2hi,
            v1[1] + v2lo,
            jnp.where(sub >= 2, v2[0] + v1lo, ninf), v2[0] + v1hi,
            jnp.where(sub >= 2, v2[1] + v1lo, ninf),
            jnp.where((sub >= 2) & (sub <= 4), v1[2] + v2lo, ninf),
            jnp.where((sub >= 2) & (sub <= 3), v1[3] + v2lo, ninf),
            jnp.where(sub == 2, v1[4] + v2lo, ninf),
        ]
        top = _top16_over_rows(cands)
        smax = top[0]
        z = jnp.exp(top[0] - smax)
        for t in top[1:]:
            z = z + jnp.exp(t - smax)
        tau_ref[0, :, cs] = top[PEER_TOPK - 1][0:1, :]
        cc_ref[0, :, cs] = (smax + jnp.log(z))[0:1, :]


def _peer_select(qp, keys1, keys2):
    T = qp.shape[0]
    H, K = PEER_HEADS, PEER_KEYS
    tm = 512
    return pl.pallas_call(
        _peer_select_kernel,
        grid=(T // tm, H),
        in_specs=[pl.BlockSpec((tm, 2 * K), lambda i, h: (i, h)),
                  pl.BlockSpec((K, K), lambda i, h: (0, 0)),
                  pl.BlockSpec((K, K), lambda i, h: (0, 0))],
        out_specs=[pl.BlockSpec((1, K, tm), lambda i, h: (h, 0, i)),
                   pl.BlockSpec((1, K, tm), lambda i, h: (h, 0, i)),
                   pl.BlockSpec((1, 1, tm), lambda i, h: (h, 0, i)),
                   pl.BlockSpec((1, 1, tm), lambda i, h: (h, 0, i))],
        out_shape=[jax.ShapeDtypeStruct((H, K, T), F32), jax.ShapeDtypeStruct((H, K, T), F32),
                   jax.ShapeDtypeStruct((H, 1, T), F32), jax.ShapeDtypeStruct((H, 1, T), F32)],
        compiler_params=_params("parallel", "parallel"),
        name="peer_select",
    )(qp, keys1, keys2)


GELU_C0 = math.sqrt(2.0 / math.pi)
GELU_C1 = GELU_C0 * 0.044715
PEER_ROWS = 256
PEER_LANES = 128


def _peer_dense_kernel(h_ref, u_ref, vt_ref, s1_ref, s2_ref, tau_ref, cc_ref, x_ref, o_ref,
                       e1_ref, e2_ref, taub_ref, rowb_ref, rowe_ref, g_ref, w_ref, acc_ref):
    K = PEER_KEYS
    H = PEER_HEADS
    SUB = 16
    j = pl.program_id(1)
    te, tm = g_ref.shape
    n_a = te // K

    @pl.when(j == 0)
    def _():
        acc_ref[...] = jnp.zeros_like(acc_ref)
        for h in range(H):
            s1 = s1_ref[h]
            m1 = jnp.max(s1, axis=0, keepdims=True)
            e1_ref[h] = jnp.exp(s1 - m1)
            e2_ref[h] = (0.5 * jnp.exp(s2_ref[h] - (cc_ref[h] - m1))).astype(BF16)
            taub_ref[h] = jnp.broadcast_to(tau_ref[h], (8, tm))

    g_ref[...] = lax.dot_general(u_ref[...], h_ref[...], NT, preferred_element_type=F32)

    a0 = pl.multiple_of(j * n_a, n_a)
    for h in range(H):
        s1t = s1_ref[h, pl.ds(a0, n_a), :]
        e1t = e1_ref[h, pl.ds(a0, n_a), :]
        for r in range(n_a):
            rowb_ref[h, r, :, 0:tm] = jnp.broadcast_to(s1t[r:r + 1, :], (8, tm))
            rowe_ref[h, r, :, 0:tm] = jnp.broadcast_to(e1t[r:r + 1, :], (SUB, tm)).astype(BF16)

    def slab(sb, carry):
        b0 = pl.multiple_of(sb * SUB, SUB)
        for l0 in range(0, tm, PEER_LANES):
            ls = slice(l0, l0 + PEER_LANES)
            w = [None] * n_a
            for h in range(H):
                s2s = s2_ref[h, pl.ds(b0, SUB), ls]
                e2s = e2_ref[h, pl.ds(b0, SUB), ls]
                tb = taub_ref[h, :, ls]
                tb = jnp.concatenate([tb, tb], axis=0)
                for r in range(n_a):
                    s1b = rowb_ref[h, r, :, ls]
                    s1b = jnp.concatenate([s1b, s1b], axis=0)
                    t = jnp.where(s2s + s1b >= tb, e2s * rowe_ref[h, r, :, ls], jnp.zeros((), BF16))
                    w[r] = t if w[r] is None else w[r] + t
            for r in range(n_a):
                w_ref[r, pl.ds(b0, SUB), ls] = w[r]
        return carry

    lax.fori_loop(0, K // SUB, slab, 0)

    total = None
    for c in range(te // PEER_ROWS):
        rows = slice(c * PEER_ROWS, (c + 1) * PEER_ROWS)
        x = g_ref[rows, :]
        th = jnp.tanh(x * (GELU_C0 + GELU_C1 * (x * x)))
        w = w_ref[c * (PEER_ROWS // K):(c + 1) * (PEER_ROWS // K)].reshape(PEER_ROWS, tm)
        wa = (w.astype(F32) * (x + x * th)).astype(BF16)
        d = jnp.dot(vt_ref[:, rows], wa, preferred_element_type=F32)
        total = d if total is None else total + d
    acc_ref[...] += total

    @pl.when(j == pl.num_programs(1) - 1)
    def _():
        o_ref[...] = x_ref[...] + acc_ref[...].T


def _peer_dense(h2, u, vt, s1, s2, tau, cc, x1):
    T, D = h2.shape
    E = u.shape[0]
    H, K = PEER_HEADS, PEER_KEYS
    tm, te = 512, 1024
    return pl.pallas_call(
        _peer_dense_kernel,
        grid=(T // tm, E // te),
        in_specs=[pl.BlockSpec((tm, D), lambda i, j: (i, 0)),
                  pl.BlockSpec((te, D), lambda i, j: (j, 0)),
                  pl.BlockSpec((D, te), lambda i, j: (0, j)),
                  pl.BlockSpec((H, K, tm), lambda i, j: (0, 0, i)),
                  pl.BlockSpec((H, K, tm), lambda i, j: (0, 0, i)),
                  pl.BlockSpec((H, 1, tm), lambda i, j: (0, 0, i)),
                  pl.BlockSpec((H, 1, tm), lambda i, j: (0, 0, i)),
                  pl.BlockSpec((tm, D), lambda i, j: (i, 0))],
        out_specs=pl.BlockSpec((tm, D), lambda i, j: (i, 0)),
        out_shape=jax.ShapeDtypeStruct((T, D), F32),
        scratch_shapes=[pltpu.VMEM((H, K, tm), F32),
                        pltpu.VMEM((H, K, tm), BF16),
                        pltpu.VMEM((H, 8, tm), F32),
                        pltpu.VMEM((H, te // K, 8, tm + LANE), F32),
                        pltpu.VMEM((H, te // K, 16, tm + LANE), BF16),
                        pltpu.VMEM((te, tm), F32),
                        pltpu.VMEM((te // K, K, tm), BF16),
                        pltpu.VMEM((D, tm), F32)],
        compiler_params=_params("parallel", "arbitrary"),
        name="peer_dense",
    )(h2, u, vt, s1, s2, tau, cc, x1)


def _rope_tables(S):
    inv = 1.0 / (ROPE_THETA ** (jnp.arange(0, LANE, 2, dtype=F32) / LANE))
    ang = jnp.arange(S, dtype=F32)[:, None] * inv[None, :]
    cos, sin = jnp.cos(ang), jnp.sin(ang)
    return jnp.concatenate([cos, cos], axis=1), jnp.concatenate([-sin, sin], axis=1)


def _layer(x, mem, attn_norm_g, mem_norm_g, ffn_norm_g, w_in, moba_q_norm_g, moba_k_norm_g,
           dn_conv_w, dn_a_log, dn_dt_bias, dn_out_norm_g, w_mem_kv, mem_q_norm_g, mem_k_norm_g,
           w_branch, w_out, peer_w_q, peer_keys1, peer_keys2, peer_u, peer_v):
    B, S, D = x.shape
    T = B * S
    x2 = x.reshape(T, D)
    row = lambda g: g.reshape(1, -1)

    moba_w, dn_w = 3 * MOBA_HEADS * LANE, 3 * DN_HEADS * LANE
    n_small = 2 * DN_HEADS
    o_small = moba_w + dn_w
    qk_w = 2 * MOBA_HEADS * LANE
    o_z = o_small + n_small
    w_f32 = jnp.concatenate([w_in[:, :qk_w], w_in[:, moba_w:o_small], w_in[:, o_z + D:o_z + 2 * D]],
                            axis=1).astype(BF16)
    w_b16 = jnp.concatenate([w_in[:, qk_w:moba_w], w_in[:, o_z:o_z + D], w_in[:, o_z + 2 * D:]],
                            axis=1).astype(BF16)
    w_small = jnp.pad(w_in[:, o_small:o_z], ((0, 0), (0, LANE - n_small)))
    proj = _norm_matmul(x2, row(attn_norm_g), w_f32, tm=1024, tn=1024, name="in_proj")
    projb = _norm_matmul(x2, row(attn_norm_g), w_b16, tm=1024, tn=1024, out_dtype=BF16, name="in_proj_bf16")
    small = _norm_matmul(x2, row(attn_norm_g), w_small, tm=1024, tn=LANE, precision=HI, name="in_proj_scalars")
    proj3 = proj.reshape(B, S, -1)
    projb3 = projb.reshape(B, S, -1)
    dn_col = qk_w // LANE
    memq_col = (qk_w + dn_w) // MEM_HEAD_DIM
    z_col = 1
    gate_col = 2

    cos2, sin2 = _rope_tables(S)
    mq, mk, kmean = _moba_prep(proj3, row(moba_q_norm_g), row(moba_k_norm_g), cos2, sin2)
    moba_out = _moba_attn(_moba_gate(mq, kmean), mk, projb3)

    dn_qkv = _dn_prep(proj3, dn_conv_w, dn_col)
    braw = small[:, :DN_HEADS].reshape(B, S, DN_HEADS)
    araw = small[:, DN_HEADS:n_small].reshape(B, S, DN_HEADS)
    araw_t = araw.reshape(B, S // DN_CHUNK, DN_CHUNK, DN_HEADS).transpose(0, 1, 3, 2)
    dn_out = _dn_scan(dn_qkv, braw, araw, araw_t, dn_a_log, dn_dt_bias, projb3, z_col, row(dn_out_norm_g))

    M = mem.shape[1]
    kv = _norm_matmul(mem.reshape(B * M, D), row(mem_norm_g), w_mem_kv.astype(BF16), tm=B * M, tn=512, name="mem_kv")
    mem_out = _mem_attn(proj3, kv.reshape(B, M, -1), row(mem_q_norm_g), row(mem_k_norm_g), memq_col)

    x1 = _merge(x2, moba_out.reshape(T, D), dn_out.reshape(T, D), mem_out.reshape(T, D), projb, gate_col,
                w_branch.astype(BF16), w_out.astype(BF16))

    qp, h2 = _norm_matmul(x1, row(ffn_norm_g), peer_w_q.astype(BF16), tm=1024, tn=1024, emit_h=True, name="peer_query")
    s1, s2, tau, cc = _peer_select(qp, peer_keys1, peer_keys2)
    out = _peer_dense(h2, peer_u.astype(BF16), peer_v.T.astype(BF16), s1, s2, tau, cc, x1)
    return out.reshape(B, S, D)


def kernel(x, mem, attn_norm_g, mem_norm_g, ffn_norm_g, w_in, moba_q_norm_g, moba_k_norm_g, dn_conv_w, dn_a_log, dn_dt_bias, dn_out_norm_g, w_mem_kv, mem_q_norm_g, mem_k_norm_g, w_branch, w_out, peer_w_q, peer_keys1, peer_keys2, peer_u, peer_v):
    for l in range(w_in.shape[0]):
        x = _layer(x, mem, attn_norm_g[l], mem_norm_g[l], ffn_norm_g[l], w_in[l], moba_q_norm_g[l],
                   moba_k_norm_g[l], dn_conv_w[l], dn_a_log[l], dn_dt_bias[l], dn_out_norm_g[l],
                   w_mem_kv[l], mem_q_norm_g[l], mem_k_norm_g[l], w_branch[l], w_out[l], peer_w_q[l],
                   peer_keys1[l], peer_keys2[l], peer_u[l], peer_v[l])
    return x
```

```python
import functools
import math

import jax
import jax.numpy as jnp
from jax import lax
from jax.experimental import pallas as pl
from jax.experimental.pallas import tpu as pltpu

F32 = jnp.float32
BF16 = jnp.bfloat16
HI = lax.Precision.HIGHEST
EPS = 1e-6
ROPE_THETA = 10000.0
NT = (((1,), (1,)), ((), ()))
TN = (((0,), (0,)), ((), ()))

LANE = 128
MOBA_HEADS = 8
MOBA_BLOCK = 256
MOBA_TOPK = 3
DN_HEADS = 8
DN_CHUNK = 64
DN_CONV = 4
MEM_HEADS = 4
MEM_HEAD_DIM = 256
PEER_HEADS = 8
PEER_KEYS = 128
PEER_TOPK = 16

VMEM_LIMIT = 56 * 1024 * 1024


def _params(*sem):
    return pltpu.CompilerParams(dimension_semantics=sem, vmem_limit_bytes=VMEM_LIMIT)


def _rms(x, g):
    ms = jnp.mean(x * x, axis=-1, keepdims=True)
    return x * lax.rsqrt(ms + EPS) * g


def _norm_matmul_kernel(x_ref, g_ref, w_ref, o_ref, *rest, precision, emit_h):
    if emit_h:
        ho_ref, h_ref = rest
    else:
        (h_ref,) = rest

    @pl.when(pl.program_id(1) == 0)
    def _():
        h = _rms(x_ref[...], g_ref[...])
        h_ref[...] = h.astype(h_ref.dtype)
        if emit_h:
            ho_ref[...] = h.astype(ho_ref.dtype)

    o_ref[...] = jnp.dot(h_ref[...], w_ref[...], precision=precision,
                         preferred_element_type=F32).astype(o_ref.dtype)


def _norm_matmul(x, g, w, *, tm, tn, name, out_dtype=F32, precision=None, emit_h=False):
    T, D = x.shape
    N = w.shape[1]
    out_shape = [jax.ShapeDtypeStruct((T, N), out_dtype)]
    out_specs = [pl.BlockSpec((tm, tn), lambda i, j: (i, j))]
    if emit_h:
        out_shape.append(jax.ShapeDtypeStruct((T, D), BF16))
        out_specs.append(pl.BlockSpec((tm, D), lambda i, j: (i, 0)))
    res = pl.pallas_call(
        functools.partial(_norm_matmul_kernel, precision=precision, emit_h=emit_h),
        grid=(T // tm, N // tn),
        in_specs=[pl.BlockSpec((tm, D), lambda i, j: (i, 0)),
                  pl.BlockSpec((1, D), lambda i, j: (0, 0)),
                  pl.BlockSpec((D, tn), lambda i, j: (0, j))],
        out_specs=out_specs,
        out_shape=out_shape,
        scratch_shapes=[pltpu.VMEM((tm, D), w.dtype)],
        compiler_params=_params("parallel", "arbitrary"),
        name=name,
    )(x, g, w)
    return res if emit_h else res[0]


MOBA_NEG = -1e30


def _moba_prep_kernel(q_ref, k_ref, gq_ref, gk_ref, cos_ref, sin_ref, qo_ref, ko_ref, km_ref):
    cos = cos_ref[...]
    sin = sin_ref[...]

    def norm_rope(x, g):
        y = _rms(x, g)
        return y * cos + pltpu.roll(y, LANE // 2, 1) * sin

    q = norm_rope(q_ref[0], gq_ref[...])
    k = norm_rope(k_ref[0], gk_ref[...])
    ts = k.shape[0]
    nb = ts // MOBA_BLOCK
    qo_ref[0] = q
    row = lax.broadcasted_iota(jnp.int32, (ts, LANE), 0) + pl.program_id(2) * ts
    lane = lax.broadcasted_iota(jnp.int32, (ts, LANE), 1)
    ko_ref[0, :, 0:LANE] = k.astype(BF16)
    ko_ref[0, :, LANE:2 * LANE] = jnp.where(lane == row // MOBA_BLOCK, 1.0, 0.0).astype(BF16)
    km_ref[0, 0] = jnp.mean(k.reshape(nb, MOBA_BLOCK, LANE), axis=1)


def _moba_prep(proj3, gq, gk, cos2, sin2):
    B, S, _ = proj3.shape
    H = MOBA_HEADS
    ts = 2048
    nb_t = ts // MOBA_BLOCK
    col = lambda off: pl.BlockSpec((1, ts, LANE), lambda b, h, s: (b, s, off + h))
    return pl.pallas_call(
        _moba_prep_kernel,
        grid=(B, H, S // ts),
        in_specs=[col(0), col(H),
                  pl.BlockSpec((1, LANE), lambda b, h, s: (0, 0)),
                  pl.BlockSpec((1, LANE), lambda b, h, s: (0, 0)),
                  pl.BlockSpec((ts, LANE), lambda b, h, s: (s, 0)),
                  pl.BlockSpec((ts, LANE), lambda b, h, s: (s, 0))],
        out_specs=[col(0),
                   pl.BlockSpec((1, ts, 2 * LANE), lambda b, h, s: (b, s, h)),
                   pl.BlockSpec((1, 1, nb_t, LANE), lambda b, h, s: (b, h, s, 0))],
        out_shape=[jax.ShapeDtypeStruct((B, S, H * LANE), F32),
                   jax.ShapeDtypeStruct((B, S, H * 2 * LANE), BF16),
                   jax.ShapeDtypeStruct((B, H, S // MOBA_BLOCK, LANE), F32)],
        compiler_params=_params("parallel", "parallel", "parallel"),
        name="moba_prep",
    )(proj3, proj3, gq, gk, cos2, sin2)


def _moba_gate_kernel(q_ref, km_ref, o_ref):
    L = MOBA_BLOCK
    q = q_ref[0]
    km = km_ref[0, 0]
    nb = km.shape[0]
    S = q.shape[0]
    gate = lax.dot_general(km, q, NT, precision=HI, preferred_element_type=F32)
    row = lax.broadcasted_iota(jnp.int32, (nb, S), 0)
    own = lax.broadcasted_iota(jnp.int32, (nb, S), 1) // L
    rank = jnp.zeros((nb, S), jnp.int32)
    for m in range(nb - 1):
        gm = gate[m:m + 1, :]
        cnt = jnp.where(row > m, jnp.where(gm >= gate, 1, 0), jnp.where(gm > gate, 1, 0))
        rank = rank + jnp.where(own > m, cnt, 0)
    keep = jnp.where(row < own, jnp.where(rank < MOBA_TOPK, 1.0, 0.0), jnp.where(row == own, 1.0, 0.0))
    eye = (lax.broadcasted_iota(jnp.int32, (nb, LANE), 0)
           == lax.broadcasted_iota(jnp.int32, (nb, LANE), 1)).astype(BF16)
    keep_t = lax.dot_general(keep.astype(BF16), eye, TN, preferred_element_type=F32)
    o_ref[0, :, 0:LANE] = (q * (LANE ** -0.5 * math.log2(math.e))).astype(BF16)
    o_ref[0, :, LANE:2 * LANE] = jnp.where(keep_t > 0.5, 0.0, MOBA_NEG).astype(BF16)


def _moba_gate(q, kmean):
    B, S, W = q.shape
    H = MOBA_HEADS
    nb = kmean.shape[2]
    return pl.pallas_call(
        _moba_gate_kernel,
        grid=(B, H),
        in_specs=[pl.BlockSpec((1, S, LANE), lambda b, h: (b, 0, h)),
                  pl.BlockSpec((1, 1, nb, LANE), lambda b, h: (b, h, 0, 0))],
        out_specs=pl.BlockSpec((1, S, 2 * LANE), lambda b, h: (b, 0, h)),
        out_shape=jax.ShapeDtypeStruct((B, S, 2 * W), BF16),
        compiler_params=_params("parallel", "parallel"),
        name="moba_gate",
    )(q, kmean)


def _moba_attn_kernel(qa_ref, qb_ref, k_ref, v_ref, oa_ref, ob_ref, qaug_ref, s_ref):
    L = MOBA_BLOCK
    nb = k_ref.shape[1] // L
    half = nb // 2
    p = pl.program_id(2)
    tiles = (p, nb - 1 - p)
    qaug_ref[0] = qa_ref[0]
    qaug_ref[1] = qb_ref[0]

    def scores(qa, n):
        kb = k_ref[0, pl.ds(pl.multiple_of(n * L, L), L), :]
        return lax.dot_general(qa, kb, NT, preferred_element_type=F32)

    def half_max(s):
        return jnp.maximum(s[:, :LANE], s[:, LANE:])

    def dyn_slot(j):
        first = j < p
        return first, jnp.where(first, 0, 1), jnp.where(first, j, half + j - p)

    r = lax.broadcasted_iota(jnp.int32, (L, L), 0)
    c = lax.broadcasted_iota(jnp.int32, (L, L), 1)
    causal = jnp.where(c <= r, 0.0, MOBA_NEG)

    m = [None, None]
    for t in range(2):
        s = scores(qaug_ref[t], tiles[t]) + causal
        s_ref[nb - 1 + t] = s
        m[t] = half_max(s)
    for n in range(half):
        s = scores(qaug_ref[1], n)
        s_ref[n] = s
        m[1] = jnp.maximum(m[1], half_max(s))
    for j in range(half - 1):
        first, t, n = dyn_slot(j)
        s = scores(qaug_ref[t], n)
        s_ref[half + j] = s
        hm = half_max(s)
        m[0] = jnp.maximum(m[0], jnp.where(first, hm, MOBA_NEG))
        m[1] = jnp.maximum(m[1], jnp.where(first, MOBA_NEG, hm))
    mb = [jnp.broadcast_to(jnp.max(m[t], axis=-1, keepdims=True), (L, LANE)) for t in range(2)]

    def probs(k, mbt):
        s = s_ref[k]
        return jnp.concatenate([jnp.exp2(s[:, :LANE] - mbt), jnp.exp2(s[:, LANE:] - mbt)], axis=1)

    def value(n):
        return v_ref[0, pl.ds(pl.multiple_of(n * L, L), L), :]

    l = [None, None]
    acc = [None, None]
    for t in range(2):
        pr = probs(nb - 1 + t, mb[t])
        l[t] = pr[:, :LANE] + pr[:, LANE:]
        acc[t] = jnp.dot(pr.astype(BF16), value(tiles[t]), preferred_element_type=F32)
    for n in range(half):
        pr = probs(n, mb[1])
        l[1] = l[1] + (pr[:, :LANE] + pr[:, LANE:])
        acc[1] = acc[1] + jnp.dot(pr.astype(BF16), value(n), preferred_element_type=F32)
    for j in range(half - 1):
        first, t, n = dyn_slot(j)
        pr = probs(half + j, jnp.where(first, mb[0], mb[1]))
        ps = pr[:, :LANE] + pr[:, LANE:]
        d = jnp.dot(pr.astype(BF16), value(n), preferred_element_type=F32)
        l[0] = l[0] + jnp.where(first, ps, 0.0)
        l[1] = l[1] + jnp.where(first, 0.0, ps)
        acc[0] = acc[0] + jnp.where(first, d, 0.0)
        acc[1] = acc[1] + jnp.where(first, 0.0, d)

    for t, o_ref in enumerate((oa_ref, ob_ref)):
        o_ref[0] = (acc[t] / jnp.sum(l[t], axis=-1, keepdims=True)).astype(o_ref.dtype)


def _moba_attn(q_aug, k_aug, v):
    B, S, _ = v.shape
    H = MOBA_HEADS
    W = H * LANE
    L = MOBA_BLOCK
    nb = S // L
    half = nb // 2
    lo, hi = pl.pallas_call(
        _moba_attn_kernel,
        grid=(B, H, half),
        in_specs=[pl.BlockSpec((1, L, 2 * LANE), lambda b, h, p: (b, p, h)),
                  pl.BlockSpec((1, L, 2 * LANE), lambda b, h, p: (b, nb - 1 - p, h)),
                  pl.BlockSpec((1, S, 2 * LANE), lambda b, h, p: (b, 0, h)),
                  pl.BlockSpec((1, S, LANE), lambda b, h, p: (b, 0, h))],
        out_specs=[pl.BlockSpec((1, L, LANE), lambda b, h, p: (b, p, h)),
                   pl.BlockSpec((1, L, LANE), lambda b, h, p: (b, half - 1 - p, h))],
        out_shape=[jax.ShapeDtypeStruct((B, S // 2, W), BF16), jax.ShapeDtypeStruct((B, S // 2, W), BF16)],
        scratch_shapes=[pltpu.VMEM((2, L, 2 * LANE), BF16),
                        pltpu.VMEM((nb + 1, L, L), F32)],
        compiler_params=_params("parallel", "parallel", "arbitrary"),
        name="moba_attn",
    )(q_aug, q_aug, k_aug, v)
    return jnp.concatenate([lo, hi], axis=1)


def _mem_attn_kernel(q_ref, k_ref, v_ref, gq_ref, gk_ref, o_ref):
    cq = _rms(q_ref[0], gq_ref[...]).astype(BF16)
    ck = _rms(k_ref[0], gk_ref[...]).astype(BF16)
    s = lax.dot_general(cq, ck, NT, preferred_element_type=F32) * (MEM_HEAD_DIM ** -0.5)
    m = jnp.max(s, axis=-1, keepdims=True)
    p = jnp.exp(s - m)
    l = jnp.sum(p, axis=-1, keepdims=True)
    o = jnp.dot(p.astype(BF16), v_ref[0].astype(BF16), preferred_element_type=F32) / l
    o_ref[0] = o.astype(o_ref.dtype)


def _mem_attn(proj3, kv3, gq, gk, q_col):
    B, S, _ = proj3.shape
    M = kv3.shape[1]
    hd = MEM_HEAD_DIM
    tq = 1024
    return pl.pallas_call(
        _mem_attn_kernel,
        grid=(B, MEM_HEADS, S // tq),
        in_specs=[pl.BlockSpec((1, tq, hd), lambda b, h, i: (b, i, q_col + h)),
                  pl.BlockSpec((1, M, hd), lambda b, h, i: (b, 0, h)),
                  pl.BlockSpec((1, M, hd), lambda b, h, i: (b, 0, MEM_HEADS + h)),
                  pl.BlockSpec((1, hd), lambda b, h, i: (0, 0)),
                  pl.BlockSpec((1, hd), lambda b, h, i: (0, 0))],
        out_specs=pl.BlockSpec((1, tq, hd), lambda b, h, i: (b, i, h)),
        out_shape=jax.ShapeDtypeStruct((B, S, MEM_HEADS * hd), BF16),
        compiler_params=_params("parallel", "parallel", "parallel"),
        name="mem_attn",
    )(proj3, kv3, kv3, gq, gk)


def _dn_prep_kernel(x_ref, w_ref, o_ref, pad_ref):
    S = x_ref.shape[1]
    cb = pl.program_id(1)
    x = x_ref[0]
    pad_ref[0:8, :] = jnp.zeros((8, LANE), F32)
    pad_ref[8:, :] = x
    w = w_ref[...]
    y = w[DN_CONV - 1:DN_CONV, :] * x
    for j in range(DN_CONV - 1):
        off = 8 - (DN_CONV - 1) + j
        y = y + w[j:j + 1, :] * pad_ref[off:off + S, :]
    y = y * jax.nn.sigmoid(y)
    nrm = lax.rsqrt(jnp.sum(y * y, axis=-1, keepdims=True) + EPS)
    scale = jnp.where(cb < DN_HEADS, nrm * (LANE ** -0.5), jnp.where(cb < 2 * DN_HEADS, nrm, 1.0))
    o_ref[0] = y * scale


def _dn_prep(proj3, conv_w, col0):
    B, S, _ = proj3.shape
    ncb = 3 * DN_HEADS
    return pl.pallas_call(
        _dn_prep_kernel,
        grid=(B, ncb),
        in_specs=[pl.BlockSpec((1, S, LANE), lambda b, c: (b, 0, col0 + c)),
                  pl.BlockSpec((DN_CONV, LANE), lambda b, c: (0, c))],
        out_specs=pl.BlockSpec((1, S, LANE), lambda b, c: (b, 0, c)),
        out_shape=jax.ShapeDtypeStruct((B, S, ncb * LANE), F32),
        scratch_shapes=[pltpu.VMEM((S + 8, LANE), F32)],
        compiler_params=_params("parallel", "parallel"),
        name="dn_prep",
    )(proj3, conv_w)


def _softplus(x):
    return jnp.maximum(x, 0.0) + jnp.log1p(jnp.exp(-jnp.abs(x)))


def _split_bf16(a):
    hi = a.astype(BF16)
    return hi, (a - hi.astype(F32)).astype(BF16)


def _mm_split3(a, b):
    ah, al = _split_bf16(a)
    bh, bl = _split_bf16(b)
    return jnp.dot(jnp.concatenate([ah, ah, al], axis=1), jnp.concatenate([bh, bl, bh], axis=0),
                   preferred_element_type=F32)


def _dn_scan_kernel(q_ref, k_ref, v_ref, b_ref, a_ref, at_ref, alr_ref, dtr_ref, alc_ref, dtc_ref,
                    z_ref, g_ref, o_ref, st_ref):
    C = DN_CHUNK
    H = DN_HEADS
    R = q_ref.shape[0]

    @pl.when(pl.program_id(1) == 0)
    def _():
        st_ref[...] = jnp.zeros_like(st_ref)

    ii = lax.broadcasted_iota(jnp.int32, (C, C), 0)
    jj = lax.broadcasted_iota(jnp.int32, (C, C), 1)
    incl = ii >= jj
    strict = ii > jj
    tril = incl.astype(F32)
    triu = (ii <= jj).astype(F32)
    gn = g_ref[...]

    gc_col, gc_row, beta = [], [], []
    for r in range(R):
        g_col = -jnp.exp(alr_ref[...]) * _softplus(a_ref[r] + dtr_ref[...])
        g_row = -jnp.exp(alc_ref[...]) * _softplus(at_ref[r, 0] + dtc_ref[...])
        gc_col.append(jnp.dot(tril, g_col, precision=HI, preferred_element_type=F32))
        gc_row.append(jnp.dot(g_row, triu, precision=HI, preferred_element_type=F32))
        beta.append(jax.nn.sigmoid(b_ref[r]))

    def mm(a, b):
        return jnp.dot(a.astype(BF16), b.astype(BF16), preferred_element_type=F32)

    units = [(r, h) for r in range(R) for h in range(H)]
    U = range(len(units))
    sl = [slice(h * LANE, (h + 1) * LANE) for _, h in units]
    q = [q_ref[r, :, sl[u]] for u, (r, h) in enumerate(units)]
    k = [k_ref[r, :, sl[u]] for u, (r, h) in enumerate(units)]
    gcc = [gc_col[r][:, h:h + 1] for r, h in units]
    bet = [beta[r][:, h:h + 1] for r, h in units]
    decay = [jnp.where(incl, jnp.exp(jnp.where(incl, gcc[u] - gc_row[r][h:h + 1, :], 0.0)), 0.0)
             for u, (r, h) in enumerate(units)]
    kb = [k[u] * bet[u] for u in U]
    vb = [v_ref[r, :, sl[u]] * bet[u] for u, (r, h) in enumerate(units)]
    kbf = [k[u].astype(BF16) for u in U]
    kk = [lax.dot_general(kb[u].astype(BF16), kbf[u], NT, preferred_element_type=F32) for u in U]
    qk = [lax.dot_general(q[u].astype(BF16), kbf[u], NT, preferred_element_type=F32) for u in U]
    n = [jnp.where(strict, kk[u] * decay[u], 0.0) for u in U]
    attn = [jnp.where(incl, qk[u] * decay[u], 0.0) for u in U]
    egc = [jnp.exp(gcc[u]) for u in U]
    x = [jnp.concatenate([vb[u], kb[u] * egc[u]], axis=1) for u in U]
    nx = [_mm_split3(n[u], x[u]) for u in U]
    p = [_mm_split3(n[u], n[u]) for u in U]
    x = [x[u] - nx[u] for u in U]
    levels = int(math.log2(C)) - 1
    for lvl in range(levels):
        px = [_mm_split3(p[u], x[u]) for u in U]
        if lvl + 1 < levels:
            p = [_mm_split3(p[u], p[u]) for u in U]
        x = [x[u] + px[u] for u in U]
    s = [st_ref[u] for u in U]
    sb = [s[u].astype(BF16) for u in U]
    ws = [mm(x[u][:, LANE:], sb[u]) for u in U]
    qs = [mm(q[u] * egc[u], sb[u]) for u in U]
    v_new = [(x[u][:, :LANE] - ws[u]).astype(BF16) for u in U]
    av = [mm(attn[u], v_new[u]) for u in U]
    g_last = [gcc[u][C - 1:C, :] for u in U]
    kd = [(k[u] * jnp.exp(g_last[u] - gcc[u])).astype(BF16) for u in U]
    kv = [lax.dot_general(kd[u], v_new[u], TN, preferred_element_type=F32) for u in U]
    for u, (r, h) in enumerate(units):
        st_ref[u] = s[u] * jnp.exp(g_last[u]) + kv[u]
        zz = z_ref[r, :, sl[u]].astype(F32)
        o_ref[r, :, sl[u]] = (_rms(qs[u] + av[u], gn) * (zz * jax.nn.sigmoid(zz))).astype(o_ref.dtype)


DN_ROWS = 2


def _dn_scan(qkv, braw, araw, araw_t, a_log, dt_bias, proj3, z_col, onorm_g):
    B, S, _ = qkv.shape
    H = DN_HEADS
    C = DN_CHUNK
    W = H * LANE
    R = DN_ROWS
    tile = lambda col: pl.BlockSpec((R, C, W), lambda b, c: (b, c, col))
    small = lambda shape: pl.BlockSpec(shape, lambda b, c: (0,) * len(shape))
    return pl.pallas_call(
        _dn_scan_kernel,
        grid=(B // R, S // C),
        in_specs=[tile(0), tile(1), tile(2),
                  pl.BlockSpec((R, C, H), lambda b, c: (b, c, 0)),
                  pl.BlockSpec((R, C, H), lambda b, c: (b, c, 0)),
                  pl.BlockSpec((R, 1, H, C), lambda b, c: (b, c, 0, 0)),
                  small((1, H)), small((1, H)), small((H, 1)), small((H, 1)),
                  tile(z_col), small((1, LANE))],
        out_specs=tile(0),
        out_shape=jax.ShapeDtypeStruct((B, S, W), BF16),
        scratch_shapes=[pltpu.VMEM((R * H, LANE, LANE), F32)],
        compiler_params=_params("parallel", "arbitrary"),
        name="dn_scan",
    )(qkv, qkv, qkv, braw, araw, araw_t, a_log.reshape(1, H), dt_bias.reshape(1, H),
      a_log.reshape(H, 1), dt_bias.reshape(H, 1), proj3, onorm_g)


def _merge_kernel(x_ref, b0_ref, b1_ref, b2_ref, g0_ref, g1_ref, g2_ref, wb_ref, wo_ref, o_ref):
    mixed = None
    for i, (b_ref, g_ref) in enumerate(((b0_ref, g0_ref), (b1_ref, g1_ref), (b2_ref, g2_ref))):
        bp = jnp.dot(b_ref[...], wb_ref[i], preferred_element_type=F32)
        t = jax.nn.sigmoid(g_ref[...].astype(F32)) * bp
        mixed = t if mixed is None else mixed + t
    o_ref[...] = x_ref[...] + jnp.dot(mixed.astype(BF16), wo_ref[...], preferred_element_type=F32)


def _merge(x2, moba_out, dn_out, mem_out, proj, gate_col, w_branch, w_out):
    T, D = x2.shape
    tm = 256
    row = lambda col: pl.BlockSpec((tm, D), lambda i: (i, col))
    return pl.pallas_call(
        _merge_kernel,
        grid=(T // tm,),
        in_specs=[row(0), row(0), row(0), row(0), row(gate_col), row(gate_col + 1), row(gate_col + 2),
                  pl.BlockSpec((3, D, D), lambda i: (0, 0, 0)),
                  pl.BlockSpec((D, D), lambda i: (0, 0))],
        out_specs=row(0),
        out_shape=jax.ShapeDtypeStruct((T, D), F32),
        compiler_params=_params("parallel"),
        name="merge",
    )(x2, moba_out, dn_out, mem_out, proj, proj, proj, w_branch, w_out)


def _compare_exchange(a, b):
    if a is None:
        return b, None
    if b is None:
        return a, None
    return jnp.maximum(a, b), jnp.minimum(a, b)


def _bitonic_merge_desc(xs):
    n = len(xs)
    j = n // 2
    while j >= 1:
        for i in range(n):
            l = i ^ j
            if l > i:
                xs[i], xs[l] = _compare_exchange(xs[i], xs[l])
        j //= 2
    return xs


def _bitonic_sort_desc(xs):
    xs = list(xs)
    n = len(xs)
    k = 2
    while k <= n:
        j = k // 2
        while j >= 1:
            for i in range(n):
                l = i ^ j
                if l > i:
                    hi, lo = _compare_exchange(xs[i], xs[l])
                    xs[i], xs[l] = (hi, lo) if (i & k) == 0 else (lo, hi)
            j //= 2
        k *= 2
    return xs


def _top16_over_rows(pieces):
    K = PEER_TOPK
    xs = _bitonic_sort_desc(list(pieces) + [None] * (K - len(pieces)))
    for shift in (4, 2, 1):
        other = [None if x is None else pltpu.roll(x, shift, 0) for x in xs]
        merged = []
        for i in range(K):
            a, b = xs[i], other[K - 1 - i]
            merged.append(b if a is None else a if b is None else jnp.maximum(a, b))
        xs = _bitonic_merge_desc(merged)
    return xs


def _rows_from_replicated(vals, sub):
    out = vals[0]
    for r in range(1, 8):
        out = jnp.where(sub == r, vals[r], out)
    return out


def _peer_select_kernel(qp_ref, k1_ref, k2_ref, s1_ref, s2_ref, tau_ref, cc_ref, v2_ref):
    tm = qp_ref.shape[0]
    K = PEER_KEYS
    q = qp_ref[...]
    s1_ref[0] = lax.dot_general(k1_ref[...], q[:, :K], NT, precision=HI, preferred_element_type=F32)
    s2_ref[0] = lax.dot_general(k2_ref[...], q[:, K:], NT, precision=HI, preferred_element_type=F32)
    sub = lax.broadcasted_iota(jnp.int32, (8, LANE), 0)
    ninf = -jnp.inf
    for c in range(tm // LANE):
        cs = slice(c * LANE, (c + 1) * LANE)
        v1 = _top16_over_rows([s1_ref[0, 8 * r:8 * r + 8, cs] for r in range(K // 8)])
        v2 = _top16_over_rows([s2_ref[0, 8 * r:8 * r + 8, cs] for r in range(K // 8)])
        v1lo, v1hi = _rows_from_replicated(v1[:8], sub), _rows_from_replicated(v1[8:], sub)
        v2lo, v2hi = _rows_from_replicated(v2[:8], sub), _rows_from_replicated(v2[8:], sub)
        cands = [
            v1[0] + v2lo, v1[0] + v2hi,
            v1[1] + v2lo,
            jnp.where(sub >= 2, v2[0] + v1lo, ninf), v2[0] + v1hi,
            jnp.where(sub >= 2, v2[1] + v1lo, ninf),
            jnp.where((sub >= 2) & (sub <= 4), v1[2] + v2lo, ninf),
            jnp.where((sub >= 2) & (sub <= 3), v1[3] + v2lo, ninf),
            jnp.where(sub == 2, v1[4] + v2lo, ninf),
        ]
        top = _top16_over_rows(cands)
        smax = top[0]
        z = jnp.exp(top[0] - smax)
        for t in top[1:]:
            z = z + jnp.exp(t - smax)
        v2_ref[0, 0:8, cs] = v2lo
        v2_ref[0, 8:16, cs] = v2hi
        tau_ref[0, :, cs] = top[PEER_TOPK - 1][0:1, :]
        cc_ref[0, :, cs] = (smax + jnp.log(z))[0:1, :]


def _peer_select(qp, keys1, keys2):
    T = qp.shape[0]
    H, K = PEER_HEADS, PEER_KEYS
    tm = 512
    return pl.pallas_call(
        _peer_select_kernel,
        grid=(T // tm, H),
        in_specs=[pl.BlockSpec((tm, 2 * K), lambda i, h: (i, h)),
                  pl.BlockSpec((K, K), lambda i, h: (0, 0)),
                  pl.BlockSpec((K, K), lambda i, h: (0, 0))],
        out_specs=[pl.BlockSpec((1, K, tm), lambda i, h: (h, 0, i)),
                   pl.BlockSpec((1, K, tm), lambda i, h: (h, 0, i)),
                   pl.BlockSpec((1, 1, tm), lambda i, h: (h, 0, i)),
                   pl.BlockSpec((1, 1, tm), lambda i, h: (h, 0, i)),
                   pl.BlockSpec((1, PEER_TOPK, tm), lambda i, h: (h, 0, i))],
        out_shape=[jax.ShapeDtypeStruct((H, K, T), F32), jax.ShapeDtypeStruct((H, K, T), F32),
                   jax.ShapeDtypeStruct((H, 1, T), F32), jax.ShapeDtypeStruct((H, 1, T), F32),
                   jax.ShapeDtypeStruct((H, PEER_TOPK, T), F32)],
        compiler_params=_params("parallel", "parallel"),
        name="peer_select",
    )(qp, keys1, keys2)


GELU_C0 = math.sqrt(2.0 / math.pi)
GELU_C1 = GELU_C0 * 0.044715
PEER_ROWS = 256
PEER_LANES = 256


def _peer_dense_kernel(h_ref, u_ref, vt_ref, s1_ref, s2_ref, tau_ref, cc_ref, v2_ref, x_ref, o_ref,
                       e1_ref, e2_ref, thr_ref, rowb_ref, g_ref, w_ref, acc_ref):
    K = PEER_KEYS
    H = PEER_HEADS
    SUB = 8
    j = pl.program_id(1)
    te, tm = g_ref.shape
    n_a = te // K

    @pl.when(j == 0)
    def _():
        acc_ref[...] = jnp.zeros_like(acc_ref)
        for h in range(H):
            s1 = s1_ref[h]
            m1 = jnp.max(s1, axis=0, keepdims=True)
            e1_ref[h] = jnp.exp(s1 - m1)
            e2_ref[h] = 0.5 * jnp.exp(s2_ref[h] - (cc_ref[h] - m1))
            tau = tau_ref[h]
            thr = jnp.full((K, tm), jnp.inf, F32)
            for k in range(PEER_TOPK):
                v2k = v2_ref[h, k:k + 1, :]
                thr = jnp.where(s1 + v2k >= tau, v2k, thr)
            thr_ref[h] = thr

    g_ref[...] = lax.dot_general(u_ref[...], h_ref[...], NT, preferred_element_type=F32)

    a0 = pl.multiple_of(j * n_a, n_a)
    for h in range(H):
        thrt = thr_ref[h, pl.ds(a0, n_a), :]
        e1t = e1_ref[h, pl.ds(a0, n_a), :]
        for r in range(n_a):
            rowb_ref[h, r, 0, :, 0:tm] = jnp.broadcast_to(thrt[r:r + 1, :], (SUB, tm))
            rowb_ref[h, r, 1, :, 0:tm] = jnp.broadcast_to(e1t[r:r + 1, :], (SUB, tm))

    def slab(sb, carry):
        b0 = pl.multiple_of(sb * SUB, SUB)
        for l0 in range(0, tm, PEER_LANES):
            ls = slice(l0, l0 + PEER_LANES)
            w = [None] * n_a
            for h in range(H):
                s2s = s2_ref[h, pl.ds(b0, SUB), ls]
                e2s = e2_ref[h, pl.ds(b0, SUB), ls]
                for r in range(n_a):
                    t = jnp.where(s2s >= rowb_ref[h, r, 0, :, ls], e2s * rowb_ref[h, r, 1, :, ls], 0.0)
                    w[r] = t if w[r] is None else w[r] + t
            for r in range(n_a):
                w_ref[r, pl.ds(b0, SUB), ls] = w[r]
        return carry

    lax.fori_loop(0, K // SUB, slab, 0, unroll=4)

    total = None
    for c in range(te // PEER_ROWS):
        rows = slice(c * PEER_ROWS, (c + 1) * PEER_ROWS)
        x = g_ref[rows, :]
        th = jnp.tanh(x * (GELU_C0 + GELU_C1 * (x * x)))
        w = w_ref[c * (PEER_ROWS // K):(c + 1) * (PEER_ROWS // K)].reshape(PEER_ROWS, tm)
        wa = (w * (x + x * th)).astype(BF16)
        d = jnp.dot(vt_ref[:, rows], wa, preferred_element_type=F32)
        total = d if total is None else total + d
    acc_ref[...] += total

    @pl.when(j == pl.num_programs(1) - 1)
    def _():
        o_ref[...] = x_ref[...] + acc_ref[...].T


def _peer_dense(h2, u, vt, s1, s2, tau, cc, v2, x1):
    T, D = h2.shape
    E = u.shape[0]
    H, K = PEER_HEADS, PEER_KEYS
    tm, te = 512, 1024
    return pl.pallas_call(
        _peer_dense_kernel,
        grid=(T // tm, E // te),
        in_specs=[pl.BlockSpec((tm, D), lambda i, j: (i, 0)),
                  pl.BlockSpec((te, D), lambda i, j: (j, 0)),
                  pl.BlockSpec((D, te), lambda i, j: (0, j)),
                  pl.BlockSpec((H, K, tm), lambda i, j: (0, 0, i)),
                  pl.BlockSpec((H, K, tm), lambda i, j: (0, 0, i)),
                  pl.BlockSpec((H, 1, tm), lambda i, j: (0, 0, i)),
                  pl.BlockSpec((H, 1, tm), lambda i, j: (0, 0, i)),
                  pl.BlockSpec((H, PEER_TOPK, tm), lambda i, j: (0, 0, i)),
                  pl.BlockSpec((tm, D), lambda i, j: (i, 0))],
        out_specs=pl.BlockSpec((tm, D), lambda i, j: (i, 0)),
        out_shape=jax.ShapeDtypeStruct((T, D), F32),
        scratch_shapes=[pltpu.VMEM((H, K, tm), F32),
                        pltpu.VMEM((H, K, tm), F32),
                        pltpu.VMEM((H, K, tm), F32),
                        pltpu.VMEM((H, te // K, 2, 8, tm + LANE), F32),
                        pltpu.VMEM((te, tm), F32),
                        pltpu.VMEM((te // K, K, tm), F32),
                        pltpu.VMEM((D, tm), F32)],
        compiler_params=_params("parallel", "arbitrary"),
        name="peer_dense",
    )(h2, u, vt, s1, s2, tau, cc, v2, x1)


def _rope_tables(S):
    inv = 1.0 / (ROPE_THETA ** (jnp.arange(0, LANE, 2, dtype=F32) / LANE))
    ang = jnp.arange(S, dtype=F32)[:, None] * inv[None, :]
    cos, sin = jnp.cos(ang), jnp.sin(ang)
    return jnp.concatenate([cos, cos], axis=1), jnp.concatenate([-sin, sin], axis=1)


def _layer(x, mem, attn_norm_g, mem_norm_g, ffn_norm_g, w_in, moba_q_norm_g, moba_k_norm_g,
           dn_conv_w, dn_a_log, dn_dt_bias, dn_out_norm_g, w_mem_kv, mem_q_norm_g, mem_k_norm_g,
           w_branch, w_out, peer_w_q, peer_keys1, peer_keys2, peer_u, peer_v):
    B, S, D = x.shape
    T = B * S
    x2 = x.reshape(T, D)
    row = lambda g: g.reshape(1, -1)

    moba_w, dn_w = 3 * MOBA_HEADS * LANE, 3 * DN_HEADS * LANE
    n_small = 2 * DN_HEADS
    o_small = moba_w + dn_w
    qk_w = 2 * MOBA_HEADS * LANE
    o_z = o_small + n_small
    w_f32 = jnp.concatenate([w_in[:, :qk_w], w_in[:, moba_w:o_small], w_in[:, o_z + D:o_z + 2 * D]],
                            axis=1).astype(BF16)
    w_b16 = jnp.concatenate([w_in[:, qk_w:moba_w], w_in[:, o_z:o_z + D], w_in[:, o_z + 2 * D:]],
                            axis=1).astype(BF16)
    w_small = jnp.pad(w_in[:, o_small:o_z], ((0, 0), (0, LANE - n_small)))
    proj = _norm_matmul(x2, row(attn_norm_g), w_f32, tm=1024, tn=1024, name="in_proj")
    projb = _norm_matmul(x2, row(attn_norm_g), w_b16, tm=1024, tn=1024, out_dtype=BF16, name="in_proj_bf16")
    small = _norm_matmul(x2, row(attn_norm_g), w_small, tm=1024, tn=LANE, precision=HI, name="in_proj_scalars")
    proj3 = proj.reshape(B, S, -1)
    projb3 = projb.reshape(B, S, -1)
    dn_col = qk_w // LANE
    memq_col = (qk_w + dn_w) // MEM_HEAD_DIM
    z_col = 1
    gate_col = 2

    cos2, sin2 = _rope_tables(S)
    mq, mk, kmean = _moba_prep(proj3, row(moba_q_norm_g), row(moba_k_norm_g), cos2, sin2)
    moba_out = _moba_attn(_moba_gate(mq, kmean), mk, projb3)

    dn_qkv = _dn_prep(proj3, dn_conv_w, dn_col)
    braw = small[:, :DN_HEADS].reshape(B, S, DN_HEADS)
    araw = small[:, DN_HEADS:n_small].reshape(B, S, DN_HEADS)
    araw_t = araw.reshape(B, S // DN_CHUNK, DN_CHUNK, DN_HEADS).transpose(0, 1, 3, 2)
    dn_out = _dn_scan(dn_qkv, braw, araw, araw_t, dn_a_log, dn_dt_bias, projb3, z_col, row(dn_out_norm_g))

    M = mem.shape[1]
    kv = _norm_matmul(mem.reshape(B * M, D), row(mem_norm_g), w_mem_kv.astype(BF16), tm=B * M, tn=512, name="mem_kv")
    mem_out = _mem_attn(proj3, kv.reshape(B, M, -1), row(mem_q_norm_g), row(mem_k_norm_g), memq_col)

    x1 = _merge(x2, moba_out.reshape(T, D), dn_out.reshape(T, D), mem_out.reshape(T, D), projb, gate_col,
                w_branch.astype(BF16), w_out.astype(BF16))

    qp, h2 = _norm_matmul(x1, row(ffn_norm_g), peer_w_q.astype(BF16), tm=1024, tn=1024, emit_h=True, name="peer_query")
    s1, s2, tau, cc, v2 = _peer_select(qp, peer_keys1, peer_keys2)
    out = _peer_dense(h2, peer_u.astype(BF16), peer_v.T.astype(BF16), s1, s2, tau, cc, v2, x1)
    return out.reshape(B, S, D)


def kernel(x, mem, attn_norm_g, mem_norm_g, ffn_norm_g, w_in, moba_q_norm_g, moba_k_norm_g, dn_conv_w, dn_a_log, dn_dt_bias, dn_out_norm_g, w_mem_kv, mem_q_norm_g, mem_k_norm_g, w_branch, w_out, peer_w_q, peer_keys1, peer_keys2, peer_u, peer_v):
    for l in range(w_in.shape[0]):
        x = _layer(x, mem, attn_norm_g[l], mem_norm_g[l], ffn_norm_g[l], w_in[l], moba_q_norm_g[l],
                   moba_k_norm_g[l], dn_conv_w[l], dn_a_log[l], dn_dt_bias[l], dn_out_norm_g[l],
                   w_mem_kv[l], mem_q_norm_g[l], mem_k_norm_g[l], w_branch[l], w_out[l], peer_w_q[l],
                   peer_keys1[l], peer_keys2[l], peer_u[l], peer_v[l])
    return x
```

```python
import functools
import math

import jax
import jax.numpy as jnp
import numpy as np
from jax import lax
from jax.experimental import pallas as pl
from jax.experimental.pallas import tpu as pltpu

F32 = jnp.float32
BF16 = jnp.bfloat16
HI = lax.Precision.HIGHEST
EPS = 1e-6
ROPE_THETA = 10000.0
NT = (((1,), (1,)), ((), ()))
TN = (((0,), (0,)), ((), ()))

LANE = 128
MOBA_HEADS = 8
MOBA_BLOCK = 256
MOBA_TOPK = 3
DN_HEADS = 8
DN_CHUNK = 64
DN_CONV = 4
MEM_HEADS = 4
MEM_HEAD_DIM = 256
PEER_HEADS = 8
PEER_KEYS = 128
PEER_TOPK = 16

VMEM_LIMIT = 56 * 1024 * 1024


def _params(*sem):
    return pltpu.CompilerParams(dimension_semantics=sem, vmem_limit_bytes=VMEM_LIMIT)


def _rms(x, g):
    ms = jnp.mean(x * x, axis=-1, keepdims=True)
    return x * lax.rsqrt(ms + EPS) * g


def _split_bf16(a):
    hi = a.astype(BF16)
    return hi, (a - hi.astype(F32)).astype(BF16)


def _norm_matmul_kernel(*refs, precision, emit_h, with_side):
    x_ref, g_ref, w_ref = refs[:3]
    refs = list(refs[3:])
    ws_ref = refs.pop(0) if with_side else None
    o_ref = refs.pop(0)
    ho_ref = refs.pop(0) if emit_h else None
    so_ref = refs.pop(0) if with_side else None
    (h_ref,) = refs

    @pl.when(pl.program_id(1) == 0)
    def _():
        h = _rms(x_ref[...], g_ref[...])
        h_ref[...] = h.astype(h_ref.dtype)
        if emit_h:
            ho_ref[...] = h.astype(ho_ref.dtype)
        if with_side:
            hh, hl = _split_bf16(h)
            ws = ws_ref[...]
            n = ws.shape[1] // 2
            r = jnp.dot(hh, ws, preferred_element_type=F32)
            so_ref[...] = r[:, :n] + r[:, n:] + jnp.dot(hl, ws[:, :n], preferred_element_type=F32)

    o_ref[...] = jnp.dot(h_ref[...], w_ref[...], precision=precision,
                         preferred_element_type=F32).astype(o_ref.dtype)


def _norm_matmul(x, g, w, *, tm, tn, name, out_dtype=F32, precision=None, emit_h=False, w_side=None):
    T, D = x.shape
    N = w.shape[1]
    with_side = w_side is not None
    operands = [x, g, w]
    in_specs = [pl.BlockSpec((tm, D), lambda i, j: (i, 0)),
                pl.BlockSpec((1, D), lambda i, j: (0, 0)),
                pl.BlockSpec((D, tn), lambda i, j: (0, j))]
    out_shape = [jax.ShapeDtypeStruct((T, N), out_dtype)]
    out_specs = [pl.BlockSpec((tm, tn), lambda i, j: (i, j))]
    if with_side:
        operands.append(jnp.concatenate(_split_bf16(w_side), axis=1))
        in_specs.append(pl.BlockSpec((D, 2 * w_side.shape[1]), lambda i, j: (0, 0)))
    if emit_h:
        out_shape.append(jax.ShapeDtypeStruct((T, D), BF16))
        out_specs.append(pl.BlockSpec((tm, D), lambda i, j: (i, 0)))
    if with_side:
        out_shape.append(jax.ShapeDtypeStruct((T, w_side.shape[1]), F32))
        out_specs.append(pl.BlockSpec((tm, w_side.shape[1]), lambda i, j: (i, 0)))
    res = pl.pallas_call(
        functools.partial(_norm_matmul_kernel, precision=precision, emit_h=emit_h, with_side=with_side),
        grid=(T // tm, N // tn),
        in_specs=in_specs,
        out_specs=out_specs,
        out_shape=out_shape,
        scratch_shapes=[pltpu.VMEM((tm, D), w.dtype)],
        compiler_params=_params("parallel", "arbitrary"),
        name=name,
    )(*operands)
    return res if len(res) > 1 else res[0]


MOBA_NEG = -1e30


def _moba_prep_kernel(q_ref, k_ref, gq_ref, gk_ref, cos_ref, sin_ref, qo_ref, ko_ref, km_ref):
    cos = cos_ref[...]
    sin = sin_ref[...]

    def norm_rope(x, g):
        y = _rms(x, g)
        return y * cos + pltpu.roll(y, LANE // 2, 1) * sin

    q = norm_rope(q_ref[0], gq_ref[...])
    k = norm_rope(k_ref[0], gk_ref[...])
    ts = k.shape[0]
    nb = ts // MOBA_BLOCK
    qo_ref[0] = q
    row = lax.broadcasted_iota(jnp.int32, (ts, LANE), 0) + pl.program_id(2) * ts
    lane = lax.broadcasted_iota(jnp.int32, (ts, LANE), 1)
    ko_ref[0, :, 0:LANE] = k.astype(BF16)
    ko_ref[0, :, LANE:2 * LANE] = jnp.where(lane == row // MOBA_BLOCK, 1.0, 0.0).astype(BF16)
    km_ref[0, 0] = jnp.mean(k.reshape(nb, MOBA_BLOCK, LANE), axis=1)


def _moba_prep(proj3, gq, gk, cos2, sin2):
    B, S, _ = proj3.shape
    H = MOBA_HEADS
    ts = 2048
    nb_t = ts // MOBA_BLOCK
    col = lambda off: pl.BlockSpec((1, ts, LANE), lambda b, h, s: (b, s, off + h))
    return pl.pallas_call(
        _moba_prep_kernel,
        grid=(B, H, S // ts),
        in_specs=[col(0), col(H),
                  pl.BlockSpec((1, LANE), lambda b, h, s: (0, 0)),
                  pl.BlockSpec((1, LANE), lambda b, h, s: (0, 0)),
                  pl.BlockSpec((ts, LANE), lambda b, h, s: (s, 0)),
                  pl.BlockSpec((ts, LANE), lambda b, h, s: (s, 0))],
        out_specs=[col(0),
                   pl.BlockSpec((1, ts, 2 * LANE), lambda b, h, s: (b, s, h)),
                   pl.BlockSpec((1, 1, nb_t, LANE), lambda b, h, s: (b, h, s, 0))],
        out_shape=[jax.ShapeDtypeStruct((B, S, H * LANE), F32),
                   jax.ShapeDtypeStruct((B, S, H * 2 * LANE), BF16),
                   jax.ShapeDtypeStruct((B, H, S // MOBA_BLOCK, LANE), F32)],
        compiler_params=_params("parallel", "parallel", "parallel"),
        name="moba_prep",
    )(proj3, proj3, gq, gk, cos2, sin2)


def _moba_gate_kernel(q_ref, km_ref, o_ref):
    L = MOBA_BLOCK
    q = q_ref[0]
    km = km_ref[0, 0]
    nb = km.shape[0]
    S = q.shape[0]
    gate = lax.dot_general(km, q, NT, precision=HI, preferred_element_type=F32)
    row = lax.broadcasted_iota(jnp.int32, (nb, S), 0)
    own = lax.broadcasted_iota(jnp.int32, (nb, S), 1) // L
    rank = jnp.zeros((nb, S), jnp.int32)
    for m in range(nb - 1):
        gm = gate[m:m + 1, :]
        cnt = jnp.where(row > m, jnp.where(gm >= gate, 1, 0), jnp.where(gm > gate, 1, 0))
        rank = rank + jnp.where(own > m, cnt, 0)
    keep = jnp.where(row < own, jnp.where(rank < MOBA_TOPK, 1.0, 0.0), jnp.where(row == own, 1.0, 0.0))
    eye = (lax.broadcasted_iota(jnp.int32, (nb, LANE), 0)
           == lax.broadcasted_iota(jnp.int32, (nb, LANE), 1)).astype(BF16)
    keep_t = lax.dot_general(keep.astype(BF16), eye, TN, preferred_element_type=F32)
    o_ref[0, :, 0:LANE] = (q * (LANE ** -0.5 * math.log2(math.e))).astype(BF16)
    o_ref[0, :, LANE:2 * LANE] = jnp.where(keep_t > 0.5, 0.0, MOBA_NEG).astype(BF16)


def _moba_gate(q, kmean):
    B, S, W = q.shape
    H = MOBA_HEADS
    nb = kmean.shape[2]
    return pl.pallas_call(
        _moba_gate_kernel,
        grid=(B, H),
        in_specs=[pl.BlockSpec((1, S, LANE), lambda b, h: (b, 0, h)),
                  pl.BlockSpec((1, 1, nb, LANE), lambda b, h: (b, h, 0, 0))],
        out_specs=pl.BlockSpec((1, S, 2 * LANE), lambda b, h: (b, 0, h)),
        out_shape=jax.ShapeDtypeStruct((B, S, 2 * W), BF16),
        compiler_params=_params("parallel", "parallel"),
        name="moba_gate",
    )(q, kmean)


def _moba_attn_kernel(qa_ref, qb_ref, k_ref, v_ref, oa_ref, ob_ref, qaug_ref, s_ref):
    L = MOBA_BLOCK
    nb = k_ref.shape[1] // L
    half = nb // 2
    p = pl.program_id(2)
    tiles = (p, nb - 1 - p)
    qaug_ref[0] = qa_ref[0]
    qaug_ref[1] = qb_ref[0]

    def scores(qa, n):
        kb = k_ref[0, pl.ds(pl.multiple_of(n * L, L), L), :]
        return lax.dot_general(qa, kb, NT, preferred_element_type=F32)

    def half_max(s):
        return jnp.maximum(s[:, :LANE], s[:, LANE:])

    def dyn_slot(j):
        first = j < p
        return first, jnp.where(first, 0, 1), jnp.where(first, j, half + j - p)

    r = lax.broadcasted_iota(jnp.int32, (L, L), 0)
    c = lax.broadcasted_iota(jnp.int32, (L, L), 1)
    causal = jnp.where(c <= r, 0.0, MOBA_NEG)

    m = [None, None]
    for t in range(2):
        s = scores(qaug_ref[t], tiles[t]) + causal
        s_ref[nb - 1 + t] = s
        m[t] = half_max(s)
    for n in range(half):
        s = scores(qaug_ref[1], n)
        s_ref[n] = s
        m[1] = jnp.maximum(m[1], half_max(s))
    for j in range(half - 1):
        first, t, n = dyn_slot(j)
        s = scores(qaug_ref[t], n)
        s_ref[half + j] = s
        hm = half_max(s)
        m[0] = jnp.maximum(m[0], jnp.where(first, hm, MOBA_NEG))
        m[1] = jnp.maximum(m[1], jnp.where(first, MOBA_NEG, hm))
    mb = [jnp.broadcast_to(jnp.max(m[t], axis=-1, keepdims=True), (L, LANE)) for t in range(2)]

    def probs(k, mbt):
        s = s_ref[k]
        return jnp.concatenate([jnp.exp2(s[:, :LANE] - mbt), jnp.exp2(s[:, LANE:] - mbt)], axis=1)

    def value(n):
        return v_ref[0, pl.ds(pl.multiple_of(n * L, L), L), :]

    l = [None, None]
    acc = [None, None]
    for t in range(2):
        pr = probs(nb - 1 + t, mb[t])
        l[t] = pr[:, :LANE] + pr[:, LANE:]
        acc[t] = jnp.dot(pr.astype(BF16), value(tiles[t]), preferred_element_type=F32)
    for n in range(half):
        pr = probs(n, mb[1])
        l[1] = l[1] + (pr[:, :LANE] + pr[:, LANE:])
        acc[1] = acc[1] + jnp.dot(pr.astype(BF16), value(n), preferred_element_type=F32)
    for j in range(half - 1):
        first, t, n = dyn_slot(j)
        pr = probs(half + j, jnp.where(first, mb[0], mb[1]))
        ps = pr[:, :LANE] + pr[:, LANE:]
        d = jnp.dot(pr.astype(BF16), value(n), preferred_element_type=F32)
        l[0] = l[0] + jnp.where(first, ps, 0.0)
        l[1] = l[1] + jnp.where(first, 0.0, ps)
        acc[0] = acc[0] + jnp.where(first, d, 0.0)
        acc[1] = acc[1] + jnp.where(first, 0.0, d)

    for t, o_ref in enumerate((oa_ref, ob_ref)):
        o_ref[0] = (acc[t] / jnp.sum(l[t], axis=-1, keepdims=True)).astype(o_ref.dtype)


def _moba_attn(q_aug, k_aug, v):
    B, S, _ = v.shape
    H = MOBA_HEADS
    W = H * LANE
    L = MOBA_BLOCK
    nb = S // L
    half = nb // 2
    lo, hi = pl.pallas_call(
        _moba_attn_kernel,
        grid=(B, H, half),
        in_specs=[pl.BlockSpec((1, L, 2 * LANE), lambda b, h, p: (b, p, h)),
                  pl.BlockSpec((1, L, 2 * LANE), lambda b, h, p: (b, nb - 1 - p, h)),
                  pl.BlockSpec((1, S, 2 * LANE), lambda b, h, p: (b, 0, h)),
                  pl.BlockSpec((1, S, LANE), lambda b, h, p: (b, 0, h))],
        out_specs=[pl.BlockSpec((1, L, LANE), lambda b, h, p: (b, p, h)),
                   pl.BlockSpec((1, L, LANE), lambda b, h, p: (b, half - 1 - p, h))],
        out_shape=[jax.ShapeDtypeStruct((B, S // 2, W), BF16), jax.ShapeDtypeStruct((B, S // 2, W), BF16)],
        scratch_shapes=[pltpu.VMEM((2, L, 2 * LANE), BF16),
                        pltpu.VMEM((nb + 1, L, L), F32)],
        compiler_params=_params("parallel", "parallel", "arbitrary"),
        name="moba_attn",
    )(q_aug, q_aug, k_aug, v)
    return jnp.concatenate([lo, hi], axis=1)


def _mem_attn_kernel(q_ref, k_ref, v_ref, gq_ref, gk_ref, o_ref):
    cq = _rms(q_ref[0], gq_ref[...]).astype(BF16)
    ck = _rms(k_ref[0], gk_ref[...]).astype(BF16)
    s = lax.dot_general(cq, ck, NT, preferred_element_type=F32) * (MEM_HEAD_DIM ** -0.5)
    m = jnp.max(s, axis=-1, keepdims=True)
    p = jnp.exp(s - m)
    l = jnp.sum(p, axis=-1, keepdims=True)
    o = jnp.dot(p.astype(BF16), v_ref[0].astype(BF16), preferred_element_type=F32) / l
    o_ref[0] = o.astype(o_ref.dtype)


def _mem_attn(proj3, kv3, gq, gk, q_col):
    B, S, _ = proj3.shape
    M = kv3.shape[1]
    hd = MEM_HEAD_DIM
    tq = 1024
    return pl.pallas_call(
        _mem_attn_kernel,
        grid=(B, MEM_HEADS, S // tq),
        in_specs=[pl.BlockSpec((1, tq, hd), lambda b, h, i: (b, i, q_col + h)),
                  pl.BlockSpec((1, M, hd), lambda b, h, i: (b, 0, h)),
                  pl.BlockSpec((1, M, hd), lambda b, h, i: (b, 0, MEM_HEADS + h)),
                  pl.BlockSpec((1, hd), lambda b, h, i: (0, 0)),
                  pl.BlockSpec((1, hd), lambda b, h, i: (0, 0))],
        out_specs=pl.BlockSpec((1, tq, hd), lambda b, h, i: (b, i, h)),
        out_shape=jax.ShapeDtypeStruct((B, S, MEM_HEADS * hd), BF16),
        compiler_params=_params("parallel", "parallel", "parallel"),
        name="mem_attn",
    )(proj3, kv3, kv3, gq, gk)


def _dn_prep_kernel(x_ref, w_ref, o_ref, pad_ref):
    S = x_ref.shape[1]
    cb = pl.program_id(1)
    x = x_ref[0]
    pad_ref[0:8, :] = jnp.zeros((8, LANE), F32)
    pad_ref[8:, :] = x
    w = w_ref[...]
    y = w[DN_CONV - 1:DN_CONV, :] * x
    for j in range(DN_CONV - 1):
        off = 8 - (DN_CONV - 1) + j
        y = y + w[j:j + 1, :] * pad_ref[off:off + S, :]
    y = y * jax.nn.sigmoid(y)
    nrm = lax.rsqrt(jnp.sum(y * y, axis=-1, keepdims=True) + EPS)
    scale = jnp.where(cb < DN_HEADS, nrm * (LANE ** -0.5), jnp.where(cb < 2 * DN_HEADS, nrm, 1.0))
    o_ref[0] = y * scale


def _dn_prep(proj3, conv_w, col0):
    B, S, _ = proj3.shape
    ncb = 3 * DN_HEADS
    return pl.pallas_call(
        _dn_prep_kernel,
        grid=(B, ncb),
        in_specs=[pl.BlockSpec((1, S, LANE), lambda b, c: (b, 0, col0 + c)),
                  pl.BlockSpec((DN_CONV, LANE), lambda b, c: (0, c))],
        out_specs=pl.BlockSpec((1, S, LANE), lambda b, c: (b, 0, c)),
        out_shape=jax.ShapeDtypeStruct((B, S, ncb * LANE), F32),
        scratch_shapes=[pltpu.VMEM((S + 8, LANE), F32)],
        compiler_params=_params("parallel", "parallel"),
        name="dn_prep",
    )(proj3, conv_w)


def _softplus(x):
    return jnp.maximum(x, 0.0) + jnp.log1p(jnp.exp(-jnp.abs(x)))


def _mm_split3(a, b):
    ah, al = _split_bf16(a)
    bh, bl = _split_bf16(b)
    return jnp.dot(jnp.concatenate([ah, ah, al], axis=1), jnp.concatenate([bh, bl, bh], axis=0),
                   preferred_element_type=F32)


def _dn_scan_kernel(q_ref, k_ref, v_ref, b_ref, a_ref, at_ref, alr_ref, dtr_ref, alc_ref, dtc_ref,
                    z_ref, g_ref, o_ref, st_ref):
    C = DN_CHUNK
    H = DN_HEADS
    R = q_ref.shape[0]

    @pl.when(pl.program_id(1) == 0)
    def _():
        st_ref[...] = jnp.zeros_like(st_ref)

    ii = lax.broadcasted_iota(jnp.int32, (C, C), 0)
    jj = lax.broadcasted_iota(jnp.int32, (C, C), 1)
    incl = ii >= jj
    strict = ii > jj
    tril = incl.astype(F32)
    triu = (ii <= jj).astype(F32)
    gn = g_ref[...]

    gc_col, gc_row, beta = [], [], []
    for r in range(R):
        g_col = -jnp.exp(alr_ref[...]) * _softplus(a_ref[r] + dtr_ref[...])
        g_row = -jnp.exp(alc_ref[...]) * _softplus(at_ref[r, 0] + dtc_ref[...])
        gc_col.append(jnp.dot(tril, g_col, precision=HI, preferred_element_type=F32))
        gc_row.append(jnp.dot(g_row, triu, precision=HI, preferred_element_type=F32))
        beta.append(jax.nn.sigmoid(b_ref[r]))

    def mm(a, b):
        return jnp.dot(a.astype(BF16), b.astype(BF16), preferred_element_type=F32)

    units = [(r, h) for r in range(R) for h in range(H)]
    U = range(len(units))
    sl = [slice(h * LANE, (h + 1) * LANE) for _, h in units]
    q = [q_ref[r, :, sl[u]] for u, (r, h) in enumerate(units)]
    k = [k_ref[r, :, sl[u]] for u, (r, h) in enumerate(units)]
    gcc = [gc_col[r][:, h:h + 1] for r, h in units]
    bet = [beta[r][:, h:h + 1] for r, h in units]
    decay = [jnp.where(incl, jnp.exp(jnp.where(incl, gcc[u] - gc_row[r][h:h + 1, :], 0.0)), 0.0)
             for u, (r, h) in enumerate(units)]
    kb = [k[u] * bet[u] for u in U]
    vb = [v_ref[r, :, sl[u]] * bet[u] for u, (r, h) in enumerate(units)]
    kbf = [k[u].astype(BF16) for u in U]
    kq = [lax.dot_general(jnp.concatenate([kb[u], q[u]], axis=0).astype(BF16), kbf[u], NT,
                          preferred_element_type=F32) for u in U]
    kk = [kq[u][:C] for u in U]
    qk = [kq[u][C:] for u in U]
    n = [jnp.where(strict, kk[u] * decay[u], 0.0) for u in U]
    attn = [jnp.where(incl, qk[u] * decay[u], 0.0) for u in U]
    egc = [jnp.exp(gcc[u]) for u in U]
    x = [jnp.concatenate([vb[u], kb[u] * egc[u]], axis=1) for u in U]
    nx = [_mm_split3(n[u], x[u]) for u in U]
    p = [_mm_split3(n[u], n[u]) for u in U]
    x = [x[u] - nx[u] for u in U]
    levels = int(math.log2(C)) - 1
    for lvl in range(levels):
        px = [_mm_split3(p[u], x[u]) for u in U]
        if lvl + 1 < levels:
            p = [_mm_split3(p[u], p[u]) for u in U]
        x = [x[u] + px[u] for u in U]
    s = [st_ref[u] for u in U]
    sb = [s[u].astype(BF16) for u in U]
    wq = [mm(jnp.concatenate([x[u][:, LANE:], q[u] * egc[u]], axis=0), sb[u]) for u in U]
    ws = [wq[u][:C] for u in U]
    qs = [wq[u][C:] for u in U]
    v_new = [(x[u][:, :LANE] - ws[u]).astype(BF16) for u in U]
    av = [mm(attn[u], v_new[u]) for u in U]
    g_last = [gcc[u][C - 1:C, :] for u in U]
    kd = [(k[u] * jnp.exp(g_last[u] - gcc[u])).astype(BF16) for u in U]
    kv = [lax.dot_general(kd[u], v_new[u], TN, preferred_element_type=F32) for u in U]
    for u, (r, h) in enumerate(units):
        st_ref[u] = s[u] * jnp.exp(g_last[u]) + kv[u]
        zz = z_ref[r, :, sl[u]].astype(F32)
        o_ref[r, :, sl[u]] = (_rms(qs[u] + av[u], gn) * (zz * jax.nn.sigmoid(zz))).astype(o_ref.dtype)


DN_ROWS = 2


def _dn_scan(qkv, braw, araw, araw_t, a_log, dt_bias, proj3, z_col, onorm_g):
    B, S, _ = qkv.shape
    H = DN_HEADS
    C = DN_CHUNK
    W = H * LANE
    R = DN_ROWS
    tile = lambda col: pl.BlockSpec((R, C, W), lambda b, c: (b, c, col))
    small = lambda shape: pl.BlockSpec(shape, lambda b, c: (0,) * len(shape))
    return pl.pallas_call(
        _dn_scan_kernel,
        grid=(B // R, S // C),
        in_specs=[tile(0), tile(1), tile(2),
                  pl.BlockSpec((R, C, H), lambda b, c: (b, c, 0)),
                  pl.BlockSpec((R, C, H), lambda b, c: (b, c, 0)),
                  pl.BlockSpec((R, 1, H, C), lambda b, c: (b, c, 0, 0)),
                  small((1, H)), small((1, H)), small((H, 1)), small((H, 1)),
                  tile(z_col), small((1, LANE))],
        out_specs=tile(0),
        out_shape=jax.ShapeDtypeStruct((B, S, W), BF16),
        scratch_shapes=[pltpu.VMEM((R * H, LANE, LANE), F32)],
        compiler_params=_params("parallel", "arbitrary"),
        name="dn_scan",
    )(qkv, qkv, qkv, braw, araw, araw_t, a_log.reshape(1, H), dt_bias.reshape(1, H),
      a_log.reshape(H, 1), dt_bias.reshape(H, 1), proj3, onorm_g)


def _merge_kernel(x_ref, b0_ref, b1_ref, b2_ref, g0_ref, g1_ref, g2_ref, wb_ref, wo_ref, o_ref):
    mixed = None
    for i, (b_ref, g_ref) in enumerate(((b0_ref, g0_ref), (b1_ref, g1_ref), (b2_ref, g2_ref))):
        bp = jnp.dot(b_ref[...], wb_ref[i], preferred_element_type=F32)
        t = jax.nn.sigmoid(g_ref[...].astype(F32)) * bp
        mixed = t if mixed is None else mixed + t
    o_ref[...] = x_ref[...] + jnp.dot(mixed.astype(BF16), wo_ref[...], preferred_element_type=F32)


def _merge(x2, moba_out, dn_out, mem_out, proj, gate_col, w_branch, w_out):
    T, D = x2.shape
    tm = 256
    row = lambda col: pl.BlockSpec((tm, D), lambda i: (i, col))
    return pl.pallas_call(
        _merge_kernel,
        grid=(T // tm,),
        in_specs=[row(0), row(0), row(0), row(0), row(gate_col), row(gate_col + 1), row(gate_col + 2),
                  pl.BlockSpec((3, D, D), lambda i: (0, 0, 0)),
                  pl.BlockSpec((D, D), lambda i: (0, 0))],
        out_specs=row(0),
        out_shape=jax.ShapeDtypeStruct((T, D), F32),
        compiler_params=_params("parallel"),
        name="merge",
    )(x2, moba_out, dn_out, mem_out, proj, proj, proj, w_branch, w_out)


def _compare_exchange(a, b):
    if a is None:
        return b, None
    if b is None:
        return a, None
    return jnp.maximum(a, b), jnp.minimum(a, b)


def _bitonic_merge_desc(xs):
    n = len(xs)
    j = n // 2
    while j >= 1:
        for i in range(n):
            l = i ^ j
            if l > i:
                xs[i], xs[l] = _compare_exchange(xs[i], xs[l])
        j //= 2
    return xs


def _bitonic_sort_desc(xs):
    xs = list(xs)
    n = len(xs)
    k = 2
    while k <= n:
        j = k // 2
        while j >= 1:
            for i in range(n):
                l = i ^ j
                if l > i:
                    hi, lo = _compare_exchange(xs[i], xs[l])
                    xs[i], xs[l] = (hi, lo) if (i & k) == 0 else (lo, hi)
            j //= 2
        k *= 2
    return xs


def _top16_over_rows(pieces):
    K = PEER_TOPK
    xs = _bitonic_sort_desc(list(pieces) + [None] * (K - len(pieces)))
    for shift in (4, 2, 1):
        other = [None if x is None else pltpu.roll(x, shift, 0) for x in xs]
        merged = []
        for i in range(K):
            a, b = xs[i], other[K - 1 - i]
            merged.append(b if a is None else a if b is None else jnp.maximum(a, b))
        xs = _bitonic_merge_desc(merged)
    return xs


def _rows_from_replicated(vals, sub):
    out = vals[0]
    for r in range(1, 8):
        out = jnp.where(sub == r, vals[r], out)
    return out


def _peer_select_kernel(qp_ref, k1_ref, k2_ref, s1_ref, s2_ref, tau_ref, cc_ref, v2_ref):
    tm = qp_ref.shape[0]
    K = PEER_KEYS
    q = qp_ref[...]
    s1_ref[0] = lax.dot_general(k1_ref[...], q[:, :K], NT, precision=HI, preferred_element_type=F32)
    s2_ref[0] = lax.dot_general(k2_ref[...], q[:, K:], NT, precision=HI, preferred_element_type=F32)
    sub = lax.broadcasted_iota(jnp.int32, (8, LANE), 0)
    ninf = -jnp.inf
    for c in range(tm // LANE):
        cs = slice(c * LANE, (c + 1) * LANE)
        v1 = _top16_over_rows([s1_ref[0, 8 * r:8 * r + 8, cs] for r in range(K // 8)])
        v2 = _top16_over_rows([s2_ref[0, 8 * r:8 * r + 8, cs] for r in range(K // 8)])
        v1lo, v1hi = _rows_from_replicated(v1[:8], sub), _rows_from_replicated(v1[8:], sub)
        v2lo, v2hi = _rows_from_replicated(v2[:8], sub), _rows_from_replicated(v2[8:], sub)
        cands = [
            v1[0] + v2lo, v1[0] + v2hi,
            v1[1] + v2lo,
            jnp.where(sub >= 2, v2[0] + v1lo, ninf), v2[0] + v1hi,
            jnp.where(sub >= 2, v2[1] + v1lo, ninf),
            jnp.where((sub >= 2) & (sub <= 4), v1[2] + v2lo, ninf),
            jnp.where((sub >= 2) & (sub <= 3), v1[3] + v2lo, ninf),
            jnp.where(sub == 2, v1[4] + v2lo, ninf),
        ]
        top = _top16_over_rows(cands)
        smax = top[0]
        z = jnp.exp(top[0] - smax)
        for t in top[1:]:
            z = z + jnp.exp(t - smax)
        v2_ref[0, 0:8, cs] = v2lo
        v2_ref[0, 8:16, cs] = v2hi
        tau_ref[0, :, cs] = top[PEER_TOPK - 1][0:1, :]
        cc_ref[0, :, cs] = (smax + jnp.log(z))[0:1, :]


def _peer_select(qp, keys1, keys2):
    T = qp.shape[0]
    H, K = PEER_HEADS, PEER_KEYS
    tm = 512
    return pl.pallas_call(
        _peer_select_kernel,
        grid=(T // tm, H),
        in_specs=[pl.BlockSpec((tm, 2 * K), lambda i, h: (i, h)),
                  pl.BlockSpec((K, K), lambda i, h: (0, 0)),
                  pl.BlockSpec((K, K), lambda i, h: (0, 0))],
        out_specs=[pl.BlockSpec((1, K, tm), lambda i, h: (h, 0, i)),
                   pl.BlockSpec((1, K, tm), lambda i, h: (h, 0, i)),
                   pl.BlockSpec((1, 1, tm), lambda i, h: (h, 0, i)),
                   pl.BlockSpec((1, 1, tm), lambda i, h: (h, 0, i)),
                   pl.BlockSpec((1, PEER_TOPK, tm), lambda i, h: (h, 0, i))],
        out_shape=[jax.ShapeDtypeStruct((H, K, T), F32), jax.ShapeDtypeStruct((H, K, T), F32),
                   jax.ShapeDtypeStruct((H, 1, T), F32), jax.ShapeDtypeStruct((H, 1, T), F32),
                   jax.ShapeDtypeStruct((H, PEER_TOPK, T), F32)],
        compiler_params=_params("parallel", "parallel"),
        name="peer_select",
    )(qp, keys1, keys2)


GELU_C0 = math.sqrt(2.0 / math.pi)
GELU_C1 = GELU_C0 * 0.044715
PEER_ROWS = 256
PEER_LANES = 256


def _peer_dense_kernel(h_ref, u_ref, v_ref, s1_ref, s2_ref, tau_ref, cc_ref, v2_ref, x_ref, o_ref,
                       e1_ref, e2_ref, thr_ref, rowb_ref, g_ref, w_ref, acc_ref):
    K = PEER_KEYS
    H = PEER_HEADS
    SUB = 8
    j = pl.program_id(1)
    te, tm = g_ref.shape
    n_a = te // K

    @pl.when(j == 0)
    def _():
        acc_ref[...] = jnp.zeros_like(acc_ref)
        for h in range(H):
            s1 = s1_ref[h]
            m1 = jnp.max(s1, axis=0, keepdims=True)
            e1_ref[h] = jnp.exp(s1 - m1)
            e2_ref[h] = 0.5 * jnp.exp(s2_ref[h] - (cc_ref[h] - m1))
            tau = tau_ref[h]
            thr = jnp.full((K, tm), jnp.inf, F32)
            for k in range(PEER_TOPK):
                v2k = v2_ref[h, k:k + 1, :]
                thr = jnp.where(s1 + v2k >= tau, v2k, thr)
            thr_ref[h] = thr

    g_ref[...] = lax.dot_general(u_ref[...], h_ref[...], NT, preferred_element_type=F32)

    a0 = pl.multiple_of(j * n_a, n_a)
    for h in range(H):
        thrt = thr_ref[h, pl.ds(a0, n_a), :]
        e1t = e1_ref[h, pl.ds(a0, n_a), :]
        for r in range(n_a):
            rowb_ref[h, r, 0, :, 0:tm] = jnp.broadcast_to(thrt[r:r + 1, :], (SUB, tm))
            rowb_ref[h, r, 1, :, 0:tm] = jnp.broadcast_to(e1t[r:r + 1, :], (SUB, tm))

    def slab(sb, carry):
        b0 = pl.multiple_of(sb * SUB, SUB)
        for l0 in range(0, tm, PEER_LANES):
            ls = slice(l0, l0 + PEER_LANES)
            w = [None] * n_a
            for h in range(H):
                s2s = s2_ref[h, pl.ds(b0, SUB), ls]
                e2s = e2_ref[h, pl.ds(b0, SUB), ls]
                for r in range(n_a):
                    t = jnp.where(s2s >= rowb_ref[h, r, 0, :, ls], e2s * rowb_ref[h, r, 1, :, ls], 0.0)
                    w[r] = t if w[r] is None else w[r] + t
            for r in range(n_a):
                w_ref[r, pl.ds(b0, SUB), ls] = w[r]
        return carry

    lax.fori_loop(0, K // SUB, slab, 0, unroll=4)

    total = None
    for c in range(te // PEER_ROWS):
        rows = slice(c * PEER_ROWS, (c + 1) * PEER_ROWS)
        x = g_ref[rows, :]
        th = jnp.tanh(x * (GELU_C0 + GELU_C1 * (x * x)))
        w = w_ref[c * (PEER_ROWS // K):(c + 1) * (PEER_ROWS // K)].reshape(PEER_ROWS, tm)
        wa = (w * (x + x * th)).astype(BF16)
        d = lax.dot_general(v_ref[rows, :], wa, TN, preferred_element_type=F32)
        total = d if total is None else total + d
    acc_ref[...] += total

    @pl.when(j == pl.num_programs(1) - 1)
    def _():
        o_ref[...] = x_ref[...] + acc_ref[...].T


def _peer_dense(h2, u, v, s1, s2, tau, cc, v2, x1):
    T, D = h2.shape
    E = u.shape[0]
    H, K = PEER_HEADS, PEER_KEYS
    tm, te = 512, 1024
    return pl.pallas_call(
        _peer_dense_kernel,
        grid=(T // tm, E // te),
        in_specs=[pl.BlockSpec((tm, D), lambda i, j: (i, 0)),
                  pl.BlockSpec((te, D), lambda i, j: (j, 0)),
                  pl.BlockSpec((te, D), lambda i, j: (j, 0)),
                  pl.BlockSpec((H, K, tm), lambda i, j: (0, 0, i)),
                  pl.BlockSpec((H, K, tm), lambda i, j: (0, 0, i)),
                  pl.BlockSpec((H, 1, tm), lambda i, j: (0, 0, i)),
                  pl.BlockSpec((H, 1, tm), lambda i, j: (0, 0, i)),
                  pl.BlockSpec((H, PEER_TOPK, tm), lambda i, j: (0, 0, i)),
                  pl.BlockSpec((tm, D), lambda i, j: (i, 0))],
        out_specs=pl.BlockSpec((tm, D), lambda i, j: (i, 0)),
        out_shape=jax.ShapeDtypeStruct((T, D), F32),
        scratch_shapes=[pltpu.VMEM((H, K, tm), F32),
                        pltpu.VMEM((H, K, tm), F32),
                        pltpu.VMEM((H, K, tm), F32),
                        pltpu.VMEM((H, te // K, 2, 8, tm + LANE), F32),
                        pltpu.VMEM((te, tm), F32),
                        pltpu.VMEM((te // K, K, tm), F32),
                        pltpu.VMEM((D, tm), F32)],
        compiler_params=_params("parallel", "arbitrary"),
        name="peer_dense",
    )(h2, u, v, s1, s2, tau, cc, v2, x1)


def _rope_tables(S):
    inv = 1.0 / (ROPE_THETA ** (np.arange(0, LANE, 2, dtype=np.float64) / LANE))
    ang = np.arange(S, dtype=np.float64)[:, None] * inv[None, :]
    cos, sin = np.cos(ang), np.sin(ang)
    return (jnp.asarray(np.concatenate([cos, cos], axis=1), F32),
            jnp.asarray(np.concatenate([-sin, sin], axis=1), F32))


def _layer(x, mem, attn_norm_g, mem_norm_g, ffn_norm_g, w_in, moba_q_norm_g, moba_k_norm_g,
           dn_conv_w, dn_a_log, dn_dt_bias, dn_out_norm_g, w_mem_kv, mem_q_norm_g, mem_k_norm_g,
           w_branch, w_out, peer_w_q, peer_keys1, peer_keys2, peer_u, peer_v):
    B, S, D = x.shape
    T = B * S
    x2 = x.reshape(T, D)
    row = lambda g: g.reshape(1, -1)

    moba_w, dn_w = 3 * MOBA_HEADS * LANE, 3 * DN_HEADS * LANE
    n_small = 2 * DN_HEADS
    o_small = moba_w + dn_w
    qk_w = 2 * MOBA_HEADS * LANE
    o_z = o_small + n_small
    w_f32 = jnp.concatenate([w_in[:, :qk_w], w_in[:, moba_w:o_small], w_in[:, o_z + D:o_z + 2 * D]],
                            axis=1).astype(BF16)
    w_b16 = jnp.concatenate([w_in[:, qk_w:moba_w], w_in[:, o_z:o_z + D], w_in[:, o_z + 2 * D:]],
                            axis=1).astype(BF16)
    w_small = jnp.pad(w_in[:, o_small:o_z], ((0, 0), (0, LANE - n_small)))
    proj, small = _norm_matmul(x2, row(attn_norm_g), w_f32, tm=1024, tn=1024, w_side=w_small, name="in_proj")
    projb = _norm_matmul(x2, row(attn_norm_g), w_b16, tm=1024, tn=1024, out_dtype=BF16, name="in_proj_bf16")
    proj3 = proj.reshape(B, S, -1)
    projb3 = projb.reshape(B, S, -1)
    dn_col = qk_w // LANE
    memq_col = (qk_w + dn_w) // MEM_HEAD_DIM
    z_col = 1
    gate_col = 2

    cos2, sin2 = _rope_tables(S)
    mq, mk, kmean = _moba_prep(proj3, row(moba_q_norm_g), row(moba_k_norm_g), cos2, sin2)
    moba_out = _moba_attn(_moba_gate(mq, kmean), mk, projb3)

    dn_qkv = _dn_prep(proj3, dn_conv_w, dn_col)
    braw = small[:, :DN_HEADS].reshape(B, S, DN_HEADS)
    araw = small[:, DN_HEADS:n_small].reshape(B, S, DN_HEADS)
    araw_t = araw.reshape(B, S // DN_CHUNK, DN_CHUNK, DN_HEADS).transpose(0, 1, 3, 2)
    dn_out = _dn_scan(dn_qkv, braw, araw, araw_t, dn_a_log, dn_dt_bias, projb3, z_col, row(dn_out_norm_g))

    M = mem.shape[1]
    kv = _norm_matmul(mem.reshape(B * M, D), row(mem_norm_g), w_mem_kv.astype(BF16), tm=B * M, tn=512, name="mem_kv")
    mem_out = _mem_attn(proj3, kv.reshape(B, M, -1), row(mem_q_norm_g), row(mem_k_norm_g), memq_col)

    x1 = _merge(x2, moba_out.reshape(T, D), dn_out.reshape(T, D), mem_out.reshape(T, D), projb, gate_col,
                w_branch.astype(BF16), w_out.astype(BF16))

    qp, h2 = _norm_matmul(x1, row(ffn_norm_g), peer_w_q.astype(BF16), tm=1024, tn=1024, emit_h=True, name="peer_query")
    s1, s2, tau, cc, v2 = _peer_select(qp, peer_keys1, peer_keys2)
    out = _peer_dense(h2, peer_u.astype(BF16), peer_v.astype(BF16), s1, s2, tau, cc, v2, x1)
    return out.reshape(B, S, D)


def kernel(x, mem, attn_norm_g, mem_norm_g, ffn_norm_g, w_in, moba_q_norm_g, moba_k_norm_g, dn_conv_w, dn_a_log, dn_dt_bias, dn_out_norm_g, w_mem_kv, mem_q_norm_g, mem_k_norm_g, w_branch, w_out, peer_w_q, peer_keys1, peer_keys2, peer_u, peer_v):
    for l in range(w_in.shape[0]):
        x = _layer(x, mem, attn_norm_g[l], mem_norm_g[l], ffn_norm_g[l], w_in[l], moba_q_norm_g[l],
                   moba_k_norm_g[l], dn_conv_w[l], dn_a_log[l], dn_dt_bias[l], dn_out_norm_g[l],
                   w_mem_kv[l], mem_q_norm_g[l], mem_k_norm_g[l], w_branch[l], w_out[l], peer_w_q[l],
                   peer_keys1[l], peer_keys2[l], peer_u[l], peer_v[l])
    return x
```

```python
import functools
import math

import jax
import jax.numpy as jnp
import numpy as np
from jax import lax
from jax.experimental import pallas as pl
from jax.experimental.pallas import tpu as pltpu

F32 = jnp.float32
BF16 = jnp.bfloat16
HI = lax.Precision.HIGHEST
EPS = 1e-6
ROPE_THETA = 10000.0
NT = (((1,), (1,)), ((), ()))
TN = (((0,), (0,)), ((), ()))

LANE = 128
MOBA_HEADS = 8
MOBA_BLOCK = 256
MOBA_TOPK = 3
DN_HEADS = 8
DN_CHUNK = 64
DN_CONV = 4
MEM_HEADS = 4
MEM_HEAD_DIM = 256
PEER_HEADS = 8
PEER_KEYS = 128
PEER_TOPK = 16

VMEM_LIMIT = 56 * 1024 * 1024


def _params(*sem):
    return pltpu.CompilerParams(dimension_semantics=sem, vmem_limit_bytes=VMEM_LIMIT)


def _rms(x, g):
    ms = jnp.mean(x * x, axis=-1, keepdims=True)
    return x * lax.rsqrt(ms + EPS) * g


def _split_bf16(a):
    hi = a.astype(BF16)
    return hi, (a - hi.astype(F32)).astype(BF16)


def _norm_matmul_kernel(*refs, precision, emit_h, with_side):
    x_ref, g_ref, w_ref = refs[:3]
    refs = list(refs[3:])
    ws_ref = refs.pop(0) if with_side else None
    o_ref = refs.pop(0)
    ho_ref = refs.pop(0) if emit_h else None
    so_ref = refs.pop(0) if with_side else None
    (h_ref,) = refs

    @pl.when(pl.program_id(1) == 0)
    def _():
        h = _rms(x_ref[...], g_ref[...])
        h_ref[...] = h.astype(h_ref.dtype)
        if emit_h:
            ho_ref[...] = h.astype(ho_ref.dtype)
        if with_side:
            hh, hl = _split_bf16(h)
            ws = ws_ref[...]
            n = ws.shape[1] // 2
            r = jnp.dot(hh, ws, preferred_element_type=F32)
            so_ref[...] = r[:, :n] + r[:, n:] + jnp.dot(hl, ws[:, :n], preferred_element_type=F32)

    o_ref[...] = jnp.dot(h_ref[...], w_ref[...], precision=precision,
                         preferred_element_type=F32).astype(o_ref.dtype)


def _norm_matmul(x, g, w, *, tm, tn, name, out_dtype=F32, precision=None, emit_h=False, w_side=None):
    T, D = x.shape
    N = w.shape[1]
    with_side = w_side is not None
    operands = [x, g, w]
    in_specs = [pl.BlockSpec((tm, D), lambda i, j: (i, 0)),
                pl.BlockSpec((1, D), lambda i, j: (0, 0)),
                pl.BlockSpec((D, tn), lambda i, j: (0, j))]
    out_shape = [jax.ShapeDtypeStruct((T, N), out_dtype)]
    out_specs = [pl.BlockSpec((tm, tn), lambda i, j: (i, j))]
    if with_side:
        operands.append(jnp.concatenate(_split_bf16(w_side), axis=1))
        in_specs.append(pl.BlockSpec((D, 2 * w_side.shape[1]), lambda i, j: (0, 0)))
    if emit_h:
        out_shape.append(jax.ShapeDtypeStruct((T, D), BF16))
        out_specs.append(pl.BlockSpec((tm, D), lambda i, j: (i, 0)))
    if with_side:
        out_shape.append(jax.ShapeDtypeStruct((T, w_side.shape[1]), F32))
        out_specs.append(pl.BlockSpec((tm, w_side.shape[1]), lambda i, j: (i, 0)))
    res = pl.pallas_call(
        functools.partial(_norm_matmul_kernel, precision=precision, emit_h=emit_h, with_side=with_side),
        grid=(T // tm, N // tn),
        in_specs=in_specs,
        out_specs=out_specs,
        out_shape=out_shape,
        scratch_shapes=[pltpu.VMEM((tm, D), w.dtype)],
        compiler_params=_params("parallel", "arbitrary"),
        name=name,
    )(*operands)
    return res if len(res) > 1 else res[0]


MOBA_NEG = -1e30


def _moba_prep_kernel(q_ref, k_ref, gq_ref, gk_ref, cos_ref, sin_ref, qo_ref, ko_ref, km_ref):
    cos = cos_ref[...]
    sin = sin_ref[...]

    def norm_rope(x, g):
        y = _rms(x, g)
        return y * cos + pltpu.roll(y, LANE // 2, 1) * sin

    q = norm_rope(q_ref[0], gq_ref[...])
    k = norm_rope(k_ref[0], gk_ref[...])
    ts = k.shape[0]
    nb = ts // MOBA_BLOCK
    qo_ref[0] = q
    row = lax.broadcasted_iota(jnp.int32, (ts, LANE), 0) + pl.program_id(2) * ts
    lane = lax.broadcasted_iota(jnp.int32, (ts, LANE), 1)
    ko_ref[0, :, 0:LANE] = k.astype(BF16)
    ko_ref[0, :, LANE:2 * LANE] = jnp.where(lane == row // MOBA_BLOCK, 1.0, 0.0).astype(BF16)
    km_ref[0, 0] = jnp.mean(k.reshape(nb, MOBA_BLOCK, LANE), axis=1)


def _moba_prep(proj3, gq, gk, cos2, sin2):
    B, S, _ = proj3.shape
    H = MOBA_HEADS
    ts = 2048
    nb_t = ts // MOBA_BLOCK
    col = lambda off: pl.BlockSpec((1, ts, LANE), lambda b, h, s: (b, s, off + h))
    return pl.pallas_call(
        _moba_prep_kernel,
        grid=(B, H, S // ts),
        in_specs=[col(0), col(H),
                  pl.BlockSpec((1, LANE), lambda b, h, s: (0, 0)),
                  pl.BlockSpec((1, LANE), lambda b, h, s: (0, 0)),
                  pl.BlockSpec((ts, LANE), lambda b, h, s: (s, 0)),
                  pl.BlockSpec((ts, LANE), lambda b, h, s: (s, 0))],
        out_specs=[col(0),
                   pl.BlockSpec((1, ts, 2 * LANE), lambda b, h, s: (b, s, h)),
                   pl.BlockSpec((1, 1, nb_t, LANE), lambda b, h, s: (b, h, s, 0))],
        out_shape=[jax.ShapeDtypeStruct((B, S, H * LANE), F32),
                   jax.ShapeDtypeStruct((B, S, H * 2 * LANE), BF16),
                   jax.ShapeDtypeStruct((B, H, S // MOBA_BLOCK, LANE), F32)],
        compiler_params=_params("parallel", "parallel", "parallel"),
        name="moba_prep",
    )(proj3, proj3, gq, gk, cos2, sin2)


def _moba_gate_kernel(q_ref, km_ref, o_ref):
    L = MOBA_BLOCK
    q = q_ref[0]
    km = km_ref[0, 0]
    nb = km.shape[0]
    S = q.shape[0]
    gate = lax.dot_general(km, q, NT, precision=HI, preferred_element_type=F32)
    row = lax.broadcasted_iota(jnp.int32, (nb, S), 0)
    own = lax.broadcasted_iota(jnp.int32, (nb, S), 1) // L
    rank = jnp.zeros((nb, S), jnp.int32)
    for m in range(nb - 1):
        gm = gate[m:m + 1, :]
        cnt = jnp.where(row > m, jnp.where(gm >= gate, 1, 0), jnp.where(gm > gate, 1, 0))
        rank = rank + jnp.where(own > m, cnt, 0)
    keep = jnp.where(row < own, jnp.where(rank < MOBA_TOPK, 1.0, 0.0), jnp.where(row == own, 1.0, 0.0))
    eye = (lax.broadcasted_iota(jnp.int32, (nb, LANE), 0)
           == lax.broadcasted_iota(jnp.int32, (nb, LANE), 1)).astype(BF16)
    keep_t = lax.dot_general(keep.astype(BF16), eye, TN, preferred_element_type=F32)
    o_ref[0, :, 0:LANE] = (q * (LANE ** -0.5 * math.log2(math.e))).astype(BF16)
    o_ref[0, :, LANE:2 * LANE] = jnp.where(keep_t > 0.5, 0.0, MOBA_NEG).astype(BF16)


def _moba_gate(q, kmean):
    B, S, W = q.shape
    H = MOBA_HEADS
    nb = kmean.shape[2]
    return pl.pallas_call(
        _moba_gate_kernel,
        grid=(B, H),
        in_specs=[pl.BlockSpec((1, S, LANE), lambda b, h: (b, 0, h)),
                  pl.BlockSpec((1, 1, nb, LANE), lambda b, h: (b, h, 0, 0))],
        out_specs=pl.BlockSpec((1, S, 2 * LANE), lambda b, h: (b, 0, h)),
        out_shape=jax.ShapeDtypeStruct((B, S, 2 * W), BF16),
        compiler_params=_params("parallel", "parallel"),
        name="moba_gate",
    )(q, kmean)


def _moba_attn_kernel(qa_ref, qb_ref, k_ref, v_ref, oa_ref, ob_ref, qaug_ref, s_ref):
    L = MOBA_BLOCK
    nb = k_ref.shape[1] // L
    half = nb // 2
    p = pl.program_id(2)
    tiles = (p, nb - 1 - p)
    qaug_ref[0] = qa_ref[0]
    qaug_ref[1] = qb_ref[0]

    def scores(qa, n):
        kb = k_ref[0, pl.ds(pl.multiple_of(n * L, L), L), :]
        return lax.dot_general(qa, kb, NT, preferred_element_type=F32)

    def half_max(s):
        return jnp.maximum(s[:, :LANE], s[:, LANE:])

    def dyn_slot(j):
        first = j < p
        return first, jnp.where(first, 0, 1), jnp.where(first, j, half + j - p)

    r = lax.broadcasted_iota(jnp.int32, (L, L), 0)
    c = lax.broadcasted_iota(jnp.int32, (L, L), 1)
    causal = jnp.where(c <= r, 0.0, MOBA_NEG)

    m = [None, None]
    for t in range(2):
        s = scores(qaug_ref[t], tiles[t]) + causal
        s_ref[nb - 1 + t] = s
        m[t] = half_max(s)
    for n in range(half):
        s = scores(qaug_ref[1], n)
        s_ref[n] = s
        m[1] = jnp.maximum(m[1], half_max(s))
    for j in range(half - 1):
        first, t, n = dyn_slot(j)
        s = scores(qaug_ref[t], n)
        s_ref[half + j] = s
        hm = half_max(s)
        m[0] = jnp.maximum(m[0], jnp.where(first, hm, MOBA_NEG))
        m[1] = jnp.maximum(m[1], jnp.where(first, MOBA_NEG, hm))
    mb = [jnp.broadcast_to(jnp.max(m[t], axis=-1, keepdims=True), (L, LANE)) for t in range(2)]

    def probs(k, mbt):
        s = s_ref[k]
        return jnp.concatenate([jnp.exp2(s[:, :LANE] - mbt), jnp.exp2(s[:, LANE:] - mbt)], axis=1)

    def value(n):
        return v_ref[0, pl.ds(pl.multiple_of(n * L, L), L), :]

    l = [None, None]
    acc = [None, None]
    for t in range(2):
        pr = probs(nb - 1 + t, mb[t])
        l[t] = pr[:, :LANE] + pr[:, LANE:]
        acc[t] = jnp.dot(pr.astype(BF16), value(tiles[t]), preferred_element_type=F32)
    for n in range(half):
        pr = probs(n, mb[1])
        l[1] = l[1] + (pr[:, :LANE] + pr[:, LANE:])
        acc[1] = acc[1] + jnp.dot(pr.astype(BF16), value(n), preferred_element_type=F32)
    for j in range(half - 1):
        first, t, n = dyn_slot(j)
        pr = probs(half + j, jnp.where(first, mb[0], mb[1]))
        ps = pr[:, :LANE] + pr[:, LANE:]
        d = jnp.dot(pr.astype(BF16), value(n), preferred_element_type=F32)
        l[0] = l[0] + jnp.where(first, ps, 0.0)
        l[1] = l[1] + jnp.where(first, 0.0, ps)
        acc[0] = acc[0] + jnp.where(first, d, 0.0)
        acc[1] = acc[1] + jnp.where(first, 0.0, d)

    for t, o_ref in enumerate((oa_ref, ob_ref)):
        o_ref[0] = (acc[t] / jnp.sum(l[t], axis=-1, keepdims=True)).astype(o_ref.dtype)


def _moba_attn(q_aug, k_aug, v):
    B, S, _ = v.shape
    H = MOBA_HEADS
    W = H * LANE
    L = MOBA_BLOCK
    nb = S // L
    half = nb // 2
    return pl.pallas_call(
        _moba_attn_kernel,
        grid=(B, H, half),
        in_specs=[pl.BlockSpec((1, L, 2 * LANE), lambda b, h, p: (b, p, h)),
                  pl.BlockSpec((1, L, 2 * LANE), lambda b, h, p: (b, nb - 1 - p, h)),
                  pl.BlockSpec((1, S, 2 * LANE), lambda b, h, p: (b, 0, h)),
                  pl.BlockSpec((1, S, LANE), lambda b, h, p: (b, 0, h))],
        out_specs=[pl.BlockSpec((1, L, LANE), lambda b, h, p: (b, p, h)),
                   pl.BlockSpec((1, L, LANE), lambda b, h, p: (b, half - 1 - p, h))],
        out_shape=[jax.ShapeDtypeStruct((B, S // 2, W), BF16), jax.ShapeDtypeStruct((B, S // 2, W), BF16)],
        scratch_shapes=[pltpu.VMEM((2, L, 2 * LANE), BF16),
                        pltpu.VMEM((nb + 1, L, L), F32)],
        compiler_params=_params("parallel", "parallel", "arbitrary"),
        name="moba_attn",
    )(q_aug, q_aug, k_aug, v)


def _mem_attn_kernel(q_ref, k_ref, v_ref, gq_ref, gk_ref, o_ref):
    cq = _rms(q_ref[0], gq_ref[...]).astype(BF16)
    ck = _rms(k_ref[0], gk_ref[...]).astype(BF16)
    s = lax.dot_general(cq, ck, NT, preferred_element_type=F32) * (MEM_HEAD_DIM ** -0.5)
    m = jnp.max(s, axis=-1, keepdims=True)
    p = jnp.exp(s - m)
    l = jnp.sum(p, axis=-1, keepdims=True)
    o = jnp.dot(p.astype(BF16), v_ref[0].astype(BF16), preferred_element_type=F32) / l
    o_ref[0] = o.astype(o_ref.dtype)


def _mem_attn(proj3, kv3, gq, gk, q_col):
    B, S, _ = proj3.shape
    M = kv3.shape[1]
    hd = MEM_HEAD_DIM
    tq = 1024
    return pl.pallas_call(
        _mem_attn_kernel,
        grid=(B, MEM_HEADS, S // tq),
        in_specs=[pl.BlockSpec((1, tq, hd), lambda b, h, i: (b, i, q_col + h)),
                  pl.BlockSpec((1, M, hd), lambda b, h, i: (b, 0, h)),
                  pl.BlockSpec((1, M, hd), lambda b, h, i: (b, 0, MEM_HEADS + h)),
                  pl.BlockSpec((1, hd), lambda b, h, i: (0, 0)),
                  pl.BlockSpec((1, hd), lambda b, h, i: (0, 0))],
        out_specs=pl.BlockSpec((1, tq, hd), lambda b, h, i: (b, i, h)),
        out_shape=jax.ShapeDtypeStruct((B, S, MEM_HEADS * hd), BF16),
        compiler_params=_params("parallel", "parallel", "parallel"),
        name="mem_attn",
    )(proj3, kv3, kv3, gq, gk)


def _dn_prep_kernel(x_ref, w_ref, o_ref, pad_ref):
    S = x_ref.shape[1]
    cb = pl.program_id(1)
    x = x_ref[0]
    pad_ref[0:8, :] = jnp.zeros((8, LANE), F32)
    pad_ref[8:, :] = x
    w = w_ref[...]
    y = w[DN_CONV - 1:DN_CONV, :] * x
    for j in range(DN_CONV - 1):
        off = 8 - (DN_CONV - 1) + j
        y = y + w[j:j + 1, :] * pad_ref[off:off + S, :]
    y = y * jax.nn.sigmoid(y)
    nrm = lax.rsqrt(jnp.sum(y * y, axis=-1, keepdims=True) + EPS)
    scale = jnp.where(cb < DN_HEADS, nrm * (LANE ** -0.5), jnp.where(cb < 2 * DN_HEADS, nrm, 1.0))
    o_ref[0] = y * scale


def _dn_prep(proj3, conv_w, col0):
    B, S, _ = proj3.shape
    ncb = 3 * DN_HEADS
    return pl.pallas_call(
        _dn_prep_kernel,
        grid=(B, ncb),
        in_specs=[pl.BlockSpec((1, S, LANE), lambda b, c: (b, 0, col0 + c)),
                  pl.BlockSpec((DN_CONV, LANE), lambda b, c: (0, c))],
        out_specs=pl.BlockSpec((1, S, LANE), lambda b, c: (b, 0, c)),
        out_shape=jax.ShapeDtypeStruct((B, S, ncb * LANE), F32),
        scratch_shapes=[pltpu.VMEM((S + 8, LANE), F32)],
        compiler_params=_params("parallel", "parallel"),
        name="dn_prep",
    )(proj3, conv_w)


def _softplus(x):
    return jnp.maximum(x, 0.0) + jnp.log1p(jnp.exp(-jnp.abs(x)))


def _mm_split3(a, b):
    ah, al = _split_bf16(a)
    bh, bl = _split_bf16(b)
    return jnp.dot(jnp.concatenate([ah, ah, al], axis=1), jnp.concatenate([bh, bl, bh], axis=0),
                   preferred_element_type=F32)


def _dn_scan_kernel(q_ref, k_ref, v_ref, ba_ref, alr_ref, dtr_ref, z_ref, g_ref, o_ref, st_ref):
    C = DN_CHUNK
    H = DN_HEADS
    R = q_ref.shape[0]

    @pl.when(pl.program_id(1) == 0)
    def _():
        st_ref[...] = jnp.zeros_like(st_ref)

    ii = lax.broadcasted_iota(jnp.int32, (C, C), 0)
    jj = lax.broadcasted_iota(jnp.int32, (C, C), 1)
    incl = ii >= jj
    strict = ii > jj
    tril = incl.astype(F32)
    triu = (ii <= jj).astype(F32)
    gn = g_ref[...]

    gc_col, gc_row, beta = [], [], []
    for r in range(R):
        ba = ba_ref[r]
        g_col = -jnp.exp(alr_ref[...]) * _softplus(ba[:, H:2 * H] + dtr_ref[...])
        gc_col.append(jnp.dot(tril, g_col, precision=HI, preferred_element_type=F32))
        gc_row.append(lax.dot_general(g_col, triu, TN, precision=HI, preferred_element_type=F32))
        beta.append(jax.nn.sigmoid(ba[:, 0:H]))

    def mm(a, b):
        return jnp.dot(a.astype(BF16), b.astype(BF16), preferred_element_type=F32)

    units = [(r, h) for r in range(R) for h in range(H)]
    U = range(len(units))
    sl = [slice(h * LANE, (h + 1) * LANE) for _, h in units]
    q = [q_ref[r, :, sl[u]] for u, (r, h) in enumerate(units)]
    k = [k_ref[r, :, sl[u]] for u, (r, h) in enumerate(units)]
    gcc = [gc_col[r][:, h:h + 1] for r, h in units]
    bet = [beta[r][:, h:h + 1] for r, h in units]
    decay = [jnp.where(incl, jnp.exp(jnp.where(incl, gcc[u] - gc_row[r][h:h + 1, :], 0.0)), 0.0)
             for u, (r, h) in enumerate(units)]
    kb = [k[u] * bet[u] for u in U]
    vb = [v_ref[r, :, sl[u]] * bet[u] for u, (r, h) in enumerate(units)]
    kbf = [k[u].astype(BF16) for u in U]
    kq = [lax.dot_general(jnp.concatenate([kb[u], q[u]], axis=0).astype(BF16), kbf[u], NT,
                          preferred_element_type=F32) for u in U]
    kk = [kq[u][:C] for u in U]
    qk = [kq[u][C:] for u in U]
    n = [jnp.where(strict, kk[u] * decay[u], 0.0) for u in U]
    attn = [jnp.where(incl, qk[u] * decay[u], 0.0) for u in U]
    egc = [jnp.exp(gcc[u]) for u in U]
    x = [jnp.concatenate([vb[u], kb[u] * egc[u]], axis=1) for u in U]
    nx = [_mm_split3(n[u], x[u]) for u in U]
    p = [_mm_split3(n[u], n[u]) for u in U]
    x = [x[u] - nx[u] for u in U]
    levels = int(math.log2(C)) - 1
    for lvl in range(levels):
        px = [_mm_split3(p[u], x[u]) for u in U]
        if lvl + 1 < levels:
            p = [_mm_split3(p[u], p[u]) for u in U]
        x = [x[u] + px[u] for u in U]
    s = [st_ref[u] for u in U]
    sb = [s[u].astype(BF16) for u in U]
    wq = [mm(jnp.concatenate([x[u][:, LANE:], q[u] * egc[u]], axis=0), sb[u]) for u in U]
    ws = [wq[u][:C] for u in U]
    qs = [wq[u][C:] for u in U]
    v_new = [(x[u][:, :LANE] - ws[u]).astype(BF16) for u in U]
    av = [mm(attn[u], v_new[u]) for u in U]
    g_last = [gcc[u][C - 1:C, :] for u in U]
    kd = [(k[u] * jnp.exp(g_last[u] - gcc[u])).astype(BF16) for u in U]
    kv = [lax.dot_general(kd[u], v_new[u], TN, preferred_element_type=F32) for u in U]
    for u, (r, h) in enumerate(units):
        st_ref[u] = s[u] * jnp.exp(g_last[u]) + kv[u]
        zz = z_ref[r, :, sl[u]].astype(F32)
        o_ref[r, :, sl[u]] = (_rms(qs[u] + av[u], gn) * (zz * jax.nn.sigmoid(zz))).astype(o_ref.dtype)


DN_ROWS = 2


def _dn_scan(qkv, ba, a_log, dt_bias, proj3, z_col, onorm_g):
    B, S, _ = qkv.shape
    H = DN_HEADS
    C = DN_CHUNK
    W = H * LANE
    R = DN_ROWS
    tile = lambda col: pl.BlockSpec((R, C, W), lambda b, c: (b, c, col))
    small = lambda shape: pl.BlockSpec(shape, lambda b, c: (0,) * len(shape))
    return pl.pallas_call(
        _dn_scan_kernel,
        grid=(B // R, S // C),
        in_specs=[tile(0), tile(1), tile(2),
                  pl.BlockSpec((R, C, LANE), lambda b, c: (b, c, 0)),
                  small((1, H)), small((1, H)),
                  tile(z_col), small((1, LANE))],
        out_specs=tile(0),
        out_shape=jax.ShapeDtypeStruct((B, S, W), BF16),
        scratch_shapes=[pltpu.VMEM((R * H, LANE, LANE), F32)],
        compiler_params=_params("parallel", "arbitrary"),
        name="dn_scan",
    )(qkv, qkv, qkv, ba, a_log.reshape(1, H), dt_bias.reshape(1, H), proj3, onorm_g)


def _merge_kernel(x_ref, lo_ref, hi_ref, b1_ref, b2_ref, g0_ref, g1_ref, g2_ref, wb_ref, wo_ref, o_ref, *,
                  tiles_per_seq):
    in_first_half = pl.program_id(0) % tiles_per_seq < tiles_per_seq // 2
    moba = jnp.where(in_first_half, lo_ref[...], hi_ref[...])
    mixed = None
    for i, (b, g_ref) in enumerate(((moba, g0_ref), (b1_ref[...], g1_ref), (b2_ref[...], g2_ref))):
        bp = jnp.dot(b, wb_ref[i], preferred_element_type=F32)
        t = jax.nn.sigmoid(g_ref[...].astype(F32)) * bp
        mixed = t if mixed is None else mixed + t
    o_ref[...] = x_ref[...] + jnp.dot(mixed.astype(BF16), wo_ref[...], preferred_element_type=F32)


def _merge(x2, moba_lo, moba_hi, dn_out, mem_out, proj, gate_col, w_branch, w_out, seq_len):
    T, D = x2.shape
    tm = 256
    tps = seq_len // tm
    half = tps // 2
    row = lambda col: pl.BlockSpec((tm, D), lambda i: (i, col))
    lo_spec = pl.BlockSpec((tm, D), lambda i: ((i // tps) * half + jnp.minimum(i % tps, half - 1), 0))
    hi_spec = pl.BlockSpec((tm, D), lambda i: ((i // tps) * half + jnp.maximum(i % tps - half, 0), 0))
    return pl.pallas_call(
        functools.partial(_merge_kernel, tiles_per_seq=tps),
        grid=(T // tm,),
        in_specs=[row(0), lo_spec, hi_spec, row(0), row(0), row(gate_col), row(gate_col + 1), row(gate_col + 2),
                  pl.BlockSpec((3, D, D), lambda i: (0, 0, 0)),
                  pl.BlockSpec((D, D), lambda i: (0, 0))],
        out_specs=row(0),
        out_shape=jax.ShapeDtypeStruct((T, D), F32),
        compiler_params=_params("parallel"),
        name="merge",
    )(x2, moba_lo, moba_hi, dn_out, mem_out, proj, proj, proj, w_branch, w_out)


def _compare_exchange(a, b):
    if a is None:
        return b, None
    if b is None:
        return a, None
    return jnp.maximum(a, b), jnp.minimum(a, b)


def _bitonic_merge_desc(xs):
    n = len(xs)
    j = n // 2
    while j >= 1:
        for i in range(n):
            l = i ^ j
            if l > i:
                xs[i], xs[l] = _compare_exchange(xs[i], xs[l])
        j //= 2
    return xs


def _bitonic_sort_desc(xs):
    xs = list(xs)
    n = len(xs)
    k = 2
    while k <= n:
        j = k // 2
        while j >= 1:
            for i in range(n):
                l = i ^ j
                if l > i:
                    hi, lo = _compare_exchange(xs[i], xs[l])
                    xs[i], xs[l] = (hi, lo) if (i & k) == 0 else (lo, hi)
            j //= 2
        k *= 2
    return xs


def _top16_over_rows(pieces):
    K = PEER_TOPK
    xs = _bitonic_sort_desc(list(pieces) + [None] * (K - len(pieces)))
    for shift in (4, 2, 1):
        other = [None if x is None else pltpu.roll(x, shift, 0) for x in xs]
        merged = []
        for i in range(K):
            a, b = xs[i], other[K - 1 - i]
            merged.append(b if a is None else a if b is None else jnp.maximum(a, b))
        xs = _bitonic_merge_desc(merged)
    return xs


def _rows_from_replicated(vals, sub):
    out = vals[0]
    for r in range(1, 8):
        out = jnp.where(sub == r, vals[r], out)
    return out


def _peer_select_kernel(qp_ref, k1_ref, k2_ref, s1_ref, s2_ref, tau_ref, cc_ref, v2_ref):
    tm = qp_ref.shape[0]
    K = PEER_KEYS
    q = qp_ref[...]
    s1_ref[0] = lax.dot_general(k1_ref[...], q[:, :K], NT, precision=HI, preferred_element_type=F32)
    s2_ref[0] = lax.dot_general(k2_ref[...], q[:, K:], NT, precision=HI, preferred_element_type=F32)
    sub = lax.broadcasted_iota(jnp.int32, (8, LANE), 0)
    ninf = -jnp.inf
    for c in range(tm // LANE):
        cs = slice(c * LANE, (c + 1) * LANE)
        v1 = _top16_over_rows([s1_ref[0, 8 * r:8 * r + 8, cs] for r in range(K // 8)])
        v2 = _top16_over_rows([s2_ref[0, 8 * r:8 * r + 8, cs] for r in range(K // 8)])
        v1lo, v1hi = _rows_from_replicated(v1[:8], sub), _rows_from_replicated(v1[8:], sub)
        v2lo, v2hi = _rows_from_replicated(v2[:8], sub), _rows_from_replicated(v2[8:], sub)
        cands = [
            v1[0] + v2lo, v1[0] + v2hi,
            v1[1] + v2lo,
            jnp.where(sub >= 2, v2[0] + v1lo, ninf), v2[0] + v1hi,
            jnp.where(sub >= 2, v2[1] + v1lo, ninf),
            jnp.where((sub >= 2) & (sub <= 4), v1[2] + v2lo, ninf),
            jnp.where((sub >= 2) & (sub <= 3), v1[3] + v2lo, ninf),
            jnp.where(sub == 2, v1[4] + v2lo, ninf),
        ]
        top = _top16_over_rows(cands)
        smax = top[0]
        z = jnp.exp(top[0] - smax)
        for t in top[1:]:
            z = z + jnp.exp(t - smax)
        v2_ref[0, 0:8, cs] = v2lo
        v2_ref[0, 8:16, cs] = v2hi
        tau_ref[0, :, cs] = top[PEER_TOPK - 1][0:1, :]
        cc_ref[0, :, cs] = (smax + jnp.log(z))[0:1, :]


def _peer_select(qp, keys1, keys2):
    T = qp.shape[0]
    H, K = PEER_HEADS, PEER_KEYS
    tm = 512
    return pl.pallas_call(
        _peer_select_kernel,
        grid=(T // tm, H),
        in_specs=[pl.BlockSpec((tm, 2 * K), lambda i, h: (i, h)),
                  pl.BlockSpec((K, K), lambda i, h: (0, 0)),
                  pl.BlockSpec((K, K), lambda i, h: (0, 0))],
        out_specs=[pl.BlockSpec((1, K, tm), lambda i, h: (h, 0, i)),
                   pl.BlockSpec((1, K, tm), lambda i, h: (h, 0, i)),
                   pl.BlockSpec((1, 1, tm), lambda i, h: (h, 0, i)),
                   pl.BlockSpec((1, 1, tm), lambda i, h: (h, 0, i)),
                   pl.BlockSpec((1, PEER_TOPK, tm), lambda i, h: (h, 0, i))],
        out_shape=[jax.ShapeDtypeStruct((H, K, T), F32), jax.ShapeDtypeStruct((H, K, T), F32),
                   jax.ShapeDtypeStruct((H, 1, T), F32), jax.ShapeDtypeStruct((H, 1, T), F32),
                   jax.ShapeDtypeStruct((H, PEER_TOPK, T), F32)],
        compiler_params=_params("parallel", "parallel"),
        name="peer_select",
    )(qp, keys1, keys2)


GELU_C0 = math.sqrt(2.0 / math.pi)
GELU_C1 = GELU_C0 * 0.044715
PEER_ROWS = 256
PEER_LANES = 256


def _peer_dense_kernel(h_ref, u_ref, v_ref, s1_ref, s2_ref, tau_ref, cc_ref, v2_ref, x_ref, o_ref,
                       e1_ref, e2_ref, thr_ref, rowb_ref, g_ref, w_ref, acc_ref):
    K = PEER_KEYS
    H = PEER_HEADS
    SUB = 8
    j = pl.program_id(1)
    te, tm = g_ref.shape
    n_a = te // K

    @pl.when(j == 0)
    def _():
        acc_ref[...] = jnp.zeros_like(acc_ref)
        for h in range(H):
            s1 = s1_ref[h]
            m1 = jnp.max(s1, axis=0, keepdims=True)
            e1_ref[h] = jnp.exp(s1 - m1)
            e2_ref[h] = 0.5 * jnp.exp(s2_ref[h] - (cc_ref[h] - m1))
            tau = tau_ref[h]
            thr = jnp.full((K, tm), jnp.inf, F32)
            for k in range(PEER_TOPK):
                v2k = v2_ref[h, k:k + 1, :]
                thr = jnp.where(s1 + v2k >= tau, v2k, thr)
            thr_ref[h] = thr

    g_ref[...] = lax.dot_general(u_ref[...], h_ref[...], NT, preferred_element_type=F32)

    a0 = pl.multiple_of(j * n_a, n_a)
    for h in range(H):
        thrt = thr_ref[h, pl.ds(a0, n_a), :]
        e1t = e1_ref[h, pl.ds(a0, n_a), :]
        for r in range(n_a):
            rowb_ref[h, r, 0, :, 0:tm] = jnp.broadcast_to(thrt[r:r + 1, :], (SUB, tm))
            rowb_ref[h, r, 1, :, 0:tm] = jnp.broadcast_to(e1t[r:r + 1, :], (SUB, tm))

    def slab(sb, carry):
        b0 = pl.multiple_of(sb * SUB, SUB)
        for l0 in range(0, tm, PEER_LANES):
            ls = slice(l0, l0 + PEER_LANES)
            w = [None] * n_a
            for h in range(H):
                s2s = s2_ref[h, pl.ds(b0, SUB), ls]
                e2s = e2_ref[h, pl.ds(b0, SUB), ls]
                for r in range(n_a):
                    t = jnp.where(s2s >= rowb_ref[h, r, 0, :, ls], e2s * rowb_ref[h, r, 1, :, ls], 0.0)
                    w[r] = t if w[r] is None else w[r] + t
            for r in range(n_a):
                w_ref[r, pl.ds(b0, SUB), ls] = w[r]
        return carry

    lax.fori_loop(0, K // SUB, slab, 0, unroll=4)

    total = None
    for c in range(te // PEER_ROWS):
        rows = slice(c * PEER_ROWS, (c + 1) * PEER_ROWS)
        x = g_ref[rows, :]
        th = jnp.tanh(x * (GELU_C0 + GELU_C1 * (x * x)))
        w = w_ref[c * (PEER_ROWS // K):(c + 1) * (PEER_ROWS // K)].reshape(PEER_ROWS, tm)
        wa = (w * (x + x * th)).astype(BF16)
        d = lax.dot_general(v_ref[rows, :], wa, TN, preferred_element_type=F32)
        total = d if total is None else total + d
    acc_ref[...] += total

    @pl.when(j == pl.num_programs(1) - 1)
    def _():
        o_ref[...] = x_ref[...] + acc_ref[...].T


def _peer_dense(h2, u, v, s1, s2, tau, cc, v2, x1):
    T, D = h2.shape
    E = u.shape[0]
    H, K = PEER_HEADS, PEER_KEYS
    tm, te = 512, 1024
    return pl.pallas_call(
        _peer_dense_kernel,
        grid=(T // tm, E // te),
        in_specs=[pl.BlockSpec((tm, D), lambda i, j: (i, 0)),
                  pl.BlockSpec((te, D), lambda i, j: (j, 0)),
                  pl.BlockSpec((te, D), lambda i, j: (j, 0)),
                  pl.BlockSpec((H, K, tm), lambda i, j: (0, 0, i)),
                  pl.BlockSpec((H, K, tm), lambda i, j: (0, 0, i)),
                  pl.BlockSpec((H, 1, tm), lambda i, j: (0, 0, i)),
                  pl.BlockSpec((H, 1, tm), lambda i, j: (0, 0, i)),
                  pl.BlockSpec((H, PEER_TOPK, tm), lambda i, j: (0, 0, i)),
                  pl.BlockSpec((tm, D), lambda i, j: (i, 0))],
        out_specs=pl.BlockSpec((tm, D), lambda i, j: (i, 0)),
        out_shape=jax.ShapeDtypeStruct((T, D), F32),
        scratch_shapes=[pltpu.VMEM((H, K, tm), F32),
                        pltpu.VMEM((H, K, tm), F32),
                        pltpu.VMEM((H, K, tm), F32),
                        pltpu.VMEM((H, te // K, 2, 8, tm + LANE), F32),
                        pltpu.VMEM((te, tm), F32),
                        pltpu.VMEM((te // K, K, tm), F32),
                        pltpu.VMEM((D, tm), F32)],
        compiler_params=_params("parallel", "arbitrary"),
        name="peer_dense",
    )(h2, u, v, s1, s2, tau, cc, v2, x1)


def _rope_tables(S):
    inv = 1.0 / (ROPE_THETA ** (np.arange(0, LANE, 2, dtype=np.float64) / LANE))
    ang = np.arange(S, dtype=np.float64)[:, None] * inv[None, :]
    cos, sin = np.cos(ang), np.sin(ang)
    return (jnp.asarray(np.concatenate([cos, cos], axis=1), F32),
            jnp.asarray(np.concatenate([-sin, sin], axis=1), F32))


def _layer(x, mem, attn_norm_g, mem_norm_g, ffn_norm_g, w_in, moba_q_norm_g, moba_k_norm_g,
           dn_conv_w, dn_a_log, dn_dt_bias, dn_out_norm_g, w_mem_kv, mem_q_norm_g, mem_k_norm_g,
           w_branch, w_out, peer_w_q, peer_keys1, peer_keys2, peer_u, peer_v):
    B, S, D = x.shape
    T = B * S
    x2 = x.reshape(T, D)
    row = lambda g: g.reshape(1, -1)

    moba_w, dn_w = 3 * MOBA_HEADS * LANE, 3 * DN_HEADS * LANE
    n_small = 2 * DN_HEADS
    o_small = moba_w + dn_w
    qk_w = 2 * MOBA_HEADS * LANE
    o_z = o_small + n_small
    w_f32 = jnp.concatenate([w_in[:, :qk_w], w_in[:, moba_w:o_small], w_in[:, o_z + D:o_z + 2 * D]],
                            axis=1).astype(BF16)
    w_b16 = jnp.concatenate([w_in[:, qk_w:moba_w], w_in[:, o_z:o_z + D], w_in[:, o_z + 2 * D:]],
                            axis=1).astype(BF16)
    w_small = jnp.pad(w_in[:, o_small:o_z], ((0, 0), (0, LANE - n_small)))
    proj, small = _norm_matmul(x2, row(attn_norm_g), w_f32, tm=1024, tn=1024, w_side=w_small, name="in_proj")
    projb = _norm_matmul(x2, row(attn_norm_g), w_b16, tm=1024, tn=1024, out_dtype=BF16, name="in_proj_bf16")
    proj3 = proj.reshape(B, S, -1)
    projb3 = projb.reshape(B, S, -1)
    dn_col = qk_w // LANE
    memq_col = (qk_w + dn_w) // MEM_HEAD_DIM
    z_col = 1
    gate_col = 2

    cos2, sin2 = _rope_tables(S)
    mq, mk, kmean = _moba_prep(proj3, row(moba_q_norm_g), row(moba_k_norm_g), cos2, sin2)
    moba_lo, moba_hi = _moba_attn(_moba_gate(mq, kmean), mk, projb3)

    dn_qkv = _dn_prep(proj3, dn_conv_w, dn_col)
    dn_out = _dn_scan(dn_qkv, small.reshape(B, S, LANE), dn_a_log, dn_dt_bias, projb3, z_col,
                      row(dn_out_norm_g))

    M = mem.shape[1]
    kv = _norm_matmul(mem.reshape(B * M, D), row(mem_norm_g), w_mem_kv.astype(BF16), tm=B * M, tn=512, name="mem_kv")
    mem_out = _mem_attn(proj3, kv.reshape(B, M, -1), row(mem_q_norm_g), row(mem_k_norm_g), memq_col)

    x1 = _merge(x2, moba_lo.reshape(T // 2, D), moba_hi.reshape(T // 2, D), dn_out.reshape(T, D),
                mem_out.reshape(T, D), projb, gate_col, w_branch.astype(BF16), w_out.astype(BF16), S)

    qp, h2 = _norm_matmul(x1, row(ffn_norm_g), peer_w_q.astype(BF16), tm=1024, tn=1024, emit_h=True, name="peer_query")
    s1, s2, tau, cc, v2 = _peer_select(qp, peer_keys1, peer_keys2)
    out = _peer_dense(h2, peer_u.astype(BF16), peer_v.astype(BF16), s1, s2, tau, cc, v2, x1)
    return out.reshape(B, S, D)


def kernel(x, mem, attn_norm_g, mem_norm_g, ffn_norm_g, w_in, moba_q_norm_g, moba_k_norm_g, dn_conv_w, dn_a_log, dn_dt_bias, dn_out_norm_g, w_mem_kv, mem_q_norm_g, mem_k_norm_g, w_branch, w_out, peer_w_q, peer_keys1, peer_keys2, peer_u, peer_v):
    for l in range(w_in.shape[0]):
        x = _layer(x, mem, attn_norm_g[l], mem_norm_g[l], ffn_norm_g[l], w_in[l], moba_q_norm_g[l],
                   moba_k_norm_g[l], dn_conv_w[l], dn_a_log[l], dn_dt_bias[l], dn_out_norm_g[l],
                   w_mem_kv[l], mem_q_norm_g[l], mem_k_norm_g[l], w_branch[l], w_out[l], peer_w_q[l],
                   peer_keys1[l], peer_keys2[l], peer_u[l], peer_v[l])
    return x
```

```python
import functools
import math

import jax
import jax.numpy as jnp
import numpy as np
from jax import lax
from jax.experimental import pallas as pl
from jax.experimental.pallas import tpu as pltpu

F32 = jnp.float32
BF16 = jnp.bfloat16
HI = lax.Precision.HIGHEST
EPS = 1e-6
ROPE_THETA = 10000.0
NT = (((1,), (1,)), ((), ()))
TN = (((0,), (0,)), ((), ()))

LANE = 128
MOBA_HEADS = 8
MOBA_BLOCK = 256
MOBA_TOPK = 3
DN_HEADS = 8
DN_CHUNK = 64
DN_CONV = 4
MEM_HEADS = 4
MEM_HEAD_DIM = 256
PEER_HEADS = 8
PEER_KEYS = 128
PEER_TOPK = 16
PEER_V2 = 8

VMEM_LIMIT = 56 * 1024 * 1024


def _params(*sem):
    return pltpu.CompilerParams(dimension_semantics=sem, vmem_limit_bytes=VMEM_LIMIT)


def _rms(x, g):
    ms = jnp.mean(x * x, axis=-1, keepdims=True)
    return x * lax.rsqrt(ms + EPS) * g


def _split_bf16(a):
    hi = a.astype(BF16)
    return hi, (a - hi.astype(F32)).astype(BF16)


def _norm_matmul_kernel(*refs, precision, emit_h, with_side):
    x_ref, g_ref, w_ref = refs[:3]
    refs = list(refs[3:])
    ws_ref = refs.pop(0) if with_side else None
    o_ref = refs.pop(0)
    ho_ref = refs.pop(0) if emit_h else None
    so_ref = refs.pop(0) if with_side else None
    (h_ref,) = refs

    @pl.when(pl.program_id(1) == 0)
    def _():
        h = _rms(x_ref[...], g_ref[...])
        h_ref[...] = h.astype(h_ref.dtype)
        if emit_h:
            ho_ref[...] = h.astype(ho_ref.dtype)
        if with_side:
            hh, hl = _split_bf16(h)
            ws = ws_ref[...]
            n = ws.shape[1] // 2
            r = jnp.dot(hh, ws, preferred_element_type=F32)
            so_ref[...] = r[:, :n] + r[:, n:] + jnp.dot(hl, ws[:, :n], preferred_element_type=F32)

    o_ref[...] = jnp.dot(h_ref[...], w_ref[...], precision=precision,
                         preferred_element_type=F32).astype(o_ref.dtype)


def _norm_matmul(x, g, w, *, tm, tn, name, out_dtype=F32, precision=None, emit_h=False, w_side=None):
    T, D = x.shape
    N = w.shape[1]
    with_side = w_side is not None
    operands = [x, g, w]
    in_specs = [pl.BlockSpec((tm, D), lambda i, j: (i, 0)),
                pl.BlockSpec((1, D), lambda i, j: (0, 0)),
                pl.BlockSpec((D, tn), lambda i, j: (0, j))]
    out_shape = [jax.ShapeDtypeStruct((T, N), out_dtype)]
    out_specs = [pl.BlockSpec((tm, tn), lambda i, j: (i, j))]
    if with_side:
        operands.append(jnp.concatenate(_split_bf16(w_side), axis=1))
        in_specs.append(pl.BlockSpec((D, 2 * w_side.shape[1]), lambda i, j: (0, 0)))
    if emit_h:
        out_shape.append(jax.ShapeDtypeStruct((T, D), BF16))
        out_specs.append(pl.BlockSpec((tm, D), lambda i, j: (i, 0)))
    if with_side:
        out_shape.append(jax.ShapeDtypeStruct((T, w_side.shape[1]), F32))
        out_specs.append(pl.BlockSpec((tm, w_side.shape[1]), lambda i, j: (i, 0)))
    res = pl.pallas_call(
        functools.partial(_norm_matmul_kernel, precision=precision, emit_h=emit_h, with_side=with_side),
        grid=(T // tm, N // tn),
        in_specs=in_specs,
        out_specs=out_specs,
        out_shape=out_shape,
        scratch_shapes=[pltpu.VMEM((tm, D), w.dtype)],
        compiler_params=_params("parallel", "arbitrary"),
        name=name,
    )(*operands)
    return res if len(res) > 1 else res[0]


MOBA_NEG = -1e30


def _moba_prep_kernel(q_ref, k_ref, gq_ref, gk_ref, cos_ref, sin_ref, qo_ref, ko_ref, km_ref):
    cos = cos_ref[...]
    sin = sin_ref[...]

    def norm_rope(x, g):
        y = _rms(x, g)
        return y * cos + pltpu.roll(y, LANE // 2, 1) * sin

    q = norm_rope(q_ref[0], gq_ref[...])
    k = norm_rope(k_ref[0], gk_ref[...])
    ts = k.shape[0]
    nb = ts // MOBA_BLOCK
    qo_ref[0] = q
    row = lax.broadcasted_iota(jnp.int32, (ts, LANE), 0) + pl.program_id(2) * ts
    lane = lax.broadcasted_iota(jnp.int32, (ts, LANE), 1)
    ko_ref[0, :, 0:LANE] = k.astype(BF16)
    ko_ref[0, :, LANE:2 * LANE] = jnp.where(lane == row // MOBA_BLOCK, 1.0, 0.0).astype(BF16)
    km_ref[0, 0] = jnp.mean(k.reshape(nb, MOBA_BLOCK, LANE), axis=1)


def _moba_prep(proj3, gq, gk, cos2, sin2):
    B, S, _ = proj3.shape
    H = MOBA_HEADS
    ts = 2048
    nb_t = ts // MOBA_BLOCK
    col = lambda off: pl.BlockSpec((1, ts, LANE), lambda b, h, s: (b, s, off + h))
    return pl.pallas_call(
        _moba_prep_kernel,
        grid=(B, H, S // ts),
        in_specs=[col(0), col(H),
                  pl.BlockSpec((1, LANE), lambda b, h, s: (0, 0)),
                  pl.BlockSpec((1, LANE), lambda b, h, s: (0, 0)),
                  pl.BlockSpec((ts, LANE), lambda b, h, s: (s, 0)),
                  pl.BlockSpec((ts, LANE), lambda b, h, s: (s, 0))],
        out_specs=[col(0),
                   pl.BlockSpec((1, ts, 2 * LANE), lambda b, h, s: (b, s, h)),
                   pl.BlockSpec((1, 1, nb_t, LANE), lambda b, h, s: (b, h, s, 0))],
        out_shape=[jax.ShapeDtypeStruct((B, S, H * LANE), F32),
                   jax.ShapeDtypeStruct((B, S, H * 2 * LANE), BF16),
                   jax.ShapeDtypeStruct((B, H, S // MOBA_BLOCK, LANE), F32)],
        compiler_params=_params("parallel", "parallel", "parallel"),
        name="moba_prep",
    )(proj3, proj3, gq, gk, cos2, sin2)


def _moba_gate_kernel(q_ref, km_ref, o_ref):
    L = MOBA_BLOCK
    q = q_ref[0]
    km = km_ref[0, 0]
    nb = km.shape[0]
    S = q.shape[0]
    gate = lax.dot_general(km, q, NT, precision=HI, preferred_element_type=F32)
    row = lax.broadcasted_iota(jnp.int32, (nb, S), 0)
    own = lax.broadcasted_iota(jnp.int32, (nb, S), 1) // L
    rank = jnp.zeros((nb, S), jnp.int32)
    for m in range(nb - 1):
        gm = gate[m:m + 1, :]
        cnt = jnp.where(row > m, jnp.where(gm >= gate, 1, 0), jnp.where(gm > gate, 1, 0))
        rank = rank + jnp.where(own > m, cnt, 0)
    keep = jnp.where(row < own, jnp.where(rank < MOBA_TOPK, 1.0, 0.0), jnp.where(row == own, 1.0, 0.0))
    eye = (lax.broadcasted_iota(jnp.int32, (nb, LANE), 0)
           == lax.broadcasted_iota(jnp.int32, (nb, LANE), 1)).astype(BF16)
    keep_t = lax.dot_general(keep.astype(BF16), eye, TN, preferred_element_type=F32)
    o_ref[0, :, 0:LANE] = (q * (LANE ** -0.5 * math.log2(math.e))).astype(BF16)
    o_ref[0, :, LANE:2 * LANE] = jnp.where(keep_t > 0.5, 0.0, MOBA_NEG).astype(BF16)


def _moba_gate(q, kmean):
    B, S, W = q.shape
    H = MOBA_HEADS
    nb = kmean.shape[2]
    return pl.pallas_call(
        _moba_gate_kernel,
        grid=(B, H),
        in_specs=[pl.BlockSpec((1, S, LANE), lambda b, h: (b, 0, h)),
                  pl.BlockSpec((1, 1, nb, LANE), lambda b, h: (b, h, 0, 0))],
        out_specs=pl.BlockSpec((1, S, 2 * LANE), lambda b, h: (b, 0, h)),
        out_shape=jax.ShapeDtypeStruct((B, S, 2 * W), BF16),
        compiler_params=_params("parallel", "parallel"),
        name="moba_gate",
    )(q, kmean)


def _moba_attn_kernel(qa_ref, qb_ref, k_ref, v_ref, oa_ref, ob_ref, qaug_ref, s_ref):
    L = MOBA_BLOCK
    nb = k_ref.shape[1] // L
    half = nb // 2
    p = pl.program_id(2)
    tiles = (p, nb - 1 - p)
    qaug_ref[0] = qa_ref[0]
    qaug_ref[1] = qb_ref[0]

    def scores(qa, n):
        kb = k_ref[0, pl.ds(pl.multiple_of(n * L, L), L), :]
        return lax.dot_general(qa, kb, NT, preferred_element_type=F32)

    def half_max(s):
        return jnp.maximum(s[:, :LANE], s[:, LANE:])

    def dyn_slot(j):
        first = j < p
        return first, jnp.where(first, 0, 1), jnp.where(first, j, half + j - p)

    r = lax.broadcasted_iota(jnp.int32, (L, L), 0)
    c = lax.broadcasted_iota(jnp.int32, (L, L), 1)
    causal = jnp.where(c <= r, 0.0, MOBA_NEG)

    m = [None, None]
    for t in range(2):
        s = scores(qaug_ref[t], tiles[t]) + causal
        s_ref[nb - 1 + t] = s
        m[t] = half_max(s)
    for n in range(half):
        s = scores(qaug_ref[1], n)
        s_ref[n] = s
        m[1] = jnp.maximum(m[1], half_max(s))
    for j in range(half - 1):
        first, t, n = dyn_slot(j)
        s = scores(qaug_ref[t], n)
        s_ref[half + j] = s
        hm = half_max(s)
        m[0] = jnp.maximum(m[0], jnp.where(first, hm, MOBA_NEG))
        m[1] = jnp.maximum(m[1], jnp.where(first, MOBA_NEG, hm))
    mb = [jnp.broadcast_to(jnp.max(m[t], axis=-1, keepdims=True), (L, LANE)) for t in range(2)]

    def probs(k, mbt):
        s = s_ref[k]
        return jnp.concatenate([jnp.exp2(s[:, :LANE] - mbt), jnp.exp2(s[:, LANE:] - mbt)], axis=1)

    def value(n):
        return v_ref[0, pl.ds(pl.multiple_of(n * L, L), L), :]

    l = [None, None]
    acc = [None, None]
    for t in range(2):
        pr = probs(nb - 1 + t, mb[t])
        l[t] = pr[:, :LANE] + pr[:, LANE:]
        acc[t] = jnp.dot(pr.astype(BF16), value(tiles[t]), preferred_element_type=F32)
    for n in range(half):
        pr = probs(n, mb[1])
        l[1] = l[1] + (pr[:, :LANE] + pr[:, LANE:])
        acc[1] = acc[1] + jnp.dot(pr.astype(BF16), value(n), preferred_element_type=F32)
    for j in range(half - 1):
        first, t, n = dyn_slot(j)
        pr = probs(half + j, jnp.where(first, mb[0], mb[1]))
        ps = pr[:, :LANE] + pr[:, LANE:]
        d = jnp.dot(pr.astype(BF16), value(n), preferred_element_type=F32)
        l[0] = l[0] + jnp.where(first, ps, 0.0)
        l[1] = l[1] + jnp.where(first, 0.0, ps)
        acc[0] = acc[0] + jnp.where(first, d, 0.0)
        acc[1] = acc[1] + jnp.where(first, 0.0, d)

    for t, o_ref in enumerate((oa_ref, ob_ref)):
        o_ref[0] = (acc[t] / jnp.sum(l[t], axis=-1, keepdims=True)).astype(o_ref.dtype)


def _moba_attn(q_aug, k_aug, v):
    B, S, _ = v.shape
    H = MOBA_HEADS
    W = H * LANE
    L = MOBA_BLOCK
    nb = S // L
    half = nb // 2
    return pl.pallas_call(
        _moba_attn_kernel,
        grid=(B, H, half),
        in_specs=[pl.BlockSpec((1, L, 2 * LANE), lambda b, h, p: (b, p, h)),
                  pl.BlockSpec((1, L, 2 * LANE), lambda b, h, p: (b, nb - 1 - p, h)),
                  pl.BlockSpec((1, S, 2 * LANE), lambda b, h, p: (b, 0, h)),
                  pl.BlockSpec((1, S, LANE), lambda b, h, p: (b, 0, h))],
        out_specs=[pl.BlockSpec((1, L, LANE), lambda b, h, p: (b, p, h)),
                   pl.BlockSpec((1, L, LANE), lambda b, h, p: (b, half - 1 - p, h))],
        out_shape=[jax.ShapeDtypeStruct((B, S // 2, W), BF16), jax.ShapeDtypeStruct((B, S // 2, W), BF16)],
        scratch_shapes=[pltpu.VMEM((2, L, 2 * LANE), BF16),
                        pltpu.VMEM((nb + 1, L, L), F32)],
        compiler_params=_params("parallel", "parallel", "arbitrary"),
        name="moba_attn",
    )(q_aug, q_aug, k_aug, v)


def _mem_attn_kernel(q_ref, k_ref, v_ref, gq_ref, gk_ref, o_ref):
    cq = _rms(q_ref[0], gq_ref[...]).astype(BF16)
    ck = _rms(k_ref[0], gk_ref[...]).astype(BF16)
    s = lax.dot_general(cq, ck, NT, preferred_element_type=F32) * (MEM_HEAD_DIM ** -0.5)
    m = jnp.max(s, axis=-1, keepdims=True)
    p = jnp.exp(s - m)
    l = jnp.sum(p, axis=-1, keepdims=True)
    o = jnp.dot(p.astype(BF16), v_ref[0].astype(BF16), preferred_element_type=F32) / l
    o_ref[0] = o.astype(o_ref.dtype)


def _mem_attn(proj3, kv3, gq, gk, q_col):
    B, S, _ = proj3.shape
    M = kv3.shape[1]
    hd = MEM_HEAD_DIM
    tq = 1024
    return pl.pallas_call(
        _mem_attn_kernel,
        grid=(B, MEM_HEADS, S // tq),
        in_specs=[pl.BlockSpec((1, tq, hd), lambda b, h, i: (b, i, q_col + h)),
                  pl.BlockSpec((1, M, hd), lambda b, h, i: (b, 0, h)),
                  pl.BlockSpec((1, M, hd), lambda b, h, i: (b, 0, MEM_HEADS + h)),
                  pl.BlockSpec((1, hd), lambda b, h, i: (0, 0)),
                  pl.BlockSpec((1, hd), lambda b, h, i: (0, 0))],
        out_specs=pl.BlockSpec((1, tq, hd), lambda b, h, i: (b, i, h)),
        out_shape=jax.ShapeDtypeStruct((B, S, MEM_HEADS * hd), BF16),
        compiler_params=_params("parallel", "parallel", "parallel"),
        name="mem_attn",
    )(proj3, kv3, kv3, gq, gk)


def _dn_prep_kernel(x_ref, w_ref, o_ref, pad_ref):
    S = x_ref.shape[1]
    cb = pl.program_id(1)
    x = x_ref[0]
    pad_ref[0:8, :] = jnp.zeros((8, LANE), F32)
    pad_ref[8:, :] = x
    w = w_ref[...]
    y = w[DN_CONV - 1:DN_CONV, :] * x
    for j in range(DN_CONV - 1):
        off = 8 - (DN_CONV - 1) + j
        y = y + w[j:j + 1, :] * pad_ref[off:off + S, :]
    y = y * jax.nn.sigmoid(y)
    nrm = lax.rsqrt(jnp.sum(y * y, axis=-1, keepdims=True) + EPS)
    scale = jnp.where(cb < DN_HEADS, nrm * (LANE ** -0.5), jnp.where(cb < 2 * DN_HEADS, nrm, 1.0))
    o_ref[0] = y * scale


def _dn_prep(proj3, conv_w, col0):
    B, S, _ = proj3.shape
    ncb = 3 * DN_HEADS
    return pl.pallas_call(
        _dn_prep_kernel,
        grid=(B, ncb),
        in_specs=[pl.BlockSpec((1, S, LANE), lambda b, c: (b, 0, col0 + c)),
                  pl.BlockSpec((DN_CONV, LANE), lambda b, c: (0, c))],
        out_specs=pl.BlockSpec((1, S, LANE), lambda b, c: (b, 0, c)),
        out_shape=jax.ShapeDtypeStruct((B, S, ncb * LANE), F32),
        scratch_shapes=[pltpu.VMEM((S + 8, LANE), F32)],
        compiler_params=_params("parallel", "parallel"),
        name="dn_prep",
    )(proj3, conv_w)


def _softplus(x):
    return jnp.maximum(x, 0.0) + jnp.log1p(jnp.exp(-jnp.abs(x)))


def _mm_split3(a, b):
    ah, al = _split_bf16(a)
    bh, bl = _split_bf16(b)
    return jnp.dot(jnp.concatenate([ah, ah, al], axis=1), jnp.concatenate([bh, bl, bh], axis=0),
                   preferred_element_type=F32)


def _dn_scan_kernel(q_ref, k_ref, v_ref, ba_ref, alr_ref, dtr_ref, z_ref, g_ref, o_ref, st_ref):
    C = DN_CHUNK
    H = DN_HEADS
    R = q_ref.shape[0]

    @pl.when(pl.program_id(1) == 0)
    def _():
        st_ref[...] = jnp.zeros_like(st_ref)

    ii = lax.broadcasted_iota(jnp.int32, (C, C), 0)
    jj = lax.broadcasted_iota(jnp.int32, (C, C), 1)
    incl = ii >= jj
    strict = ii > jj
    tril = incl.astype(F32)
    triu = (ii <= jj).astype(F32)
    gn = g_ref[...]

    beta = [jax.nn.sigmoid(ba_ref[r][:, 0:H]) for r in range(R)]
    g_all = jnp.concatenate([-jnp.exp(alr_ref[...]) * _softplus(ba_ref[r][:, H:2 * H] + dtr_ref[...])
                             for r in range(R)], axis=1)
    gc_all = jnp.dot(tril, g_all, precision=HI, preferred_element_type=F32)
    gc_all_t = lax.dot_general(g_all, triu, TN, precision=HI, preferred_element_type=F32)
    gc_col = [gc_all[:, r * H:(r + 1) * H] for r in range(R)]
    gc_row = [gc_all_t[r * H:(r + 1) * H, :] for r in range(R)]

    def mm(a, b):
        return jnp.dot(a.astype(BF16), b.astype(BF16), preferred_element_type=F32)

    units = [(r, h) for r in range(R) for h in range(H)]
    U = range(len(units))
    sl = [slice(h * LANE, (h + 1) * LANE) for _, h in units]
    q = [q_ref[r, :, sl[u]] for u, (r, h) in enumerate(units)]
    k = [k_ref[r, :, sl[u]] for u, (r, h) in enumerate(units)]
    gcc = [gc_col[r][:, h:h + 1] for r, h in units]
    bet = [beta[r][:, h:h + 1] for r, h in units]
    decay = [jnp.where(incl, jnp.exp(jnp.where(incl, gcc[u] - gc_row[r][h:h + 1, :], 0.0)), 0.0)
             for u, (r, h) in enumerate(units)]
    kb = [k[u] * bet[u] for u in U]
    vb = [v_ref[r, :, sl[u]] * bet[u] for u, (r, h) in enumerate(units)]
    kbf = [k[u].astype(BF16) for u in U]
    kq = [lax.dot_general(jnp.concatenate([kb[u], q[u]], axis=0).astype(BF16), kbf[u], NT,
                          preferred_element_type=F32) for u in U]
    kk = [kq[u][:C] for u in U]
    qk = [kq[u][C:] for u in U]
    n = [jnp.where(strict, kk[u] * decay[u], 0.0) for u in U]
    attn = [jnp.where(incl, qk[u] * decay[u], 0.0) for u in U]
    egc = [jnp.exp(gcc[u]) for u in U]
    x = [jnp.concatenate([vb[u], kb[u] * egc[u]], axis=1) for u in U]
    nx = [_mm_split3(n[u], x[u]) for u in U]
    p = [_mm_split3(n[u], n[u]) for u in U]
    x = [x[u] - nx[u] for u in U]
    levels = int(math.log2(C)) - 1
    for lvl in range(levels):
        px = [_mm_split3(p[u], x[u]) for u in U]
        if lvl + 1 < levels:
            p = [_mm_split3(p[u], p[u]) for u in U]
        x = [x[u] + px[u] for u in U]
    s = [st_ref[u] for u in U]
    sb = [s[u].astype(BF16) for u in U]
    wq = [mm(jnp.concatenate([x[u][:, LANE:], q[u] * egc[u]], axis=0), sb[u]) for u in U]
    ws = [wq[u][:C] for u in U]
    qs = [wq[u][C:] for u in U]
    v_new = [(x[u][:, :LANE] - ws[u]).astype(BF16) for u in U]
    av = [mm(attn[u], v_new[u]) for u in U]
    g_last = [gcc[u][C - 1:C, :] for u in U]
    kd = [(k[u] * jnp.exp(g_last[u] - gcc[u])).astype(BF16) for u in U]
    kv = [lax.dot_general(kd[u], v_new[u], TN, preferred_element_type=F32) for u in U]
    for u, (r, h) in enumerate(units):
        st_ref[u] = s[u] * jnp.exp(g_last[u]) + kv[u]
        zz = z_ref[r, :, sl[u]].astype(F32)
        o_ref[r, :, sl[u]] = (_rms(qs[u] + av[u], gn) * (zz * jax.nn.sigmoid(zz))).astype(o_ref.dtype)


DN_ROWS = 2


def _dn_scan(qkv, ba, a_log, dt_bias, proj3, z_col, onorm_g):
    B, S, _ = qkv.shape
    H = DN_HEADS
    C = DN_CHUNK
    W = H * LANE
    R = DN_ROWS
    tile = lambda col: pl.BlockSpec((R, C, W), lambda b, c: (b, c, col))
    small = lambda shape: pl.BlockSpec(shape, lambda b, c: (0,) * len(shape))
    return pl.pallas_call(
        _dn_scan_kernel,
        grid=(B // R, S // C),
        in_specs=[tile(0), tile(1), tile(2),
                  pl.BlockSpec((R, C, LANE), lambda b, c: (b, c, 0)),
                  small((1, H)), small((1, H)),
                  tile(z_col), small((1, LANE))],
        out_specs=tile(0),
        out_shape=jax.ShapeDtypeStruct((B, S, W), BF16),
        scratch_shapes=[pltpu.VMEM((R * H, LANE, LANE), F32)],
        compiler_params=_params("parallel", "arbitrary"),
        name="dn_scan",
    )(qkv, qkv, qkv, ba, a_log.reshape(1, H), dt_bias.reshape(1, H), proj3, onorm_g)


def _merge_kernel(x_ref, lo_ref, hi_ref, b1_ref, b2_ref, g0_ref, g1_ref, g2_ref, wb_ref, wo_ref, o_ref, *,
                  tiles_per_seq):
    in_first_half = pl.program_id(0) % tiles_per_seq < tiles_per_seq // 2
    moba = jnp.where(in_first_half, lo_ref[...], hi_ref[...])
    mixed = None
    for i, (b, g_ref) in enumerate(((moba, g0_ref), (b1_ref[...], g1_ref), (b2_ref[...], g2_ref))):
        bp = jnp.dot(b, wb_ref[i], preferred_element_type=F32)
        t = jax.nn.sigmoid(g_ref[...].astype(F32)) * bp
        mixed = t if mixed is None else mixed + t
    o_ref[...] = x_ref[...] + jnp.dot(mixed.astype(BF16), wo_ref[...], preferred_element_type=F32)


def _merge(x2, moba_lo, moba_hi, dn_out, mem_out, proj, gate_col, w_branch, w_out, seq_len):
    T, D = x2.shape
    tm = 256
    tps = seq_len // tm
    half = tps // 2
    row = lambda col: pl.BlockSpec((tm, D), lambda i: (i, col))
    lo_spec = pl.BlockSpec((tm, D), lambda i: ((i // tps) * half + jnp.minimum(i % tps, half - 1), 0))
    hi_spec = pl.BlockSpec((tm, D), lambda i: ((i // tps) * half + jnp.maximum(i % tps - half, 0), 0))
    return pl.pallas_call(
        functools.partial(_merge_kernel, tiles_per_seq=tps),
        grid=(T // tm,),
        in_specs=[row(0), lo_spec, hi_spec, row(0), row(0), row(gate_col), row(gate_col + 1), row(gate_col + 2),
                  pl.BlockSpec((3, D, D), lambda i: (0, 0, 0)),
                  pl.BlockSpec((D, D), lambda i: (0, 0))],
        out_specs=row(0),
        out_shape=jax.ShapeDtypeStruct((T, D), F32),
        compiler_params=_params("parallel"),
        name="merge",
    )(x2, moba_lo, moba_hi, dn_out, mem_out, proj, proj, proj, w_branch, w_out)


def _compare_exchange(a, b):
    if a is None:
        return b, None
    if b is None:
        return a, None
    return jnp.maximum(a, b), jnp.minimum(a, b)


def _bitonic_merge_desc(xs):
    n = len(xs)
    j = n // 2
    while j >= 1:
        for i in range(n):
            l = i ^ j
            if l > i:
                xs[i], xs[l] = _compare_exchange(xs[i], xs[l])
        j //= 2
    return xs


def _bitonic_sort_desc(xs):
    xs = list(xs)
    n = len(xs)
    k = 2
    while k <= n:
        j = k // 2
        while j >= 1:
            for i in range(n):
                l = i ^ j
                if l > i:
                    hi, lo = _compare_exchange(xs[i], xs[l])
                    xs[i], xs[l] = (hi, lo) if (i & k) == 0 else (lo, hi)
            j //= 2
        k *= 2
    return xs


def _top16_over_rows(pieces):
    K = PEER_TOPK
    xs = _bitonic_sort_desc(list(pieces) + [None] * (K - len(pieces)))
    for shift in (4, 2, 1):
        other = [None if x is None else pltpu.roll(x, shift, 0) for x in xs]
        merged = []
        for i in range(K):
            a, b = xs[i], other[K - 1 - i]
            merged.append(b if a is None else a if b is None else jnp.maximum(a, b))
        xs = _bitonic_merge_desc(merged)
    return xs


def _rows_from_replicated(vals, sub):
    out = vals[0]
    for r in range(1, 8):
        out = jnp.where(sub == r, vals[r], out)
    return out


def _peer_select_kernel(qp_ref, k1_ref, k2_ref, s1_ref, s2_ref, tau_ref, cc_ref, v2_ref, thr0_ref):
    tm = qp_ref.shape[0]
    K = PEER_KEYS
    q = qp_ref[...]
    s1_ref[0] = lax.dot_general(k1_ref[...], q[:, :K], NT, precision=HI, preferred_element_type=F32)
    s2_ref[0] = lax.dot_general(k2_ref[...], q[:, K:], NT, precision=HI, preferred_element_type=F32)
    sub = lax.broadcasted_iota(jnp.int32, (8, LANE), 0)
    ninf = -jnp.inf
    for c in range(tm // LANE):
        cs = slice(c * LANE, (c + 1) * LANE)
        v1 = _top16_over_rows([s1_ref[0, 8 * r:8 * r + 8, cs] for r in range(K // 8)])
        v2 = _top16_over_rows([s2_ref[0, 8 * r:8 * r + 8, cs] for r in range(K // 8)])
        v1lo, v1hi = _rows_from_replicated(v1[:8], sub), _rows_from_replicated(v1[8:], sub)
        v2lo, v2hi = _rows_from_replicated(v2[:8], sub), _rows_from_replicated(v2[8:], sub)
        cands = [
            v1[0] + v2lo, v1[0] + v2hi,
            v1[1] + v2lo,
            jnp.where(sub >= 2, v2[0] + v1lo, ninf), v2[0] + v1hi,
            jnp.where(sub >= 2, v2[1] + v1lo, ninf),
            jnp.where((sub >= 2) & (sub <= 4), v1[2] + v2lo, ninf),
            jnp.where((sub >= 2) & (sub <= 3), v1[3] + v2lo, ninf),
            jnp.where(sub == 2, v1[4] + v2lo, ninf),
        ]
        top = _top16_over_rows(cands)
        smax = top[0]
        z = jnp.exp(top[0] - smax)
        for t in top[1:]:
            z = z + jnp.exp(t - smax)
        tau = top[PEER_TOPK - 1]
        t0 = jnp.minimum(jnp.where(v1[0] + v2lo >= tau, v2lo, jnp.inf), jnp.where(v1[0] + v2hi >= tau, v2hi, jnp.inf))
        for shift in (4, 2, 1):
            t0 = jnp.minimum(t0, pltpu.roll(t0, shift, 0))
        thr0_ref[0, :, cs] = t0[0:1, :]
        v2_ref[0, :, cs] = v2lo
        tau_ref[0, :, cs] = tau[0:1, :]
        cc_ref[0, :, cs] = (smax + jnp.log(z))[0:1, :]


def _peer_select(qp, keys1, keys2):
    T = qp.shape[0]
    H, K = PEER_HEADS, PEER_KEYS
    tm = 512
    return pl.pallas_call(
        _peer_select_kernel,
        grid=(T // tm, H),
        in_specs=[pl.BlockSpec((tm, 2 * K), lambda i, h: (i, h)),
                  pl.BlockSpec((K, K), lambda i, h: (0, 0)),
                  pl.BlockSpec((K, K), lambda i, h: (0, 0))],
        out_specs=[pl.BlockSpec((1, K, tm), lambda i, h: (h, 0, i)),
                   pl.BlockSpec((1, K, tm), lambda i, h: (h, 0, i)),
                   pl.BlockSpec((1, 1, tm), lambda i, h: (h, 0, i)),
                   pl.BlockSpec((1, 1, tm), lambda i, h: (h, 0, i)),
                   pl.BlockSpec((1, PEER_V2, tm), lambda i, h: (h, 0, i)),
                   pl.BlockSpec((1, 1, tm), lambda i, h: (h, 0, i))],
        out_shape=[jax.ShapeDtypeStruct((H, K, T), F32), jax.ShapeDtypeStruct((H, K, T), F32),
                   jax.ShapeDtypeStruct((H, 1, T), F32), jax.ShapeDtypeStruct((H, 1, T), F32),
                   jax.ShapeDtypeStruct((H, PEER_V2, T), F32), jax.ShapeDtypeStruct((H, 1, T), F32)],
        compiler_params=_params("parallel", "parallel"),
        name="peer_select",
    )(qp, keys1, keys2)


GELU_C0 = math.sqrt(2.0 / math.pi)
GELU_C1 = GELU_C0 * 0.044715
PEER_ROWS = 256
PEER_LANES = 256


def _peer_dense_kernel(h_ref, u_ref, v_ref, s1_ref, s2_ref, tau_ref, cc_ref, v2_ref, thr0_ref, x_ref, o_ref,
                       e1_ref, e2_ref, thr_ref, rowb_ref, g_ref, w_ref, acc_ref):
    K = PEER_KEYS
    H = PEER_HEADS
    SUB = 8
    j = pl.program_id(1)
    te, tm = g_ref.shape
    n_a = te // K

    @pl.when(j == 0)
    def _():
        acc_ref[...] = jnp.zeros_like(acc_ref)
        for h in range(H):
            s1 = s1_ref[h]
            m1 = jnp.max(s1, axis=0, keepdims=True)
            e1_ref[h] = jnp.exp(s1 - m1)
            e2_ref[h] = 0.5 * jnp.exp(s2_ref[h] - (cc_ref[h] - m1))
            tau = tau_ref[h]
            thr = jnp.full((K, tm), jnp.inf, F32)
            for k in range(PEER_V2):
                v2k = v2_ref[h, k:k + 1, :]
                thr = jnp.where(s1 + v2k >= tau, v2k, thr)
            thr_ref[h] = jnp.where(s1 == m1, thr0_ref[h], thr)

    g_ref[...] = lax.dot_general(u_ref[...], h_ref[...], NT, preferred_element_type=F32)

    a0 = pl.multiple_of(j * n_a, n_a)
    for h in range(H):
        thrt = thr_ref[h, pl.ds(a0, n_a), :]
        e1t = e1_ref[h, pl.ds(a0, n_a), :]
        for r in range(n_a):
            rowb_ref[h, r, 0, :, 0:tm] = jnp.broadcast_to(thrt[r:r + 1, :], (SUB, tm))
            rowb_ref[h, r, 1, :, 0:tm] = jnp.broadcast_to(e1t[r:r + 1, :], (SUB, tm))

    def slab(sb, carry):
        b0 = pl.multiple_of(sb * SUB, SUB)
        for l0 in range(0, tm, PEER_LANES):
            ls = slice(l0, l0 + PEER_LANES)
            w = [None] * n_a
            for h in range(H):
                s2s = s2_ref[h, pl.ds(b0, SUB), ls]
                e2s = e2_ref[h, pl.ds(b0, SUB), ls]
                for r in range(n_a):
                    t = jnp.where(s2s >= rowb_ref[h, r, 0, :, ls], e2s * rowb_ref[h, r, 1, :, ls], 0.0)
                    w[r] = t if w[r] is None else w[r] + t
            for r in range(n_a):
                w_ref[r, pl.ds(b0, SUB), ls] = w[r]
        return carry

    lax.fori_loop(0, K // SUB, slab, 0, unroll=4)

    total = None
    for c in range(te // PEER_ROWS):
        rows = slice(c * PEER_ROWS, (c + 1) * PEER_ROWS)
        x = g_ref[rows, :]
        th = jnp.tanh(x * (GELU_C0 + GELU_C1 * (x * x)))
        w = w_ref[c * (PEER_ROWS // K):(c + 1) * (PEER_ROWS // K)].reshape(PEER_ROWS, tm)
        wa = (w * (x + x * th)).astype(BF16)
        d = lax.dot_general(v_ref[rows, :], wa, TN, preferred_element_type=F32)
        total = d if total is None else total + d
    acc_ref[...] += total

    @pl.when(j == pl.num_programs(1) - 1)
    def _():
        o_ref[...] = x_ref[...] + acc_ref[...].T


def _peer_dense(h2, u, v, s1, s2, tau, cc, v2, thr0, x1):
    T, D = h2.shape
    E = u.shape[0]
    H, K = PEER_HEADS, PEER_KEYS
    tm, te = 512, 1024
    return pl.pallas_call(
        _peer_dense_kernel,
        grid=(T // tm, E // te),
        in_specs=[pl.BlockSpec((tm, D), lambda i, j: (i, 0)),
                  pl.BlockSpec((te, D), lambda i, j: (j, 0)),
                  pl.BlockSpec((te, D), lambda i, j: (j, 0)),
                  pl.BlockSpec((H, K, tm), lambda i, j: (0, 0, i)),
                  pl.BlockSpec((H, K, tm), lambda i, j: (0, 0, i)),
                  pl.BlockSpec((H, 1, tm), lambda i, j: (0, 0, i)),
                  pl.BlockSpec((H, 1, tm), lambda i, j: (0, 0, i)),
                  pl.BlockSpec((H, PEER_V2, tm), lambda i, j: (0, 0, i)),
                  pl.BlockSpec((H, 1, tm), lambda i, j: (0, 0, i)),
                  pl.BlockSpec((tm, D), lambda i, j: (i, 0))],
        out_specs=pl.BlockSpec((tm, D), lambda i, j: (i, 0)),
        out_shape=jax.ShapeDtypeStruct((T, D), F32),
        scratch_shapes=[pltpu.VMEM((H, K, tm), F32),
                        pltpu.VMEM((H, K, tm), F32),
                        pltpu.VMEM((H, K, tm), F32),
                        pltpu.VMEM((H, te // K, 2, 8, tm + LANE), F32),
                        pltpu.VMEM((te, tm), F32),
                        pltpu.VMEM((te // K, K, tm), F32),
                        pltpu.VMEM((D, tm), F32)],
        compiler_params=_params("parallel", "arbitrary"),
        name="peer_dense",
    )(h2, u, v, s1, s2, tau, cc, v2, thr0, x1)


def _rope_tables(S):
    inv = 1.0 / (ROPE_THETA ** (np.arange(0, LANE, 2, dtype=np.float64) / LANE))
    ang = np.arange(S, dtype=np.float64)[:, None] * inv[None, :]
    cos, sin = np.cos(ang), np.sin(ang)
    return (jnp.asarray(np.concatenate([cos, cos], axis=1), F32),
            jnp.asarray(np.concatenate([-sin, sin], axis=1), F32))


def _layer(x, mem, attn_norm_g, mem_norm_g, ffn_norm_g, w_in, moba_q_norm_g, moba_k_norm_g,
           dn_conv_w, dn_a_log, dn_dt_bias, dn_out_norm_g, w_mem_kv, mem_q_norm_g, mem_k_norm_g,
           w_branch, w_out, peer_w_q, peer_keys1, peer_keys2, peer_u, peer_v):
    B, S, D = x.shape
    T = B * S
    x2 = x.reshape(T, D)
    row = lambda g: g.reshape(1, -1)

    moba_w, dn_w = 3 * MOBA_HEADS * LANE, 3 * DN_HEADS * LANE
    n_small = 2 * DN_HEADS
    o_small = moba_w + dn_w
    qk_w = 2 * MOBA_HEADS * LANE
    o_z = o_small + n_small
    w_f32 = jnp.concatenate([w_in[:, :qk_w], w_in[:, moba_w:o_small], w_in[:, o_z + D:o_z + 2 * D]],
                            axis=1).astype(BF16)
    w_b16 = jnp.concatenate([w_in[:, qk_w:moba_w], w_in[:, o_z:o_z + D], w_in[:, o_z + 2 * D:]],
                            axis=1).astype(BF16)
    w_small = jnp.pad(w_in[:, o_small:o_z], ((0, 0), (0, LANE - n_small)))
    proj, small = _norm_matmul(x2, row(attn_norm_g), w_f32, tm=1024, tn=1024, w_side=w_small, name="in_proj")
    projb = _norm_matmul(x2, row(attn_norm_g), w_b16, tm=1024, tn=1024, out_dtype=BF16, name="in_proj_bf16")
    proj3 = proj.reshape(B, S, -1)
    projb3 = projb.reshape(B, S, -1)
    dn_col = qk_w // LANE
    memq_col = (qk_w + dn_w) // MEM_HEAD_DIM
    z_col = 1
    gate_col = 2

    cos2, sin2 = _rope_tables(S)
    mq, mk, kmean = _moba_prep(proj3, row(moba_q_norm_g), row(moba_k_norm_g), cos2, sin2)
    moba_lo, moba_hi = _moba_attn(_moba_gate(mq, kmean), mk, projb3)

    dn_qkv = _dn_prep(proj3, dn_conv_w, dn_col)
    dn_out = _dn_scan(dn_qkv, small.reshape(B, S, LANE), dn_a_log, dn_dt_bias, projb3, z_col,
                      row(dn_out_norm_g))

    M = mem.shape[1]
    kv = _norm_matmul(mem.reshape(B * M, D), row(mem_norm_g), w_mem_kv.astype(BF16), tm=B * M, tn=512, name="mem_kv")
    mem_out = _mem_attn(proj3, kv.reshape(B, M, -1), row(mem_q_norm_g), row(mem_k_norm_g), memq_col)

    x1 = _merge(x2, moba_lo.reshape(T // 2, D), moba_hi.reshape(T // 2, D), dn_out.reshape(T, D),
                mem_out.reshape(T, D), projb, gate_col, w_branch.astype(BF16), w_out.astype(BF16), S)

    qp, h2 = _norm_matmul(x1, row(ffn_norm_g), peer_w_q.astype(BF16), tm=1024, tn=1024, emit_h=True, name="peer_query")
    s1, s2, tau, cc, v2, thr0 = _peer_select(qp, peer_keys1, peer_keys2)
    out = _peer_dense(h2, peer_u.astype(BF16), peer_v.astype(BF16), s1, s2, tau, cc, v2, thr0, x1)
    return out.reshape(B, S, D)


def kernel(x, mem, attn_norm_g, mem_norm_g, ffn_norm_g, w_in, moba_q_norm_g, moba_k_norm_g, dn_conv_w, dn_a_log, dn_dt_bias, dn_out_norm_g, w_mem_kv, mem_q_norm_g, mem_k_norm_g, w_branch, w_out, peer_w_q, peer_keys1, peer_keys2, peer_u, peer_v):
    for l in range(w_in.shape[0]):
        x = _layer(x, mem, attn_norm_g[l], mem_norm_g[l], ffn_norm_g[l], w_in[l], moba_q_norm_g[l],
                   moba_k_norm_g[l], dn_conv_w[l], dn_a_log[l], dn_dt_bias[l], dn_out_norm_g[l],
                   w_mem_kv[l], mem_q_norm_g[l], mem_k_norm_g[l], w_branch[l], w_out[l], peer_w_q[l],
                   peer_keys1[l], peer_keys2[l], peer_u[l], peer_v[l])
    return x
```

```python
import functools
import math

import jax
import jax.numpy as jnp
import numpy as np
from jax import lax
from jax.experimental import pallas as pl
from jax.experimental.pallas import tpu as pltpu

F32 = jnp.float32
BF16 = jnp.bfloat16
HI = lax.Precision.HIGHEST
EPS = 1e-6
ROPE_THETA = 10000.0
NT = (((1,), (1,)), ((), ()))
TN = (((0,), (0,)), ((), ()))

LANE = 128
MOBA_HEADS = 8
MOBA_BLOCK = 256
MOBA_TOPK = 3
DN_HEADS = 8
DN_CHUNK = 64
DN_CONV = 4
MEM_HEADS = 4
MEM_HEAD_DIM = 256
PEER_HEADS = 8
PEER_KEYS = 128
PEER_TOPK = 16
PEER_V2 = 8

VMEM_LIMIT = 56 * 1024 * 1024


def _params(*sem):
    return pltpu.CompilerParams(dimension_semantics=sem, vmem_limit_bytes=VMEM_LIMIT)


def _rms(x, g):
    ms = jnp.mean(x * x, axis=-1, keepdims=True)
    return x * lax.rsqrt(ms + EPS) * g


def _split_bf16(a):
    hi = a.astype(BF16)
    return hi, (a - hi.astype(F32)).astype(BF16)


def _norm_matmul_kernel(*refs, precision, emit_h, with_side):
    x_ref, g_ref, w_ref = refs[:3]
    refs = list(refs[3:])
    ws_ref = refs.pop(0) if with_side else None
    o_ref = refs.pop(0)
    ho_ref = refs.pop(0) if emit_h else None
    so_ref = refs.pop(0) if with_side else None
    (h_ref,) = refs

    @pl.when(pl.program_id(1) == 0)
    def _():
        h = _rms(x_ref[...], g_ref[...])
        h_ref[...] = h.astype(h_ref.dtype)
        if emit_h:
            ho_ref[...] = h.astype(ho_ref.dtype)
        if with_side:
            hh, hl = _split_bf16(h)
            ws = ws_ref[...]
            n = ws.shape[1] // 2
            r = jnp.dot(hh, ws, preferred_element_type=F32)
            so_ref[...] = r[:, :n] + r[:, n:] + jnp.dot(hl, ws[:, :n], preferred_element_type=F32)

    o_ref[...] = jnp.dot(h_ref[...], w_ref[...], precision=precision,
                         preferred_element_type=F32).astype(o_ref.dtype)


def _norm_matmul(x, g, w, *, tm, tn, name, out_dtype=F32, precision=None, emit_h=False, w_side=None):
    T, D = x.shape
    N = w.shape[1]
    with_side = w_side is not None
    operands = [x, g, w]
    in_specs = [pl.BlockSpec((tm, D), lambda i, j: (i, 0)),
                pl.BlockSpec((1, D), lambda i, j: (0, 0)),
                pl.BlockSpec((D, tn), lambda i, j: (0, j))]
    out_shape = [jax.ShapeDtypeStruct((T, N), out_dtype)]
    out_specs = [pl.BlockSpec((tm, tn), lambda i, j: (i, j))]
    if with_side:
        operands.append(jnp.concatenate(_split_bf16(w_side), axis=1))
        in_specs.append(pl.BlockSpec((D, 2 * w_side.shape[1]), lambda i, j: (0, 0)))
    if emit_h:
        out_shape.append(jax.ShapeDtypeStruct((T, D), BF16))
        out_specs.append(pl.BlockSpec((tm, D), lambda i, j: (i, 0)))
    if with_side:
        out_shape.append(jax.ShapeDtypeStruct((T, w_side.shape[1]), F32))
        out_specs.append(pl.BlockSpec((tm, w_side.shape[1]), lambda i, j: (i, 0)))
    res = pl.pallas_call(
        functools.partial(_norm_matmul_kernel, precision=precision, emit_h=emit_h, with_side=with_side),
        grid=(T // tm, N // tn),
        in_specs=in_specs,
        out_specs=out_specs,
        out_shape=out_shape,
        scratch_shapes=[pltpu.VMEM((tm, D), w.dtype)],
        compiler_params=_params("parallel", "arbitrary"),
        name=name,
    )(*operands)
    return res if len(res) > 1 else res[0]


MOBA_NEG = -1e30


def _moba_prep_kernel(q_ref, k_ref, gq_ref, gk_ref, cos_ref, sin_ref, blk_ref, qo_ref, ko_ref, km_ref):
    cos = cos_ref[...]
    sin = sin_ref[...]

    def norm_rope(x, g):
        y = _rms(x, g)
        return y * cos + pltpu.roll(y, LANE // 2, 1) * sin

    q = norm_rope(q_ref[0], gq_ref[...])
    k = norm_rope(k_ref[0], gk_ref[...])
    ts = k.shape[0]
    nb = ts // MOBA_BLOCK
    qo_ref[0] = q
    ko_ref[0, :, 0:LANE] = k.astype(BF16)
    ko_ref[0, :, LANE:2 * LANE] = blk_ref[...]
    km_ref[0, 0] = jnp.mean(k.reshape(nb, MOBA_BLOCK, LANE), axis=1)


def _moba_prep(proj3, gq, gk, cos2, sin2, blk_onehot):
    B, S, _ = proj3.shape
    H = MOBA_HEADS
    ts = 2048
    nb_t = ts // MOBA_BLOCK
    col = lambda off: pl.BlockSpec((1, ts, LANE), lambda b, h, s: (b, s, off + h))
    return pl.pallas_call(
        _moba_prep_kernel,
        grid=(B, H, S // ts),
        in_specs=[col(0), col(H),
                  pl.BlockSpec((1, LANE), lambda b, h, s: (0, 0)),
                  pl.BlockSpec((1, LANE), lambda b, h, s: (0, 0)),
                  pl.BlockSpec((ts, LANE), lambda b, h, s: (s, 0)),
                  pl.BlockSpec((ts, LANE), lambda b, h, s: (s, 0)),
                  pl.BlockSpec((ts, LANE), lambda b, h, s: (s, 0))],
        out_specs=[col(0),
                   pl.BlockSpec((1, ts, 2 * LANE), lambda b, h, s: (b, s, h)),
                   pl.BlockSpec((1, 1, nb_t, LANE), lambda b, h, s: (b, h, s, 0))],
        out_shape=[jax.ShapeDtypeStruct((B, S, H * LANE), F32),
                   jax.ShapeDtypeStruct((B, S, H * 2 * LANE), BF16),
                   jax.ShapeDtypeStruct((B, H, S // MOBA_BLOCK, LANE), F32)],
        compiler_params=_params("parallel", "parallel", "parallel"),
        name="moba_prep",
    )(proj3, proj3, gq, gk, cos2, sin2, blk_onehot)


def _moba_gate_kernel(q_ref, km_ref, o_ref):
    L = MOBA_BLOCK
    q = q_ref[0]
    km = km_ref[0, 0]
    nb = km.shape[0]
    S = q.shape[0]
    gate = lax.dot_general(km, q, NT, precision=HI, preferred_element_type=F32)
    row = lax.broadcasted_iota(jnp.int32, (nb, S), 0)
    own = lax.broadcasted_iota(jnp.int32, (nb, S), 1) // L
    rank = jnp.zeros((nb, S), jnp.int32)
    for m in range(nb - 1):
        gm = gate[m:m + 1, :]
        cnt = jnp.where(row > m, jnp.where(gm >= gate, 1, 0), jnp.where(gm > gate, 1, 0))
        rank = rank + jnp.where(own > m, cnt, 0)
    keep = jnp.where(row < own, jnp.where(rank < MOBA_TOPK, 1.0, 0.0), jnp.where(row == own, 1.0, 0.0))
    eye = (lax.broadcasted_iota(jnp.int32, (nb, LANE), 0)
           == lax.broadcasted_iota(jnp.int32, (nb, LANE), 1)).astype(BF16)
    keep_t = lax.dot_general(keep.astype(BF16), eye, TN, preferred_element_type=F32)
    o_ref[0, :, 0:LANE] = (q * (LANE ** -0.5 * math.log2(math.e))).astype(BF16)
    o_ref[0, :, LANE:2 * LANE] = jnp.where(keep_t > 0.5, 0.0, MOBA_NEG).astype(BF16)


def _moba_gate(q, kmean):
    B, S, W = q.shape
    H = MOBA_HEADS
    nb = kmean.shape[2]
    return pl.pallas_call(
        _moba_gate_kernel,
        grid=(B, H),
        in_specs=[pl.BlockSpec((1, S, LANE), lambda b, h: (b, 0, h)),
                  pl.BlockSpec((1, 1, nb, LANE), lambda b, h: (b, h, 0, 0))],
        out_specs=pl.BlockSpec((1, S, 2 * LANE), lambda b, h: (b, 0, h)),
        out_shape=jax.ShapeDtypeStruct((B, S, 2 * W), BF16),
        compiler_params=_params("parallel", "parallel"),
        name="moba_gate",
    )(q, kmean)


def _moba_attn_kernel(qa_ref, qb_ref, k_ref, v_ref, oa_ref, ob_ref, qaug_ref, s_ref):
    L = MOBA_BLOCK
    nb = k_ref.shape[1] // L
    half = nb // 2
    p = pl.program_id(2)
    tiles = (p, nb - 1 - p)
    qaug_ref[0] = qa_ref[0]
    qaug_ref[1] = qb_ref[0]

    def scores(qa, n):
        kb = k_ref[0, pl.ds(pl.multiple_of(n * L, L), L), :]
        return lax.dot_general(qa, kb, NT, preferred_element_type=F32)

    def half_max(s):
        return jnp.maximum(s[:, :LANE], s[:, LANE:])

    def dyn_slot(j):
        first = j < p
        return first, jnp.where(first, 0, 1), jnp.where(first, j, half + j - p)

    r = lax.broadcasted_iota(jnp.int32, (L, L), 0)
    c = lax.broadcasted_iota(jnp.int32, (L, L), 1)
    causal = jnp.where(c <= r, 0.0, MOBA_NEG)

    m = [None, None]
    for t in range(2):
        s = scores(qaug_ref[t], tiles[t]) + causal
        s_ref[nb - 1 + t] = s
        m[t] = half_max(s)
    for n in range(half):
        s = scores(qaug_ref[1], n)
        s_ref[n] = s
        m[1] = jnp.maximum(m[1], half_max(s))
    for j in range(half - 1):
        first, t, n = dyn_slot(j)
        s = scores(qaug_ref[t], n)
        s_ref[half + j] = s
        hm = half_max(s)
        m[0] = jnp.maximum(m[0], jnp.where(first, hm, MOBA_NEG))
        m[1] = jnp.maximum(m[1], jnp.where(first, MOBA_NEG, hm))
    mb = [jnp.broadcast_to(jnp.max(m[t], axis=-1, keepdims=True), (L, LANE)) for t in range(2)]

    def probs(k, mbt):
        s = s_ref[k]
        return jnp.concatenate([jnp.exp2(s[:, :LANE] - mbt), jnp.exp2(s[:, LANE:] - mbt)], axis=1)

    def value(n):
        return v_ref[0, pl.ds(pl.multiple_of(n * L, L), L), :]

    l = [None, None]
    acc = [None, None]
    for t in range(2):
        pr = probs(nb - 1 + t, mb[t])
        l[t] = pr[:, :LANE] + pr[:, LANE:]
        acc[t] = jnp.dot(pr.astype(BF16), value(tiles[t]), preferred_element_type=F32)
    for n in range(half):
        pr = probs(n, mb[1])
        l[1] = l[1] + (pr[:, :LANE] + pr[:, LANE:])
        acc[1] = acc[1] + jnp.dot(pr.astype(BF16), value(n), preferred_element_type=F32)
    for j in range(half - 1):
        first, t, n = dyn_slot(j)
        pr = probs(half + j, jnp.where(first, mb[0], mb[1]))
        ps = pr[:, :LANE] + pr[:, LANE:]
        d = jnp.dot(pr.astype(BF16), value(n), preferred_element_type=F32)
        l[0] = l[0] + jnp.where(first, ps, 0.0)
        l[1] = l[1] + jnp.where(first, 0.0, ps)
        acc[0] = acc[0] + jnp.where(first, d, 0.0)
        acc[1] = acc[1] + jnp.where(first, 0.0, d)

    for t, o_ref in enumerate((oa_ref, ob_ref)):
        o_ref[0] = (acc[t] / jnp.sum(l[t], axis=-1, keepdims=True)).astype(o_ref.dtype)


def _moba_attn(q_aug, k_aug, v):
    B, S, _ = v.shape
    H = MOBA_HEADS
    W = H * LANE
    L = MOBA_BLOCK
    nb = S // L
    half = nb // 2
    return pl.pallas_call(
        _moba_attn_kernel,
        grid=(B, H, half),
        in_specs=[pl.BlockSpec((1, L, 2 * LANE), lambda b, h, p: (b, p, h)),
                  pl.BlockSpec((1, L, 2 * LANE), lambda b, h, p: (b, nb - 1 - p, h)),
                  pl.BlockSpec((1, S, 2 * LANE), lambda b, h, p: (b, 0, h)),
                  pl.BlockSpec((1, S, LANE), lambda b, h, p: (b, 0, h))],
        out_specs=[pl.BlockSpec((1, L, LANE), lambda b, h, p: (b, p, h)),
                   pl.BlockSpec((1, L, LANE), lambda b, h, p: (b, half - 1 - p, h))],
        out_shape=[jax.ShapeDtypeStruct((B, S // 2, W), BF16), jax.ShapeDtypeStruct((B, S // 2, W), BF16)],
        scratch_shapes=[pltpu.VMEM((2, L, 2 * LANE), BF16),
                        pltpu.VMEM((nb + 1, L, L), F32)],
        compiler_params=_params("parallel", "parallel", "arbitrary"),
        name="moba_attn",
    )(q_aug, q_aug, k_aug, v)


def _mem_attn_kernel(q_ref, k_ref, v_ref, gq_ref, gk_ref, o_ref):
    cq = _rms(q_ref[0], gq_ref[...]).astype(BF16)
    ck = _rms(k_ref[0], gk_ref[...]).astype(BF16)
    s = lax.dot_general(cq, ck, NT, preferred_element_type=F32) * (MEM_HEAD_DIM ** -0.5)
    m = jnp.max(s, axis=-1, keepdims=True)
    p = jnp.exp(s - m)
    l = jnp.sum(p, axis=-1, keepdims=True)
    o = jnp.dot(p.astype(BF16), v_ref[0].astype(BF16), preferred_element_type=F32) / l
    o_ref[0] = o.astype(o_ref.dtype)


def _mem_attn(proj3, kv3, gq, gk, q_col):
    B, S, _ = proj3.shape
    M = kv3.shape[1]
    hd = MEM_HEAD_DIM
    tq = 1024
    return pl.pallas_call(
        _mem_attn_kernel,
        grid=(B, MEM_HEADS, S // tq),
        in_specs=[pl.BlockSpec((1, tq, hd), lambda b, h, i: (b, i, q_col + h)),
                  pl.BlockSpec((1, M, hd), lambda b, h, i: (b, 0, h)),
                  pl.BlockSpec((1, M, hd), lambda b, h, i: (b, 0, MEM_HEADS + h)),
                  pl.BlockSpec((1, hd), lambda b, h, i: (0, 0)),
                  pl.BlockSpec((1, hd), lambda b, h, i: (0, 0))],
        out_specs=pl.BlockSpec((1, tq, hd), lambda b, h, i: (b, i, h)),
        out_shape=jax.ShapeDtypeStruct((B, S, MEM_HEADS * hd), BF16),
        compiler_params=_params("parallel", "parallel", "parallel"),
        name="mem_attn",
    )(proj3, kv3, kv3, gq, gk)


def _dn_prep_kernel(x_ref, w_ref, o_ref, pad_ref):
    S = x_ref.shape[1]
    cb = pl.program_id(1)
    x = x_ref[0]
    pad_ref[0:8, :] = jnp.zeros((8, LANE), F32)
    pad_ref[8:, :] = x
    w = w_ref[...]
    y = w[DN_CONV - 1:DN_CONV, :] * x
    for j in range(DN_CONV - 1):
        off = 8 - (DN_CONV - 1) + j
        y = y + w[j:j + 1, :] * pad_ref[off:off + S, :]
    y = y * jax.nn.sigmoid(y)
    nrm = lax.rsqrt(jnp.sum(y * y, axis=-1, keepdims=True) + EPS)
    scale = jnp.where(cb < DN_HEADS, nrm * (LANE ** -0.5), jnp.where(cb < 2 * DN_HEADS, nrm, 1.0))
    o_ref[0] = y * scale


def _dn_prep(proj3, conv_w, col0):
    B, S, _ = proj3.shape
    ncb = 3 * DN_HEADS
    return pl.pallas_call(
        _dn_prep_kernel,
        grid=(B, ncb),
        in_specs=[pl.BlockSpec((1, S, LANE), lambda b, c: (b, 0, col0 + c)),
                  pl.BlockSpec((DN_CONV, LANE), lambda b, c: (0, c))],
        out_specs=pl.BlockSpec((1, S, LANE), lambda b, c: (b, 0, c)),
        out_shape=jax.ShapeDtypeStruct((B, S, ncb * LANE), F32),
        scratch_shapes=[pltpu.VMEM((S + 8, LANE), F32)],
        compiler_params=_params("parallel", "parallel"),
        name="dn_prep",
    )(proj3, conv_w)


def _softplus(x):
    return jnp.maximum(x, 0.0) + jnp.log1p(jnp.exp(-jnp.abs(x)))


def _mm_split3(a, b):
    ah, al = _split_bf16(a)
    bh, bl = _split_bf16(b)
    return jnp.dot(jnp.concatenate([ah, ah, al], axis=1), jnp.concatenate([bh, bl, bh], axis=0),
                   preferred_element_type=F32)


def _dn_scan_kernel(q_ref, k_ref, v_ref, ba_ref, alr_ref, dtr_ref, z_ref, g_ref, o_ref, st_ref):
    C = DN_CHUNK
    H = DN_HEADS
    R = q_ref.shape[0]

    @pl.when(pl.program_id(1) == 0)
    def _():
        st_ref[...] = jnp.zeros_like(st_ref)

    ii = lax.broadcasted_iota(jnp.int32, (C, C), 0)
    jj = lax.broadcasted_iota(jnp.int32, (C, C), 1)
    incl = ii >= jj
    strict = ii > jj
    tril = incl.astype(F32)
    triu = (ii <= jj).astype(F32)
    gn = g_ref[...]

    beta = [jax.nn.sigmoid(ba_ref[r][:, 0:H]) for r in range(R)]
    g_all = jnp.concatenate([-jnp.exp(alr_ref[...]) * _softplus(ba_ref[r][:, H:2 * H] + dtr_ref[...])
                             for r in range(R)], axis=1)
    gc_all = jnp.dot(tril, g_all, precision=HI, preferred_element_type=F32)
    gc_all_t = lax.dot_general(g_all, triu, TN, precision=HI, preferred_element_type=F32)
    gc_col = [gc_all[:, r * H:(r + 1) * H] for r in range(R)]
    gc_row = [gc_all_t[r * H:(r + 1) * H, :] for r in range(R)]

    def mm(a, b):
        return jnp.dot(a.astype(BF16), b.astype(BF16), preferred_element_type=F32)

    units = [(r, h) for r in range(R) for h in range(H)]
    U = range(len(units))
    sl = [slice(h * LANE, (h + 1) * LANE) for _, h in units]
    q = [q_ref[r, :, sl[u]] for u, (r, h) in enumerate(units)]
    k = [k_ref[r, :, sl[u]] for u, (r, h) in enumerate(units)]
    gcc = [gc_col[r][:, h:h + 1] for r, h in units]
    bet = [beta[r][:, h:h + 1] for r, h in units]
    decay = [jnp.where(incl, jnp.exp(jnp.where(incl, gcc[u] - gc_row[r][h:h + 1, :], 0.0)), 0.0)
             for u, (r, h) in enumerate(units)]
    kb = [k[u] * bet[u] for u in U]
    vb = [v_ref[r, :, sl[u]] * bet[u] for u, (r, h) in enumerate(units)]
    kbf = [k[u].astype(BF16) for u in U]
    kq = [lax.dot_general(jnp.concatenate([kb[u], q[u]], axis=0).astype(BF16), kbf[u], NT,
                          preferred_element_type=F32) for u in U]
    kk = [kq[u][:C] for u in U]
    qk = [kq[u][C:] for u in U]
    n = [jnp.where(strict, kk[u] * decay[u], 0.0) for u in U]
    attn = [jnp.where(incl, qk[u] * decay[u], 0.0) for u in U]
    egc = [jnp.exp(gcc[u]) for u in U]
    x = [jnp.concatenate([vb[u], kb[u] * egc[u]], axis=1) for u in U]
    nx = [_mm_split3(n[u], x[u]) for u in U]
    p = [_mm_split3(n[u], n[u]) for u in U]
    x = [x[u] - nx[u] for u in U]
    levels = int(math.log2(C)) - 1
    for lvl in range(levels):
        px = [_mm_split3(p[u], x[u]) for u in U]
        if lvl + 1 < levels:
            p = [_mm_split3(p[u], p[u]) for u in U]
        x = [x[u] + px[u] for u in U]
    s = [st_ref[u] for u in U]
    sb = [s[u].astype(BF16) for u in U]
    wq = [mm(jnp.concatenate([x[u][:, LANE:], q[u] * egc[u]], axis=0), sb[u]) for u in U]
    ws = [wq[u][:C] for u in U]
    qs = [wq[u][C:] for u in U]
    v_new = [(x[u][:, :LANE] - ws[u]).astype(BF16) for u in U]
    av = [mm(attn[u], v_new[u]) for u in U]
    g_last = [gcc[u][C - 1:C, :] for u in U]
    kd = [(k[u] * jnp.exp(g_last[u] - gcc[u])).astype(BF16) for u in U]
    kv = [lax.dot_general(kd[u], v_new[u], TN, preferred_element_type=F32) for u in U]
    for u, (r, h) in enumerate(units):
        st_ref[u] = s[u] * jnp.exp(g_last[u]) + kv[u]
        zz = z_ref[r, :, sl[u]].astype(F32)
        o_ref[r, :, sl[u]] = (_rms(qs[u] + av[u], gn) * (zz * jax.nn.sigmoid(zz))).astype(o_ref.dtype)


DN_ROWS = 2


def _dn_scan(qkv, ba, a_log, dt_bias, proj3, z_col, onorm_g):
    B, S, _ = qkv.shape
    H = DN_HEADS
    C = DN_CHUNK
    W = H * LANE
    R = DN_ROWS
    tile = lambda col: pl.BlockSpec((R, C, W), lambda b, c: (b, c, col))
    small = lambda shape: pl.BlockSpec(shape, lambda b, c: (0,) * len(shape))
    return pl.pallas_call(
        _dn_scan_kernel,
        grid=(B // R, S // C),
        in_specs=[tile(0), tile(1), tile(2),
                  pl.BlockSpec((R, C, LANE), lambda b, c: (b, c, 0)),
                  small((1, H)), small((1, H)),
                  tile(z_col), small((1, LANE))],
        out_specs=tile(0),
        out_shape=jax.ShapeDtypeStruct((B, S, W), BF16),
        scratch_shapes=[pltpu.VMEM((R * H, LANE, LANE), F32)],
        compiler_params=_params("parallel", "arbitrary"),
        name="dn_scan",
    )(qkv, qkv, qkv, ba, a_log.reshape(1, H), dt_bias.reshape(1, H), proj3, onorm_g)


def _merge_kernel(x_ref, lo_ref, hi_ref, b1_ref, b2_ref, g0_ref, g1_ref, g2_ref, wb_ref, wo_ref, o_ref, *,
                  tiles_per_seq):
    in_first_half = pl.program_id(0) % tiles_per_seq < tiles_per_seq // 2
    moba = jnp.where(in_first_half, lo_ref[...], hi_ref[...])
    mixed = None
    for i, (b, g_ref) in enumerate(((moba, g0_ref), (b1_ref[...], g1_ref), (b2_ref[...], g2_ref))):
        bp = jnp.dot(b, wb_ref[i], preferred_element_type=F32)
        t = jax.nn.sigmoid(g_ref[...].astype(F32)) * bp
        mixed = t if mixed is None else mixed + t
    o_ref[...] = x_ref[...] + jnp.dot(mixed.astype(BF16), wo_ref[...], preferred_element_type=F32)


def _merge(x2, moba_lo, moba_hi, dn_out, mem_out, proj, gate_col, w_branch, w_out, seq_len):
    T, D = x2.shape
    tm = 256
    tps = seq_len // tm
    half = tps // 2
    row = lambda col: pl.BlockSpec((tm, D), lambda i: (i, col))
    lo_spec = pl.BlockSpec((tm, D), lambda i: ((i // tps) * half + jnp.minimum(i % tps, half - 1), 0))
    hi_spec = pl.BlockSpec((tm, D), lambda i: ((i // tps) * half + jnp.maximum(i % tps - half, 0), 0))
    return pl.pallas_call(
        functools.partial(_merge_kernel, tiles_per_seq=tps),
        grid=(T // tm,),
        in_specs=[row(0), lo_spec, hi_spec, row(0), row(0), row(gate_col), row(gate_col + 1), row(gate_col + 2),
                  pl.BlockSpec((3, D, D), lambda i: (0, 0, 0)),
                  pl.BlockSpec((D, D), lambda i: (0, 0))],
        out_specs=row(0),
        out_shape=jax.ShapeDtypeStruct((T, D), F32),
        compiler_params=_params("parallel"),
        name="merge",
    )(x2, moba_lo, moba_hi, dn_out, mem_out, proj, proj, proj, w_branch, w_out)


def _compare_exchange(a, b):
    if a is None:
        return b, None
    if b is None:
        return a, None
    return jnp.maximum(a, b), jnp.minimum(a, b)


def _bitonic_merge_desc(xs):
    n = len(xs)
    j = n // 2
    while j >= 1:
        for i in range(n):
            l = i ^ j
            if l > i:
                xs[i], xs[l] = _compare_exchange(xs[i], xs[l])
        j //= 2
    return xs


def _bitonic_sort_desc(xs):
    xs = list(xs)
    n = len(xs)
    k = 2
    while k <= n:
        j = k // 2
        while j >= 1:
            for i in range(n):
                l = i ^ j
                if l > i:
                    hi, lo = _compare_exchange(xs[i], xs[l])
                    xs[i], xs[l] = (hi, lo) if (i & k) == 0 else (lo, hi)
            j //= 2
        k *= 2
    return xs


def _top16_over_rows(pieces):
    K = PEER_TOPK
    xs = _bitonic_sort_desc(list(pieces) + [None] * (K - len(pieces)))
    for shift in (4, 2, 1):
        other = [None if x is None else pltpu.roll(x, shift, 0) for x in xs]
        merged = []
        for i in range(K):
            a, b = xs[i], other[K - 1 - i]
            merged.append(b if a is None else a if b is None else jnp.maximum(a, b))
        xs = _bitonic_merge_desc(merged)
    return xs


def _rows_from_replicated(vals, sub):
    out = vals[0]
    for r in range(1, 8):
        out = jnp.where(sub == r, vals[r], out)
    return out


def _peer_select_kernel(qp_ref, k1_ref, k2_ref, s1_ref, s2_ref, tau_ref, cc_ref, v2_ref, thr0_ref):
    tm = qp_ref.shape[0]
    K = PEER_KEYS
    q = qp_ref[...]
    s1_ref[0] = lax.dot_general(k1_ref[...], q[:, :K], NT, precision=HI, preferred_element_type=F32)
    s2_ref[0] = lax.dot_general(k2_ref[...], q[:, K:], NT, precision=HI, preferred_element_type=F32)
    sub = lax.broadcasted_iota(jnp.int32, (8, LANE), 0)
    ninf = -jnp.inf
    for c in range(tm // LANE):
        cs = slice(c * LANE, (c + 1) * LANE)
        v1 = _top16_over_rows([s1_ref[0, 8 * r:8 * r + 8, cs] for r in range(K // 8)])
        v2 = _top16_over_rows([s2_ref[0, 8 * r:8 * r + 8, cs] for r in range(K // 8)])
        v1lo, v1hi = _rows_from_replicated(v1[:8], sub), _rows_from_replicated(v1[8:], sub)
        v2lo, v2hi = _rows_from_replicated(v2[:8], sub), _rows_from_replicated(v2[8:], sub)
        cands = [
            v1[0] + v2lo, v1[0] + v2hi,
            v1[1] + v2lo,
            jnp.where(sub >= 2, v2[0] + v1lo, ninf), v2[0] + v1hi,
            jnp.where(sub >= 2, v2[1] + v1lo, ninf),
            jnp.where((sub >= 2) & (sub <= 4), v1[2] + v2lo, ninf),
            jnp.where((sub >= 2) & (sub <= 3), v1[3] + v2lo, ninf),
            jnp.where(sub == 2, v1[4] + v2lo, ninf),
        ]
        top = _top16_over_rows(cands)
        smax = top[0]
        z = jnp.exp(top[0] - smax)
        for t in top[1:]:
            z = z + jnp.exp(t - smax)
        tau = top[PEER_TOPK - 1]
        t0 = jnp.minimum(jnp.where(v1[0] + v2lo >= tau, v2lo, jnp.inf), jnp.where(v1[0] + v2hi >= tau, v2hi, jnp.inf))
        for shift in (4, 2, 1):
            t0 = jnp.minimum(t0, pltpu.roll(t0, shift, 0))
        thr0_ref[0, :, cs] = t0[0:1, :]
        v2_ref[0, :, cs] = v2lo
        tau_ref[0, :, cs] = tau[0:1, :]
        cc_ref[0, :, cs] = (smax + jnp.log(z))[0:1, :]


def _peer_select(qp, keys1, keys2):
    T = qp.shape[0]
    H, K = PEER_HEADS, PEER_KEYS
    tm = 512
    return pl.pallas_call(
        _peer_select_kernel,
        grid=(T // tm, H),
        in_specs=[pl.BlockSpec((tm, 2 * K), lambda i, h: (i, h)),
                  pl.BlockSpec((K, K), lambda i, h: (0, 0)),
                  pl.BlockSpec((K, K), lambda i, h: (0, 0))],
        out_specs=[pl.BlockSpec((1, K, tm), lambda i, h: (h, 0, i)),
                   pl.BlockSpec((1, K, tm), lambda i, h: (h, 0, i)),
                   pl.BlockSpec((1, 1, tm), lambda i, h: (h, 0, i)),
                   pl.BlockSpec((1, 1, tm), lambda i, h: (h, 0, i)),
                   pl.BlockSpec((1, PEER_V2, tm), lambda i, h: (h, 0, i)),
                   pl.BlockSpec((1, 1, tm), lambda i, h: (h, 0, i))],
        out_shape=[jax.ShapeDtypeStruct((H, K, T), F32), jax.ShapeDtypeStruct((H, K, T), F32),
                   jax.ShapeDtypeStruct((H, 1, T), F32), jax.ShapeDtypeStruct((H, 1, T), F32),
                   jax.ShapeDtypeStruct((H, PEER_V2, T), F32), jax.ShapeDtypeStruct((H, 1, T), F32)],
        compiler_params=_params("parallel", "parallel"),
        name="peer_select",
    )(qp, keys1, keys2)


GELU_C0 = math.sqrt(2.0 / math.pi)
GELU_C1 = GELU_C0 * 0.044715
PEER_ROWS = 256
PEER_LANES = 256


def _peer_dense_kernel(h_ref, u_ref, v_ref, s1_ref, s2_ref, tau_ref, cc_ref, v2_ref, thr0_ref, x_ref, o_ref,
                       e1_ref, e2_ref, thr_ref, rowb_ref, g_ref, w_ref, acc_ref):
    K = PEER_KEYS
    H = PEER_HEADS
    SUB = 8
    j = pl.program_id(1)
    te, tm = g_ref.shape
    n_a = te // K

    @pl.when(j == 0)
    def _():
        acc_ref[...] = jnp.zeros_like(acc_ref)
        for h in range(H):
            s1 = s1_ref[h]
            m1 = jnp.max(s1, axis=0, keepdims=True)
            e1_ref[h] = jnp.exp(s1 - m1)
            e2_ref[h] = 0.5 * jnp.exp(s2_ref[h] - (cc_ref[h] - m1))
            tau = tau_ref[h]
            thr = jnp.full((K, tm), jnp.inf, F32)
            for k in range(PEER_V2):
                v2k = v2_ref[h, k:k + 1, :]
                thr = jnp.where(s1 + v2k >= tau, v2k, thr)
            thr_ref[h] = jnp.where(s1 == m1, thr0_ref[h], thr)

    g_ref[...] = lax.dot_general(u_ref[...], h_ref[...], NT, preferred_element_type=F32)

    a0 = pl.multiple_of(j * n_a, n_a)
    for h in range(H):
        thrt = thr_ref[h, pl.ds(a0, n_a), :]
        e1t = e1_ref[h, pl.ds(a0, n_a), :]
        for r in range(n_a):
            rowb_ref[h, r, 0, :, 0:tm] = jnp.broadcast_to(thrt[r:r + 1, :], (SUB, tm))
            rowb_ref[h, r, 1, :, 0:tm] = jnp.broadcast_to(e1t[r:r + 1, :], (SUB, tm))

    def slab(sb, carry):
        b0 = pl.multiple_of(sb * SUB, SUB)
        for l0 in range(0, tm, PEER_LANES):
            ls = slice(l0, l0 + PEER_LANES)
            w = [None] * n_a
            for h in range(H):
                s2s = s2_ref[h, pl.ds(b0, SUB), ls]
                e2s = e2_ref[h, pl.ds(b0, SUB), ls]
                for r in range(n_a):
                    t = jnp.where(s2s >= rowb_ref[h, r, 0, :, ls], e2s * rowb_ref[h, r, 1, :, ls], 0.0)
                    w[r] = t if w[r] is None else w[r] + t
            for r in range(n_a):
                w_ref[r, pl.ds(b0, SUB), ls] = w[r]
        return carry

    lax.fori_loop(0, K // SUB, slab, 0, unroll=4)

    total = None
    for c in range(te // PEER_ROWS):
        rows = slice(c * PEER_ROWS, (c + 1) * PEER_ROWS)
        x = g_ref[rows, :]
        th = jnp.tanh(x * (GELU_C0 + GELU_C1 * (x * x)))
        w = w_ref[c * (PEER_ROWS // K):(c + 1) * (PEER_ROWS // K)].reshape(PEER_ROWS, tm)
        wa = (w * (x + x * th)).astype(BF16)
        d = lax.dot_general(v_ref[rows, :], wa, TN, preferred_element_type=F32)
        total = d if total is None else total + d
    acc_ref[...] += total

    @pl.when(j == pl.num_programs(1) - 1)
    def _():
        o_ref[...] = x_ref[...] + acc_ref[...].T


def _peer_dense(h2, u, v, s1, s2, tau, cc, v2, thr0, x1):
    T, D = h2.shape
    E = u.shape[0]
    H, K = PEER_HEADS, PEER_KEYS
    tm, te = 512, 1024
    return pl.pallas_call(
        _peer_dense_kernel,
        grid=(T // tm, E // te),
        in_specs=[pl.BlockSpec((tm, D), lambda i, j: (i, 0)),
                  pl.BlockSpec((te, D), lambda i, j: (j, 0)),
                  pl.BlockSpec((te, D), lambda i, j: (j, 0)),
                  pl.BlockSpec((H, K, tm), lambda i, j: (0, 0, i)),
                  pl.BlockSpec((H, K, tm), lambda i, j: (0, 0, i)),
                  pl.BlockSpec((H, 1, tm), lambda i, j: (0, 0, i)),
                  pl.BlockSpec((H, 1, tm), lambda i, j: (0, 0, i)),
                  pl.BlockSpec((H, PEER_V2, tm), lambda i, j: (0, 0, i)),
                  pl.BlockSpec((H, 1, tm), lambda i, j: (0, 0, i)),
                  pl.BlockSpec((tm, D), lambda i, j: (i, 0))],
        out_specs=pl.BlockSpec((tm, D), lambda i, j: (i, 0)),
        out_shape=jax.ShapeDtypeStruct((T, D), F32),
        scratch_shapes=[pltpu.VMEM((H, K, tm), F32),
                        pltpu.VMEM((H, K, tm), F32),
                        pltpu.VMEM((H, K, tm), F32),
                        pltpu.VMEM((H, te // K, 2, 8, tm + LANE), F32),
                        pltpu.VMEM((te, tm), F32),
                        pltpu.VMEM((te // K, K, tm), F32),
                        pltpu.VMEM((D, tm), F32)],
        compiler_params=_params("parallel", "arbitrary"),
        name="peer_dense",
    )(h2, u, v, s1, s2, tau, cc, v2, thr0, x1)


def _rope_tables(S):
    inv = 1.0 / (ROPE_THETA ** (np.arange(0, LANE, 2, dtype=np.float64) / LANE))
    ang = np.arange(S, dtype=np.float64)[:, None] * inv[None, :]
    cos, sin = np.cos(ang), np.sin(ang)
    return (jnp.asarray(np.concatenate([cos, cos], axis=1), F32),
            jnp.asarray(np.concatenate([-sin, sin], axis=1), F32))


def _layer(x, mem, attn_norm_g, mem_norm_g, ffn_norm_g, w_in, moba_q_norm_g, moba_k_norm_g,
           dn_conv_w, dn_a_log, dn_dt_bias, dn_out_norm_g, w_mem_kv, mem_q_norm_g, mem_k_norm_g,
           w_branch, w_out, peer_w_q, peer_keys1, peer_keys2, peer_u, peer_v):
    B, S, D = x.shape
    T = B * S
    x2 = x.reshape(T, D)
    row = lambda g: g.reshape(1, -1)

    moba_w, dn_w = 3 * MOBA_HEADS * LANE, 3 * DN_HEADS * LANE
    n_small = 2 * DN_HEADS
    o_small = moba_w + dn_w
    qk_w = 2 * MOBA_HEADS * LANE
    o_z = o_small + n_small
    w_f32 = jnp.concatenate([w_in[:, :qk_w], w_in[:, moba_w:o_small], w_in[:, o_z + D:o_z + 2 * D]],
                            axis=1).astype(BF16)
    w_b16 = jnp.concatenate([w_in[:, qk_w:moba_w], w_in[:, o_z:o_z + D], w_in[:, o_z + 2 * D:]],
                            axis=1).astype(BF16)
    w_small = jnp.pad(w_in[:, o_small:o_z], ((0, 0), (0, LANE - n_small)))
    proj, small = _norm_matmul(x2, row(attn_norm_g), w_f32, tm=1024, tn=1024, w_side=w_small, name="in_proj")
    projb = _norm_matmul(x2, row(attn_norm_g), w_b16, tm=1024, tn=1024, out_dtype=BF16, name="in_proj_bf16")
    proj3 = proj.reshape(B, S, -1)
    projb3 = projb.reshape(B, S, -1)
    dn_col = qk_w // LANE
    memq_col = (qk_w + dn_w) // MEM_HEAD_DIM
    z_col = 1
    gate_col = 2

    cos2, sin2 = _rope_tables(S)
    blk_onehot = jnp.asarray(np.arange(S)[:, None] // MOBA_BLOCK == np.arange(LANE)[None, :], BF16)
    mq, mk, kmean = _moba_prep(proj3, row(moba_q_norm_g), row(moba_k_norm_g), cos2, sin2, blk_onehot)
    moba_lo, moba_hi = _moba_attn(_moba_gate(mq, kmean), mk, projb3)

    dn_qkv = _dn_prep(proj3, dn_conv_w, dn_col)
    dn_out = _dn_scan(dn_qkv, small.reshape(B, S, LANE), dn_a_log, dn_dt_bias, projb3, z_col,
                      row(dn_out_norm_g))

    M = mem.shape[1]
    kv = _norm_matmul(mem.reshape(B * M, D), row(mem_norm_g), w_mem_kv.astype(BF16), tm=B * M, tn=512, name="mem_kv")
    mem_out = _mem_attn(proj3, kv.reshape(B, M, -1), row(mem_q_norm_g), row(mem_k_norm_g), memq_col)

    x1 = _merge(x2, moba_lo.reshape(T // 2, D), moba_hi.reshape(T // 2, D), dn_out.reshape(T, D),
                mem_out.reshape(T, D), projb, gate_col, w_branch.astype(BF16), w_out.astype(BF16), S)

    qp, h2 = _norm_matmul(x1, row(ffn_norm_g), peer_w_q.astype(BF16), tm=1024, tn=1024, emit_h=True, name="peer_query")
    s1, s2, tau, cc, v2, thr0 = _peer_select(qp, peer_keys1, peer_keys2)
    out = _peer_dense(h2, peer_u.astype(BF16), peer_v.astype(BF16), s1, s2, tau, cc, v2, thr0, x1)
    return out.reshape(B, S, D)


def kernel(x, mem, attn_norm_g, mem_norm_g, ffn_norm_g, w_in, moba_q_norm_g, moba_k_norm_g, dn_conv_w, dn_a_log, dn_dt_bias, dn_out_norm_g, w_mem_kv, mem_q_norm_g, mem_k_norm_g, w_branch, w_out, peer_w_q, peer_keys1, peer_keys2, peer_u, peer_v):
    for l in range(w_in.shape[0]):
        x = _layer(x, mem, attn_norm_g[l], mem_norm_g[l], ffn_norm_g[l], w_in[l], moba_q_norm_g[l],
                   moba_k_norm_g[l], dn_conv_w[l], dn_a_log[l], dn_dt_bias[l], dn_out_norm_g[l],
                   w_mem_kv[l], mem_q_norm_g[l], mem_k_norm_g[l], w_branch[l], w_out[l], peer_w_q[l],
                   peer_keys1[l], peer_keys2[l], peer_u[l], peer_v[l])
    return x
```

```python
import functools
import math

import jax
import jax.numpy as jnp
import numpy as np
from jax import lax
from jax.experimental import pallas as pl
from jax.experimental.pallas import tpu as pltpu

F32 = jnp.float32
BF16 = jnp.bfloat16
HI = lax.Precision.HIGHEST
EPS = 1e-6
ROPE_THETA = 10000.0
NT = (((1,), (1,)), ((), ()))
TN = (((0,), (0,)), ((), ()))

LANE = 128
MOBA_HEADS = 8
MOBA_BLOCK = 256
MOBA_TOPK = 3
DN_HEADS = 8
DN_CHUNK = 64
DN_CONV = 4
MEM_HEADS = 4
MEM_HEAD_DIM = 256
PEER_HEADS = 8
PEER_KEYS = 128
PEER_TOPK = 16
PEER_V2 = 8

VMEM_LIMIT = 56 * 1024 * 1024


def _params(*sem):
    return pltpu.CompilerParams(dimension_semantics=sem, vmem_limit_bytes=VMEM_LIMIT)


def _rms(x, g):
    ms = jnp.mean(x * x, axis=-1, keepdims=True)
    return x * lax.rsqrt(ms + EPS) * g


def _split_bf16(a):
    hi = a.astype(BF16)
    return hi, (a - hi.astype(F32)).astype(BF16)


def _norm_matmul_kernel(*refs, precision, emit_h, with_side):
    x_ref, g_ref, w_ref = refs[:3]
    refs = list(refs[3:])
    ws_ref = refs.pop(0) if with_side else None
    o_ref = refs.pop(0)
    ho_ref = refs.pop(0) if emit_h else None
    so_ref = refs.pop(0) if with_side else None
    (h_ref,) = refs

    @pl.when(pl.program_id(1) == 0)
    def _():
        h = _rms(x_ref[...], g_ref[...])
        h_ref[...] = h.astype(h_ref.dtype)
        if emit_h:
            ho_ref[...] = h.astype(ho_ref.dtype)
        if with_side:
            hh, hl = _split_bf16(h)
            ws = ws_ref[...]
            n = ws.shape[1] // 2
            r = jnp.dot(hh, ws, preferred_element_type=F32)
            so_ref[...] = r[:, :n] + r[:, n:] + jnp.dot(hl, ws[:, :n], preferred_element_type=F32)

    o_ref[...] = jnp.dot(h_ref[...], w_ref[...], precision=precision,
                         preferred_element_type=F32).astype(o_ref.dtype)


def _norm_matmul(x, g, w, *, tm, tn, name, out_dtype=F32, precision=None, emit_h=False, w_side=None):
    T, D = x.shape
    N = w.shape[1]
    with_side = w_side is not None
    operands = [x, g, w]
    in_specs = [pl.BlockSpec((tm, D), lambda i, j: (i, 0)),
                pl.BlockSpec((1, D), lambda i, j: (0, 0)),
                pl.BlockSpec((D, tn), lambda i, j: (0, j))]
    out_shape = [jax.ShapeDtypeStruct((T, N), out_dtype)]
    out_specs = [pl.BlockSpec((tm, tn), lambda i, j: (i, j))]
    if with_side:
        operands.append(jnp.concatenate(_split_bf16(w_side), axis=1))
        in_specs.append(pl.BlockSpec((D, 2 * w_side.shape[1]), lambda i, j: (0, 0)))
    if emit_h:
        out_shape.append(jax.ShapeDtypeStruct((T, D), BF16))
        out_specs.append(pl.BlockSpec((tm, D), lambda i, j: (i, 0)))
    if with_side:
        out_shape.append(jax.ShapeDtypeStruct((T, w_side.shape[1]), F32))
        out_specs.append(pl.BlockSpec((tm, w_side.shape[1]), lambda i, j: (i, 0)))
    res = pl.pallas_call(
        functools.partial(_norm_matmul_kernel, precision=precision, emit_h=emit_h, with_side=with_side),
        grid=(T // tm, N // tn),
        in_specs=in_specs,
        out_specs=out_specs,
        out_shape=out_shape,
        scratch_shapes=[pltpu.VMEM((tm, D), w.dtype)],
        compiler_params=_params("parallel", "arbitrary"),
        name=name,
    )(*operands)
    return res if len(res) > 1 else res[0]


MOBA_NEG = -1e30


def _moba_prep_kernel(q_ref, k_ref, gq_ref, gk_ref, cos_ref, sin_ref, blk_ref, qo_ref, ko_ref, km_ref):
    cos = cos_ref[...]
    sin = sin_ref[...]

    def norm_rope(x, g):
        y = _rms(x, g)
        return y * cos + pltpu.roll(y, LANE // 2, 1) * sin

    q = norm_rope(q_ref[0], gq_ref[...])
    k = norm_rope(k_ref[0], gk_ref[...])
    ts = k.shape[0]
    nb = ts // MOBA_BLOCK
    qo_ref[0] = q
    ko_ref[0, :, 0:LANE] = k.astype(BF16)
    ko_ref[0, :, LANE:2 * LANE] = blk_ref[...]
    km_ref[0, 0] = jnp.mean(k.reshape(nb, MOBA_BLOCK, LANE), axis=1)


def _moba_prep(proj3, gq, gk, cos2, sin2, blk_onehot):
    B, S, _ = proj3.shape
    H = MOBA_HEADS
    ts = 2048
    nb_t = ts // MOBA_BLOCK
    col = lambda off: pl.BlockSpec((1, ts, LANE), lambda b, h, s: (b, s, off + h))
    return pl.pallas_call(
        _moba_prep_kernel,
        grid=(B, H, S // ts),
        in_specs=[col(0), col(H),
                  pl.BlockSpec((1, LANE), lambda b, h, s: (0, 0)),
                  pl.BlockSpec((1, LANE), lambda b, h, s: (0, 0)),
                  pl.BlockSpec((ts, LANE), lambda b, h, s: (s, 0)),
                  pl.BlockSpec((ts, LANE), lambda b, h, s: (s, 0)),
                  pl.BlockSpec((ts, LANE), lambda b, h, s: (s, 0))],
        out_specs=[col(0),
                   pl.BlockSpec((1, ts, 2 * LANE), lambda b, h, s: (b, s, h)),
                   pl.BlockSpec((1, 1, nb_t, LANE), lambda b, h, s: (b, h, s, 0))],
        out_shape=[jax.ShapeDtypeStruct((B, S, H * LANE), F32),
                   jax.ShapeDtypeStruct((B, S, H * 2 * LANE), BF16),
                   jax.ShapeDtypeStruct((B, H, S // MOBA_BLOCK, LANE), F32)],
        compiler_params=_params("parallel", "parallel", "parallel"),
        name="moba_prep",
    )(proj3, proj3, gq, gk, cos2, sin2, blk_onehot)


def _moba_gate_kernel(q_ref, km_ref, o_ref):
    L = MOBA_BLOCK
    q = q_ref[0]
    km = km_ref[0, 0]
    nb = km.shape[0]
    S = q.shape[0]
    gate = lax.dot_general(km, q, NT, precision=HI, preferred_element_type=F32)
    row = lax.broadcasted_iota(jnp.int32, (nb, S), 0)
    own = lax.broadcasted_iota(jnp.int32, (nb, S), 1) // L
    rank = jnp.zeros((nb, S), jnp.int32)
    for m in range(nb - 1):
        gm = gate[m:m + 1, :]
        cnt = jnp.where(row > m, jnp.where(gm >= gate, 1, 0), jnp.where(gm > gate, 1, 0))
        rank = rank + jnp.where(own > m, cnt, 0)
    keep = jnp.where(row < own, jnp.where(rank < MOBA_TOPK, 1.0, 0.0), jnp.where(row == own, 1.0, 0.0))
    eye = (lax.broadcasted_iota(jnp.int32, (nb, LANE), 0)
           == lax.broadcasted_iota(jnp.int32, (nb, LANE), 1)).astype(BF16)
    keep_t = lax.dot_general(keep.astype(BF16), eye, TN, preferred_element_type=F32)
    o_ref[0, :, 0:LANE] = (q * (LANE ** -0.5 * math.log2(math.e))).astype(BF16)
    o_ref[0, :, LANE:2 * LANE] = jnp.where(keep_t > 0.5, 0.0, MOBA_NEG).astype(BF16)


def _moba_gate(q, kmean):
    B, S, W = q.shape
    H = MOBA_HEADS
    nb = kmean.shape[2]
    return pl.pallas_call(
        _moba_gate_kernel,
        grid=(B, H),
        in_specs=[pl.BlockSpec((1, S, LANE), lambda b, h: (b, 0, h)),
                  pl.BlockSpec((1, 1, nb, LANE), lambda b, h: (b, h, 0, 0))],
        out_specs=pl.BlockSpec((1, S, 2 * LANE), lambda b, h: (b, 0, h)),
        out_shape=jax.ShapeDtypeStruct((B, S, 2 * W), BF16),
        compiler_params=_params("parallel", "parallel"),
        name="moba_gate",
    )(q, kmean)


def _moba_attn_kernel(qa_ref, qb_ref, k_ref, v_ref, oa_ref, ob_ref, qaug_ref, s_ref):
    L = MOBA_BLOCK
    nb = k_ref.shape[1] // L
    half = nb // 2
    p = pl.program_id(2)
    tiles = (p, nb - 1 - p)
    qaug_ref[0] = qa_ref[0]
    qaug_ref[1] = qb_ref[0]

    def scores(qa, n):
        kb = k_ref[0, pl.ds(pl.multiple_of(n * L, L), L), :]
        return lax.dot_general(qa, kb, NT, preferred_element_type=F32)

    def half_max(s):
        return jnp.maximum(s[:, :LANE], s[:, LANE:])

    def dyn_slot(j):
        first = j < p
        return first, jnp.where(first, 0, 1), jnp.where(first, j, half + j - p)

    r = lax.broadcasted_iota(jnp.int32, (L, L), 0)
    c = lax.broadcasted_iota(jnp.int32, (L, L), 1)
    causal = jnp.where(c <= r, 0.0, MOBA_NEG)

    m = [None, None]
    for t in range(2):
        s = scores(qaug_ref[t], tiles[t]) + causal
        s_ref[nb - 1 + t] = s
        m[t] = half_max(s)
    for n in range(half):
        s = scores(qaug_ref[1], n)
        s_ref[n] = s
        m[1] = jnp.maximum(m[1], half_max(s))
    for j in range(half - 1):
        first, t, n = dyn_slot(j)
        s = scores(qaug_ref[t], n)
        s_ref[half + j] = s
        hm = half_max(s)
        m[0] = jnp.maximum(m[0], jnp.where(first, hm, MOBA_NEG))
        m[1] = jnp.maximum(m[1], jnp.where(first, MOBA_NEG, hm))
    mb = [jnp.broadcast_to(jnp.max(m[t], axis=-1, keepdims=True), (L, LANE)) for t in range(2)]

    def probs(k, mbt):
        s = s_ref[k]
        return jnp.concatenate([jnp.exp2(s[:, :LANE] - mbt), jnp.exp2(s[:, LANE:] - mbt)], axis=1)

    def value(n):
        return v_ref[0, pl.ds(pl.multiple_of(n * L, L), L), :]

    l = [None, None]
    acc = [None, None]
    for t in range(2):
        pr = probs(nb - 1 + t, mb[t])
        l[t] = pr[:, :LANE] + pr[:, LANE:]
        acc[t] = jnp.dot(pr.astype(BF16), value(tiles[t]), preferred_element_type=F32)
    for n in range(half):
        pr = probs(n, mb[1])
        l[1] = l[1] + (pr[:, :LANE] + pr[:, LANE:])
        acc[1] = acc[1] + jnp.dot(pr.astype(BF16), value(n), preferred_element_type=F32)
    for j in range(half - 1):
        first, t, n = dyn_slot(j)
        pr = probs(half + j, jnp.where(first, mb[0], mb[1]))
        ps = pr[:, :LANE] + pr[:, LANE:]
        d = jnp.dot(pr.astype(BF16), value(n), preferred_element_type=F32)
        l[0] = l[0] + jnp.where(first, ps, 0.0)
        l[1] = l[1] + jnp.where(first, 0.0, ps)
        acc[0] = acc[0] + jnp.where(first, d, 0.0)
        acc[1] = acc[1] + jnp.where(first, 0.0, d)

    for t, o_ref in enumerate((oa_ref, ob_ref)):
        o_ref[0] = (acc[t] / jnp.sum(l[t], axis=-1, keepdims=True)).astype(o_ref.dtype)


def _moba_attn(q_aug, k_aug, v):
    B, S, _ = v.shape
    H = MOBA_HEADS
    W = H * LANE
    L = MOBA_BLOCK
    nb = S // L
    half = nb // 2
    return pl.pallas_call(
        _moba_attn_kernel,
        grid=(B, H, half),
        in_specs=[pl.BlockSpec((1, L, 2 * LANE), lambda b, h, p: (b, p, h)),
                  pl.BlockSpec((1, L, 2 * LANE), lambda b, h, p: (b, nb - 1 - p, h)),
                  pl.BlockSpec((1, S, 2 * LANE), lambda b, h, p: (b, 0, h)),
                  pl.BlockSpec((1, S, LANE), lambda b, h, p: (b, 0, h))],
        out_specs=[pl.BlockSpec((1, L, LANE), lambda b, h, p: (b, p, h)),
                   pl.BlockSpec((1, L, LANE), lambda b, h, p: (b, half - 1 - p, h))],
        out_shape=[jax.ShapeDtypeStruct((B, S // 2, W), BF16), jax.ShapeDtypeStruct((B, S // 2, W), BF16)],
        scratch_shapes=[pltpu.VMEM((2, L, 2 * LANE), BF16),
                        pltpu.VMEM((nb + 1, L, L), F32)],
        compiler_params=_params("parallel", "parallel", "arbitrary"),
        name="moba_attn",
    )(q_aug, q_aug, k_aug, v)


def _mem_attn_kernel(q_ref, k_ref, v_ref, gq_ref, gk_ref, o_ref):
    cq = _rms(q_ref[0], gq_ref[...]).astype(BF16)
    ck = _rms(k_ref[0], gk_ref[...]).astype(BF16)
    s = lax.dot_general(cq, ck, NT, preferred_element_type=F32) * (MEM_HEAD_DIM ** -0.5)
    m = jnp.max(s, axis=-1, keepdims=True)
    p = jnp.exp(s - m)
    l = jnp.sum(p, axis=-1, keepdims=True)
    o = jnp.dot(p.astype(BF16), v_ref[0].astype(BF16), preferred_element_type=F32) / l
    o_ref[0] = o.astype(o_ref.dtype)


def _mem_attn(proj3, kv3, gq, gk, q_col):
    B, S, _ = proj3.shape
    M = kv3.shape[1]
    hd = MEM_HEAD_DIM
    tq = 1024
    return pl.pallas_call(
        _mem_attn_kernel,
        grid=(B, MEM_HEADS, S // tq),
        in_specs=[pl.BlockSpec((1, tq, hd), lambda b, h, i: (b, i, q_col + h)),
                  pl.BlockSpec((1, M, hd), lambda b, h, i: (b, 0, h)),
                  pl.BlockSpec((1, M, hd), lambda b, h, i: (b, 0, MEM_HEADS + h)),
                  pl.BlockSpec((1, hd), lambda b, h, i: (0, 0)),
                  pl.BlockSpec((1, hd), lambda b, h, i: (0, 0))],
        out_specs=pl.BlockSpec((1, tq, hd), lambda b, h, i: (b, i, h)),
        out_shape=jax.ShapeDtypeStruct((B, S, MEM_HEADS * hd), BF16),
        compiler_params=_params("parallel", "parallel", "parallel"),
        name="mem_attn",
    )(proj3, kv3, kv3, gq, gk)


def _dn_prep_kernel(x_ref, w_ref, o_ref, pad_ref):
    S = x_ref.shape[1]
    cb = pl.program_id(1)
    x = x_ref[0]
    pad_ref[0:8, :] = jnp.zeros((8, LANE), F32)
    pad_ref[8:, :] = x
    w = w_ref[...]
    y = w[DN_CONV - 1:DN_CONV, :] * x
    for j in range(DN_CONV - 1):
        off = 8 - (DN_CONV - 1) + j
        y = y + w[j:j + 1, :] * pad_ref[off:off + S, :]
    y = y * jax.nn.sigmoid(y)
    nrm = lax.rsqrt(jnp.sum(y * y, axis=-1, keepdims=True) + EPS)
    scale = jnp.where(cb < DN_HEADS, nrm * (LANE ** -0.5), jnp.where(cb < 2 * DN_HEADS, nrm, 1.0))
    o_ref[0] = y * scale


def _dn_prep(proj3, conv_w, col0):
    B, S, _ = proj3.shape
    ncb = 3 * DN_HEADS
    return pl.pallas_call(
        _dn_prep_kernel,
        grid=(B, ncb),
        in_specs=[pl.BlockSpec((1, S, LANE), lambda b, c: (b, 0, col0 + c)),
                  pl.BlockSpec((DN_CONV, LANE), lambda b, c: (0, c))],
        out_specs=pl.BlockSpec((1, S, LANE), lambda b, c: (b, 0, c)),
        out_shape=jax.ShapeDtypeStruct((B, S, ncb * LANE), F32),
        scratch_shapes=[pltpu.VMEM((S + 8, LANE), F32)],
        compiler_params=_params("parallel", "parallel"),
        name="dn_prep",
    )(proj3, conv_w)


def _softplus(x):
    return jnp.maximum(x, 0.0) + jnp.log1p(jnp.exp(-jnp.abs(x)))


def _mm_split3(a, b):
    ah, al = _split_bf16(a)
    bh, bl = _split_bf16(b)
    return jnp.dot(jnp.concatenate([ah, ah, al], axis=1), jnp.concatenate([bh, bl, bh], axis=0),
                   preferred_element_type=F32)


def _dn_scan_kernel(q_ref, k_ref, v_ref, ba_ref, alr_ref, dtr_ref, z_ref, g_ref, o_ref, st_ref):
    C = DN_CHUNK
    H = DN_HEADS
    R = q_ref.shape[0]

    @pl.when(pl.program_id(1) == 0)
    def _():
        st_ref[...] = jnp.zeros_like(st_ref)

    ii = lax.broadcasted_iota(jnp.int32, (C, C), 0)
    jj = lax.broadcasted_iota(jnp.int32, (C, C), 1)
    incl = ii >= jj
    strict = ii > jj
    tril = incl.astype(F32)
    triu = (ii <= jj).astype(F32)
    gn = g_ref[...]

    beta = [jax.nn.sigmoid(ba_ref[r][:, 0:H]) for r in range(R)]
    g_all = jnp.concatenate([-jnp.exp(alr_ref[...]) * _softplus(ba_ref[r][:, H:2 * H] + dtr_ref[...])
                             for r in range(R)], axis=1)
    gc_all = jnp.dot(tril, g_all, precision=HI, preferred_element_type=F32)
    gc_all_t = lax.dot_general(g_all, triu, TN, precision=HI, preferred_element_type=F32)
    gc_col = [gc_all[:, r * H:(r + 1) * H] for r in range(R)]
    gc_row = [gc_all_t[r * H:(r + 1) * H, :] for r in range(R)]

    def mm(a, b):
        return jnp.dot(a.astype(BF16), b.astype(BF16), preferred_element_type=F32)

    units = [(r, h) for r in range(R) for h in range(H)]
    U = range(len(units))
    sl = [slice(h * LANE, (h + 1) * LANE) for _, h in units]
    q = [q_ref[r, :, sl[u]] for u, (r, h) in enumerate(units)]
    k = [k_ref[r, :, sl[u]] for u, (r, h) in enumerate(units)]
    gcc = [gc_col[r][:, h:h + 1] for r, h in units]
    bet = [beta[r][:, h:h + 1] for r, h in units]
    decay = [jnp.where(incl, jnp.exp(jnp.where(incl, gcc[u] - gc_row[r][h:h + 1, :], 0.0)), 0.0)
             for u, (r, h) in enumerate(units)]
    kb = [k[u] * bet[u] for u in U]
    vb = [v_ref[r, :, sl[u]] * bet[u] for u, (r, h) in enumerate(units)]
    kbf = [k[u].astype(BF16) for u in U]
    kq = [lax.dot_general(jnp.concatenate([kb[u], q[u]], axis=0).astype(BF16), kbf[u], NT,
                          preferred_element_type=F32) for u in U]
    kk = [kq[u][:C] for u in U]
    qk = [kq[u][C:] for u in U]
    n = [jnp.where(strict, kk[u] * decay[u], 0.0) for u in U]
    attn = [jnp.where(incl, qk[u] * decay[u], 0.0) for u in U]
    egc = [jnp.exp(gcc[u]) for u in U]
    x = [jnp.concatenate([vb[u], kb[u] * egc[u]], axis=1) for u in U]
    nx = [_mm_split3(n[u], x[u]) for u in U]
    p = [_mm_split3(n[u], n[u]) for u in U]
    x = [x[u] - nx[u] for u in U]
    levels = int(math.log2(C)) - 1
    for lvl in range(levels):
        px = [_mm_split3(p[u], x[u]) for u in U]
        if lvl + 1 < levels:
            p = [_mm_split3(p[u], p[u]) for u in U]
        x = [x[u] + px[u] for u in U]
    s = [st_ref[u] for u in U]
    sb = [s[u].astype(BF16) for u in U]
    wq = [mm(jnp.concatenate([x[u][:, LANE:], q[u] * egc[u]], axis=0), sb[u]) for u in U]
    ws = [wq[u][:C] for u in U]
    qs = [wq[u][C:] for u in U]
    v_new = [(x[u][:, :LANE] - ws[u]).astype(BF16) for u in U]
    av = [mm(attn[u], v_new[u]) for u in U]
    g_last = [gcc[u][C - 1:C, :] for u in U]
    kd = [(k[u] * jnp.exp(g_last[u] - gcc[u])).astype(BF16) for u in U]
    kv = [lax.dot_general(kd[u], v_new[u], TN, preferred_element_type=F32) for u in U]
    for u, (r, h) in enumerate(units):
        st_ref[u] = s[u] * jnp.exp(g_last[u]) + kv[u]
        zz = z_ref[r, :, sl[u]].astype(F32)
        o_ref[r, :, sl[u]] = (_rms(qs[u] + av[u], gn) * (zz * jax.nn.sigmoid(zz))).astype(o_ref.dtype)


DN_ROWS = 2


def _dn_scan(qkv, ba, a_log, dt_bias, proj3, z_col, onorm_g):
    B, S, _ = qkv.shape
    H = DN_HEADS
    C = DN_CHUNK
    W = H * LANE
    R = DN_ROWS
    tile = lambda col: pl.BlockSpec((R, C, W), lambda b, c: (b, c, col))
    small = lambda shape: pl.BlockSpec(shape, lambda b, c: (0,) * len(shape))
    return pl.pallas_call(
        _dn_scan_kernel,
        grid=(B // R, S // C),
        in_specs=[tile(0), tile(1), tile(2),
                  pl.BlockSpec((R, C, LANE), lambda b, c: (b, c, 0)),
                  small((1, H)), small((1, H)),
                  tile(z_col), small((1, LANE))],
        out_specs=tile(0),
        out_shape=jax.ShapeDtypeStruct((B, S, W), BF16),
        scratch_shapes=[pltpu.VMEM((R * H, LANE, LANE), F32)],
        compiler_params=_params("parallel", "arbitrary"),
        name="dn_scan",
    )(qkv, qkv, qkv, ba, a_log.reshape(1, H), dt_bias.reshape(1, H), proj3, onorm_g)


def _merge_kernel(x_ref, lo_ref, hi_ref, b1_ref, b2_ref, g0_ref, g1_ref, g2_ref, wb_ref, wo_ref, o_ref, *,
                  tiles_per_seq):
    in_first_half = pl.program_id(0) % tiles_per_seq < tiles_per_seq // 2
    moba = jnp.where(in_first_half, lo_ref[...], hi_ref[...])
    mixed = None
    for i, (b, g_ref) in enumerate(((moba, g0_ref), (b1_ref[...], g1_ref), (b2_ref[...], g2_ref))):
        bp = jnp.dot(b, wb_ref[i], preferred_element_type=F32)
        t = jax.nn.sigmoid(g_ref[...].astype(F32)) * bp
        mixed = t if mixed is None else mixed + t
    o_ref[...] = x_ref[...] + jnp.dot(mixed.astype(BF16), wo_ref[...], preferred_element_type=F32)


def _merge(x2, moba_lo, moba_hi, dn_out, mem_out, proj, gate_col, w_branch, w_out, seq_len):
    T, D = x2.shape
    tm = 256
    tps = seq_len // tm
    half = tps // 2
    row = lambda col: pl.BlockSpec((tm, D), lambda i: (i, col))
    lo_spec = pl.BlockSpec((tm, D), lambda i: ((i // tps) * half + jnp.minimum(i % tps, half - 1), 0))
    hi_spec = pl.BlockSpec((tm, D), lambda i: ((i // tps) * half + jnp.maximum(i % tps - half, 0), 0))
    return pl.pallas_call(
        functools.partial(_merge_kernel, tiles_per_seq=tps),
        grid=(T // tm,),
        in_specs=[row(0), lo_spec, hi_spec, row(0), row(0), row(gate_col), row(gate_col + 1), row(gate_col + 2),
                  pl.BlockSpec((3, D, D), lambda i: (0, 0, 0)),
                  pl.BlockSpec((D, D), lambda i: (0, 0))],
        out_specs=row(0),
        out_shape=jax.ShapeDtypeStruct((T, D), F32),
        compiler_params=_params("parallel"),
        name="merge",
    )(x2, moba_lo, moba_hi, dn_out, mem_out, proj, proj, proj, w_branch, w_out)


def _compare_exchange(a, b):
    if a is None:
        return b, None
    if b is None:
        return a, None
    return jnp.maximum(a, b), jnp.minimum(a, b)


def _bitonic_merge_desc(xs):
    n = len(xs)
    j = n // 2
    while j >= 1:
        for i in range(n):
            l = i ^ j
            if l > i:
                xs[i], xs[l] = _compare_exchange(xs[i], xs[l])
        j //= 2
    return xs


def _bitonic_sort_desc(xs):
    xs = list(xs)
    n = len(xs)
    k = 2
    while k <= n:
        j = k // 2
        while j >= 1:
            for i in range(n):
                l = i ^ j
                if l > i:
                    hi, lo = _compare_exchange(xs[i], xs[l])
                    xs[i], xs[l] = (hi, lo) if (i & k) == 0 else (lo, hi)
            j //= 2
        k *= 2
    return xs


def _top16_over_rows(pieces):
    K = PEER_TOPK
    xs = _bitonic_sort_desc(list(pieces) + [None] * (K - len(pieces)))
    for shift in (4, 2, 1):
        other = [None if x is None else pltpu.roll(x, shift, 0) for x in xs]
        merged = []
        for i in range(K):
            a, b = xs[i], other[K - 1 - i]
            merged.append(b if a is None else a if b is None else jnp.maximum(a, b))
        xs = _bitonic_merge_desc(merged)
    return xs


def _rows_from_replicated(vals, sub):
    out = vals[0]
    for r in range(1, 8):
        out = jnp.where(sub == r, vals[r], out)
    return out


def _peer_select_kernel(qp_ref, k1_ref, k2_ref, s1_ref, s2_ref, tau_ref, cc_ref, v2_ref, thr0_ref):
    tm = qp_ref.shape[0]
    K = PEER_KEYS
    q = qp_ref[...]
    s1_ref[0] = lax.dot_general(k1_ref[...], q[:, :K], NT, precision=HI, preferred_element_type=F32)
    s2_ref[0] = lax.dot_general(k2_ref[...], q[:, K:], NT, precision=HI, preferred_element_type=F32)
    sub = lax.broadcasted_iota(jnp.int32, (8, LANE), 0)
    ninf = -jnp.inf
    for c in range(tm // LANE):
        cs = slice(c * LANE, (c + 1) * LANE)
        v1 = _top16_over_rows([s1_ref[0, 8 * r:8 * r + 8, cs] for r in range(K // 8)])
        v2 = _top16_over_rows([s2_ref[0, 8 * r:8 * r + 8, cs] for r in range(K // 8)])
        v1lo, v1hi = _rows_from_replicated(v1[:8], sub), _rows_from_replicated(v1[8:], sub)
        v2lo, v2hi = _rows_from_replicated(v2[:8], sub), _rows_from_replicated(v2[8:], sub)
        cands = [
            v1[0] + v2lo, v1[0] + v2hi,
            v1[1] + v2lo,
            jnp.where(sub >= 2, v2[0] + v1lo, ninf), v2[0] + v1hi,
            jnp.where(sub >= 2, v2[1] + v1lo, ninf),
            jnp.where((sub >= 2) & (sub <= 4), v1[2] + v2lo, ninf),
            jnp.where((sub >= 2) & (sub <= 3), v1[3] + v2lo, ninf),
            jnp.where(sub == 2, v1[4] + v2lo, ninf),
        ]
        top = _top16_over_rows(cands)
        smax = top[0]
        z = jnp.exp(top[0] - smax)
        for t in top[1:]:
            z = z + jnp.exp(t - smax)
        tau = top[PEER_TOPK - 1]
        t0 = jnp.minimum(jnp.where(v1[0] + v2lo >= tau, v2lo, jnp.inf), jnp.where(v1[0] + v2hi >= tau, v2hi, jnp.inf))
        for shift in (4, 2, 1):
            t0 = jnp.minimum(t0, pltpu.roll(t0, shift, 0))
        thr0_ref[0, :, cs] = t0[0:1, :]
        v2_ref[0, :, cs] = v2lo
        tau_ref[0, :, cs] = tau[0:1, :]
        cc_ref[0, :, cs] = (smax + jnp.log(z))[0:1, :]


def _peer_select(qp, keys1, keys2):
    T = qp.shape[0]
    H, K = PEER_HEADS, PEER_KEYS
    tm = 512
    return pl.pallas_call(
        _peer_select_kernel,
        grid=(T // tm, H),
        in_specs=[pl.BlockSpec((tm, 2 * K), lambda i, h: (i, h)),
                  pl.BlockSpec((K, K), lambda i, h: (0, 0)),
                  pl.BlockSpec((K, K), lambda i, h: (0, 0))],
        out_specs=[pl.BlockSpec((1, K, tm), lambda i, h: (h, 0, i)),
                   pl.BlockSpec((1, K, tm), lambda i, h: (h, 0, i)),
                   pl.BlockSpec((1, 1, tm), lambda i, h: (h, 0, i)),
                   pl.BlockSpec((1, 1, tm), lambda i, h: (h, 0, i)),
                   pl.BlockSpec((1, PEER_V2, tm), lambda i, h: (h, 0, i)),
                   pl.BlockSpec((1, 1, tm), lambda i, h: (h, 0, i))],
        out_shape=[jax.ShapeDtypeStruct((H, K, T), F32), jax.ShapeDtypeStruct((H, K, T), F32),
                   jax.ShapeDtypeStruct((H, 1, T), F32), jax.ShapeDtypeStruct((H, 1, T), F32),
                   jax.ShapeDtypeStruct((H, PEER_V2, T), F32), jax.ShapeDtypeStruct((H, 1, T), F32)],
        compiler_params=_params("parallel", "parallel"),
        name="peer_select",
    )(qp, keys1, keys2)


GELU_C0 = math.sqrt(2.0 / math.pi)
GELU_C1 = GELU_C0 * 0.044715
PEER_ROWS = 256
PEER_LANES = 256


def _peer_dense_kernel(h_ref, u_ref, v_ref, s1_ref, s2_ref, tau_ref, cc_ref, v2_ref, thr0_ref, x_ref, o_ref,
                       e1_ref, e2_ref, thr_ref, rowb_ref, g_ref, w_ref, acc_ref):
    K = PEER_KEYS
    H = PEER_HEADS
    SUB = 8
    j = pl.program_id(1)
    te, tm = g_ref.shape
    n_a = te // K

    @pl.when(j == 0)
    def _():
        acc_ref[...] = jnp.zeros_like(acc_ref)
        for h in range(H):
            s1 = s1_ref[h]
            m1 = jnp.max(s1, axis=0, keepdims=True)
            e1_ref[h] = jnp.exp(s1 - m1)
            e2_ref[h] = 0.5 * jnp.exp(s2_ref[h] - (cc_ref[h] - m1))
            tau = tau_ref[h]
            thr = jnp.full((K, tm), jnp.inf, F32)
            for k in range(PEER_V2):
                v2k = v2_ref[h, k:k + 1, :]
                thr = jnp.where(s1 + v2k >= tau, v2k, thr)
            thr_ref[h] = jnp.where(s1 == m1, thr0_ref[h], thr)

    g_ref[...] = lax.dot_general(u_ref[...], h_ref[...], NT, preferred_element_type=F32)

    a0 = pl.multiple_of(j * n_a, n_a)
    for h in range(H):
        thrt = thr_ref[h, pl.ds(a0, n_a), :]
        e1t = e1_ref[h, pl.ds(a0, n_a), :]
        for r in range(n_a):
            rowb_ref[h, r, 0, :, 0:tm] = jnp.broadcast_to(thrt[r:r + 1, :], (SUB, tm))
            rowb_ref[h, r, 1, :, 0:tm] = jnp.broadcast_to(e1t[r:r + 1, :], (SUB, tm))

    def slab(sb, carry):
        b0 = pl.multiple_of(sb * SUB, SUB)
        for l0 in range(0, tm, PEER_LANES):
            ls = slice(l0, l0 + PEER_LANES)
            w = [None] * n_a
            for h in range(H):
                s2s = s2_ref[h, pl.ds(b0, SUB), ls]
                e2s = e2_ref[h, pl.ds(b0, SUB), ls]
                for r in range(n_a):
                    t = jnp.where(s2s >= rowb_ref[h, r, 0, :, ls], e2s * rowb_ref[h, r, 1, :, ls], 0.0)
                    w[r] = t if w[r] is None else w[r] + t
            for r in range(n_a):
                w_ref[r, pl.ds(b0, SUB), ls] = w[r]
        return carry

    lax.fori_loop(0, K // SUB, slab, 0, unroll=4)

    total = None
    for c in range(te // PEER_ROWS):
        rows = slice(c * PEER_ROWS, (c + 1) * PEER_ROWS)
        x = g_ref[rows, :]
        th = jnp.tanh(x * (GELU_C0 + GELU_C1 * (x * x)))
        w = w_ref[c * (PEER_ROWS // K):(c + 1) * (PEER_ROWS // K)].reshape(PEER_ROWS, tm)
        wa = (w * (x + x * th)).astype(BF16)
        d = lax.dot_general(v_ref[rows, :], wa, TN, preferred_element_type=F32)
        total = d if total is None else total + d
    acc_ref[...] += total

    @pl.when(j == pl.num_programs(1) - 1)
    def _():
        o_ref[...] = x_ref[...] + acc_ref[...].T


def _peer_dense(h2, u, v, s1, s2, tau, cc, v2, thr0, x1):
    T, D = h2.shape
    E = u.shape[0]
    H, K = PEER_HEADS, PEER_KEYS
    tm, te = 512, 2048
    once = dict(pipeline_mode=pl.Buffered(1))
    return pl.pallas_call(
        _peer_dense_kernel,
        grid=(T // tm, E // te),
        in_specs=[pl.BlockSpec((tm, D), lambda i, j: (i, 0), **once),
                  pl.BlockSpec((te, D), lambda i, j: (j, 0)),
                  pl.BlockSpec((te, D), lambda i, j: (j, 0)),
                  pl.BlockSpec((H, K, tm), lambda i, j: (0, 0, i), **once),
                  pl.BlockSpec((H, K, tm), lambda i, j: (0, 0, i), **once),
                  pl.BlockSpec((H, 1, tm), lambda i, j: (0, 0, i), **once),
                  pl.BlockSpec((H, 1, tm), lambda i, j: (0, 0, i), **once),
                  pl.BlockSpec((H, PEER_V2, tm), lambda i, j: (0, 0, i), **once),
                  pl.BlockSpec((H, 1, tm), lambda i, j: (0, 0, i), **once),
                  pl.BlockSpec((tm, D), lambda i, j: (i, 0), **once)],
        out_specs=pl.BlockSpec((tm, D), lambda i, j: (i, 0)),
        out_shape=jax.ShapeDtypeStruct((T, D), F32),
        scratch_shapes=[pltpu.VMEM((H, K, tm), F32),
                        pltpu.VMEM((H, K, tm), F32),
                        pltpu.VMEM((H, K, tm), F32),
                        pltpu.VMEM((H, te // K, 2, 8, tm + LANE), F32),
                        pltpu.VMEM((te, tm), F32),
                        pltpu.VMEM((te // K, K, tm), F32),
                        pltpu.VMEM((D, tm), F32)],
        compiler_params=_params("parallel", "arbitrary"),
        name="peer_dense",
    )(h2, u, v, s1, s2, tau, cc, v2, thr0, x1)


def _rope_tables(S):
    inv = 1.0 / (ROPE_THETA ** (np.arange(0, LANE, 2, dtype=np.float64) / LANE))
    ang = np.arange(S, dtype=np.float64)[:, None] * inv[None, :]
    cos, sin = np.cos(ang), np.sin(ang)
    return (jnp.asarray(np.concatenate([cos, cos], axis=1), F32),
            jnp.asarray(np.concatenate([-sin, sin], axis=1), F32))


def _layer(x, mem, attn_norm_g, mem_norm_g, ffn_norm_g, w_in, moba_q_norm_g, moba_k_norm_g,
           dn_conv_w, dn_a_log, dn_dt_bias, dn_out_norm_g, w_mem_kv, mem_q_norm_g, mem_k_norm_g,
           w_branch, w_out, peer_w_q, peer_keys1, peer_keys2, peer_u, peer_v):
    B, S, D = x.shape
    T = B * S
    x2 = x.reshape(T, D)
    row = lambda g: g.reshape(1, -1)

    moba_w, dn_w = 3 * MOBA_HEADS * LANE, 3 * DN_HEADS * LANE
    n_small = 2 * DN_HEADS
    o_small = moba_w + dn_w
    qk_w = 2 * MOBA_HEADS * LANE
    o_z = o_small + n_small
    w_f32 = jnp.concatenate([w_in[:, :qk_w], w_in[:, moba_w:o_small], w_in[:, o_z + D:o_z + 2 * D]],
                            axis=1).astype(BF16)
    w_b16 = jnp.concatenate([w_in[:, qk_w:moba_w], w_in[:, o_z:o_z + D], w_in[:, o_z + 2 * D:]],
                            axis=1).astype(BF16)
    w_small = jnp.pad(w_in[:, o_small:o_z], ((0, 0), (0, LANE - n_small)))
    proj, small = _norm_matmul(x2, row(attn_norm_g), w_f32, tm=1024, tn=1024, w_side=w_small, name="in_proj")
    projb = _norm_matmul(x2, row(attn_norm_g), w_b16, tm=1024, tn=1024, out_dtype=BF16, name="in_proj_bf16")
    proj3 = proj.reshape(B, S, -1)
    projb3 = projb.reshape(B, S, -1)
    dn_col = qk_w // LANE
    memq_col = (qk_w + dn_w) // MEM_HEAD_DIM
    z_col = 1
    gate_col = 2

    cos2, sin2 = _rope_tables(S)
    blk_onehot = jnp.asarray(np.arange(S)[:, None] // MOBA_BLOCK == np.arange(LANE)[None, :], BF16)
    mq, mk, kmean = _moba_prep(proj3, row(moba_q_norm_g), row(moba_k_norm_g), cos2, sin2, blk_onehot)
    moba_lo, moba_hi = _moba_attn(_moba_gate(mq, kmean), mk, projb3)

    dn_qkv = _dn_prep(proj3, dn_conv_w, dn_col)
    dn_out = _dn_scan(dn_qkv, small.reshape(B, S, LANE), dn_a_log, dn_dt_bias, projb3, z_col,
                      row(dn_out_norm_g))

    M = mem.shape[1]
    kv = _norm_matmul(mem.reshape(B * M, D), row(mem_norm_g), w_mem_kv.astype(BF16), tm=B * M, tn=512, name="mem_kv")
    mem_out = _mem_attn(proj3, kv.reshape(B, M, -1), row(mem_q_norm_g), row(mem_k_norm_g), memq_col)

    x1 = _merge(x2, moba_lo.reshape(T // 2, D), moba_hi.reshape(T // 2, D), dn_out.reshape(T, D),
                mem_out.reshape(T, D), projb, gate_col, w_branch.astype(BF16), w_out.astype(BF16), S)

    qp, h2 = _norm_matmul(x1, row(ffn_norm_g), peer_w_q.astype(BF16), tm=1024, tn=1024, emit_h=True, name="peer_query")
    s1, s2, tau, cc, v2, thr0 = _peer_select(qp, peer_keys1, peer_keys2)
    out = _peer_dense(h2, peer_u.astype(BF16), peer_v.astype(BF16), s1, s2, tau, cc, v2, thr0, x1)
    return out.reshape(B, S, D)


def kernel(x, mem, attn_norm_g, mem_norm_g, ffn_norm_g, w_in, moba_q_norm_g, moba_k_norm_g, dn_conv_w, dn_a_log, dn_dt_bias, dn_out_norm_g, w_mem_kv, mem_q_norm_g, mem_k_norm_g, w_branch, w_out, peer_w_q, peer_keys1, peer_keys2, peer_u, peer_v):
    for l in range(w_in.shape[0]):
        x = _layer(x, mem, attn_norm_g[l], mem_norm_g[l], ffn_norm_g[l], w_in[l], moba_q_norm_g[l],
                   moba_k_norm_g[l], dn_conv_w[l], dn_a_log[l], dn_dt_bias[l], dn_out_norm_g[l],
                   w_mem_kv[l], mem_q_norm_g[l], mem_k_norm_g[l], w_branch[l], w_out[l], peer_w_q[l],
                   peer_keys1[l], peer_keys2[l], peer_u[l], peer_v[l])
    return x
```
